```python
import jax, jax.numpy as jnp
from jax import lax
import numpy as np

D_MODEL = 1024
BATCH = 16
SEQ = 2048
DEPTH = 4

GRID_W = 64
CTX_LEN = 256
D_MIX = 1024
HEAD_DIM = 64
EPS = 1e-6
A_GROUPS = 4
A_WIDTH = A_GROUPS * HEAD_DIM
CHUNK = 128
B_HEADS = 8
B_KV_HEADS = 2
B_GROUP = B_HEADS // B_KV_HEADS
B_WIDTH = B_HEADS * HEAD_DIM
B_KV_WIDTH = B_KV_HEADS * HEAD_DIM
Q_BLOCK = 128
ATTN_SCALE = HEAD_DIM ** -0.5
ROPE_THETA = 10000.0
HALF_ROT = HEAD_DIM // 2
ROPE_FREQS = HALF_ROT // 2
C_HEADS = 4
C_WIDTH = C_HEADS * HEAD_DIM
M_CHUNK = 128
F_BIAS_LO = 3.0
F_BIAS_HI = 6.0
SPLIT_SIZES = (A_WIDTH, A_WIDTH, A_WIDTH,
               B_WIDTH, B_KV_WIDTH, B_KV_WIDTH, B_WIDTH,
               C_WIDTH, C_WIDTH, C_WIDTH, C_WIDTH, C_WIDTH, 4 * C_HEADS)
D_IN = 3 * A_WIDTH + 2 * B_WIDTH + 2 * B_KV_WIDTH + 5 * C_WIDTH + 4 * C_HEADS

kernel_name = 'hybrid_gmlp_gqa_mlstm_dit_block'


def _rms_norm(x, g):
    xf = x.astype(jnp.float32)
    y = xf * lax.rsqrt(jnp.mean(xf * xf, axis=-1, keepdims=True) + EPS)
    return (y * g.astype(jnp.float32)).astype(x.dtype)


def _layer_norm(x):
    xf = x.astype(jnp.float32)
    mu = jnp.mean(xf, axis=-1, keepdims=True)
    var = jnp.mean(jnp.square(xf - mu), axis=-1, keepdims=True)
    return ((xf - mu) * lax.rsqrt(var + EPS)).astype(x.dtype)


def _split_proj(p):
    idx = [int(i) for i in np.cumsum(SPLIT_SIZES)[:-1]]
    return jnp.split(p, idx, axis=-1)


def _rope_tables(rows_n):
    row = jnp.repeat(jnp.arange(rows_n, dtype=jnp.int32), GRID_W)
    col = jnp.tile(jnp.arange(GRID_W, dtype=jnp.int32), rows_n)
    freqs = ROPE_THETA ** (-jnp.arange(ROPE_FREQS, dtype=jnp.float32) / ROPE_FREQS)
    ang_r = row.astype(jnp.float32)[:, None] * freqs
    ang_c = col.astype(jnp.float32)[:, None] * freqs
    return (jnp.cos(ang_r), jnp.sin(ang_r), jnp.cos(ang_c), jnp.sin(ang_c))


def _apply_rope(x, tabs):
    cos_r, sin_r, cos_c, sin_c = tabs
    shp = (1, x.shape[1]) + (1,) * (x.ndim - 3) + (ROPE_FREQS,)
    xf = x.astype(jnp.float32)

    def rot(xh, cos, sin):
        cos = cos.reshape(shp)
        sin = sin.reshape(shp)
        x1, x2 = xh[..., :ROPE_FREQS], xh[..., ROPE_FREQS:]
        return jnp.concatenate([x1 * cos - x2 * sin, x2 * cos + x1 * sin], axis=-1)

    out = jnp.concatenate([rot(xf[..., :HALF_ROT], cos_r, sin_r),
                           rot(xf[..., HALF_ROT:], cos_c, sin_c)], axis=-1)
    return out.astype(x.dtype)


def _attend(q, k, v):
    s = jnp.einsum('bqkgd,bskd->bkgqs', q, k).astype(jnp.float32) * ATTN_SCALE
    p = jax.nn.softmax(s, axis=-1).astype(v.dtype)
    return jnp.einsum('bkgqs,bskd->bqkgd', p, v)


def _gmlp_branch(u, v, z, w_s, b_s):
    B_, T, _ = u.shape
    shp = (B_, T // CHUNK, CHUNK, A_GROUPS, HEAD_DIM)
    vn = _layer_norm(jax.nn.gelu(v).reshape(shp))
    sv = jnp.einsum('gpq,bcqgd->bcpgd', w_s, vn) + b_s.T[:, :, None]
    return (jax.nn.gelu(u).reshape(shp) * sv).reshape(B_, T, A_WIDTH) * jax.nn.silu(z)


def _mlstm_scan(q, k, v, log_i, log_f, state, emit_h):
    B_, T, H, D = q.shape
    nc = T // M_CHUNK

    def chunks(a):
        return a.reshape((B_, nc, M_CHUNK) + a.shape[2:]).swapaxes(0, 1)

    xs = (chunks(q), chunks(k), chunks(v), chunks(log_i), chunks(log_f))
    lower = jnp.tril(jnp.ones((M_CHUNK, M_CHUNK), dtype=bool))[None, :, :, None]

    def step(carry, inp):
        C, n, m = carry
        qc, kc, vc, li, lf = inp
        qf = qc.astype(jnp.float32)
        kf = kc.astype(jnp.float32) * (D ** -0.5)
        vf = vc.astype(jnp.float32)
        b = jnp.cumsum(lf, axis=1)
        b_end = b[:, -1]
        w_end = b_end[:, None] - b + li
        m_new = jnp.maximum(b_end + m, jnp.max(w_end, axis=1))
        decay = jnp.exp(b_end + m - m_new)
        wk = jnp.exp(w_end - m_new[:, None])
        C_new = decay[..., None, None] * C + jnp.einsum('blh,blhd,blhe->bhde', wk, vf, kf)
        n_new = decay[..., None] * n + jnp.einsum('blh,blhe->bhe', wk, kf)
        h = None
        if emit_h:
            a = b + m[:, None]
            dmat = b[:, :, None] - b[:, None] + li[:, None]
            dmat = jnp.where(lower, dmat, -jnp.inf)
            m_t = jnp.maximum(a, jnp.max(dmat, axis=2))
            inter = jnp.exp(a - m_t)
            s = jnp.einsum('bthd,bshd->btsh', qf, kf) * jnp.exp(dmat - m_t[:, :, None])
            num = jnp.einsum('btsh,bshd->bthd', s, vf) + inter[..., None] * jnp.einsum('bhde,bthe->bthd', C, qf)
            den = jnp.sum(s, axis=2) + inter * jnp.einsum('bhe,bthe->bth', n, qf)
            h = num / jnp.maximum(jnp.abs(den), jnp.exp(-m_t))[..., None]
        return (C_new, n_new, m_new), h

    state, hs = lax.scan(step, state, xs)
    h = hs.swapaxes(0, 1).reshape(B_, T, H, D).astype(q.dtype) if emit_h else None
    return h, state


def _layer(x, ctx, mod_lat, mod_ctx, tabs, g_norm, w_in, w_s, b_s, g_q, g_k, b_gates, g_head, w_out,
           with_ctx_out):
    B_, S, _ = x.shape
    Lc = ctx.shape[1]
    sh, sc, gt = jnp.split(mod_lat, 3, axis=-1)
    sh_c, sc_c, gt_c = jnp.split(mod_ctx, 3, axis=-1)
    xn = _rms_norm(x, g_norm) * (1 + sc[:, None]) + sh[:, None]
    cn = _rms_norm(ctx, g_norm) * (1 + sc_c) + sh_c
    (au, av, az, bq, bk, bv, bz, cq, ck, cv, co, cz, cg) = _split_proj(xn @ w_in)
    (au_c, av_c, az_c, bq_c, bk_c, bv_c, bz_c, cq_c, ck_c, cv_c, co_c, cz_c, cg_c) = _split_proj(cn @ w_in)

    def heads_q(t):
        return _rms_norm(t.reshape(t.shape[:2] + (B_KV_HEADS, B_GROUP, HEAD_DIM)), g_q)

    def heads_k(t):
        return _rms_norm(t.reshape(t.shape[:2] + (B_KV_HEADS, HEAD_DIM)), g_k)

    q_lat = _apply_rope(heads_q(bq), tabs)
    k_lat = _apply_rope(heads_k(bk), tabs)
    v_lat = bv.reshape(B_, S, B_KV_HEADS, HEAD_DIM)
    k_ctx = heads_k(bk_c)
    v_ctx = bv_c.reshape(B_, Lc, B_KV_HEADS, HEAD_DIM)
    k_all = jnp.concatenate([k_ctx, k_lat], axis=1)
    v_all = jnp.concatenate([v_ctx, v_lat], axis=1)
    q_blocks = q_lat.reshape(B_, S // Q_BLOCK, Q_BLOCK, B_KV_HEADS, B_GROUP, HEAD_DIM).swapaxes(0, 1)
    o_lat = lax.map(lambda qb: _attend(qb, k_all, v_all), q_blocks)
    y_b = o_lat.swapaxes(0, 1).reshape(B_, S, B_WIDTH) * jax.nn.silu(bz)

    def mheads(t):
        return t.reshape(t.shape[:2] + (C_HEADS, HEAD_DIM))

    def gates(g):
        g = g.astype(jnp.float32) + b_gates.astype(jnp.float32)
        i_f, f_f, i_b, f_b = jnp.split(g, 4, axis=-1)
        return i_f, jax.nn.log_sigmoid(f_f), i_b, jax.nn.log_sigmoid(f_b)

    def flip(t):
        return jnp.flip(t, axis=1)

    ql, kl, vl = mheads(cq), mheads(ck), mheads(cv)
    qc_, kc_, vc_ = mheads(cq_c), mheads(ck_c), mheads(cv_c)
    li_f, lf_f, li_b, lf_b = gates(cg)
    ci_f, cf_f, ci_b, cf_b = gates(cg_c)
    st0 = (jnp.zeros((B_, C_HEADS, HEAD_DIM, HEAD_DIM), jnp.float32),
           jnp.zeros((B_, C_HEADS, HEAD_DIM), jnp.float32),
           jnp.zeros((B_, C_HEADS), jnp.float32))
    h_cf, st_f = _mlstm_scan(qc_, kc_, vc_, ci_f, cf_f, st0, with_ctx_out)
    h_cb, st_b = _mlstm_scan(flip(qc_), flip(kc_), flip(vc_), flip(ci_b), flip(cf_b), st0, with_ctx_out)
    h_lf, _ = _mlstm_scan(ql, kl, vl, li_f, lf_f, st_f, True)
    h_lb, _ = _mlstm_scan(flip(ql), flip(kl), flip(vl), flip(li_b), flip(lf_b), st_b, True)
    g_hd = g_head.reshape(C_HEADS, HEAD_DIM)
    h_lat = _rms_norm(h_lf + flip(h_lb), g_hd).reshape(B_, S, C_WIDTH)
    y_c = jax.nn.sigmoid(co) * h_lat * jax.nn.silu(cz)

    y_a = _gmlp_branch(au, av, az, w_s, b_s)

    x = x + gt[:, None] * (jnp.concatenate([y_a, y_b, y_c], axis=-1) @ w_out)

    if with_ctx_out:
        o_ctx = _attend(heads_q(bq_c), k_ctx, v_ctx).reshape(B_, Lc, B_WIDTH)
        y_b_c = o_ctx * jax.nn.silu(bz_c)
        h_ctx = _rms_norm(h_cf + flip(h_cb), g_hd).reshape(B_, Lc, C_WIDTH)
        y_c_c = jax.nn.sigmoid(co_c) * h_ctx * jax.nn.silu(cz_c)
        y_a_c = _gmlp_branch(au_c, av_c, az_c, w_s, b_s)
        ctx = ctx + gt_c * (jnp.concatenate([y_a_c, y_b_c, y_c_c], axis=-1) @ w_out)
    return x, ctx


def setup_inputs(seed: int = 0) -> dict:
    key = jax.random.key(seed)
    ks = jax.random.split(key, 16)
    f32 = jnp.float32

    def nrm(k, shape, scale):
        return jax.random.normal(k, shape, f32) * scale

    f_bias = jnp.linspace(F_BIAS_LO, F_BIAS_HI, C_HEADS, dtype=f32)
    zeros_h = jnp.zeros((C_HEADS,), f32)
    gate_base = jnp.concatenate([zeros_h, f_bias, zeros_h, f_bias])
    return {
        'x': nrm(ks[0], (BATCH, SEQ, D_MODEL), 1.0),
        'c': nrm(ks[1], (BATCH, D_MODEL), 1.0),
        'ctx': nrm(ks[2], (BATCH, CTX_LEN, D_MODEL), 1.0),
        'c_ctx': nrm(ks[3], (D_MODEL,), 1.0),
        'w_ada': nrm(ks[4], (DEPTH, D_MODEL, 3 * D_MODEL), 0.5 * D_MODEL ** -0.5),
        'b_ada': nrm(ks[5], (DEPTH, 3 * D_MODEL), 0.01),
        'g_norm': 1.0 + nrm(ks[6], (DEPTH, D_MODEL), 0.02),
        'w_in': nrm(ks[7], (DEPTH, D_MODEL, D_IN), D_MODEL ** -0.5),
        'w_s': nrm(ks[8], (DEPTH, A_GROUPS, CHUNK, CHUNK), CHUNK ** -0.5),
        'b_s': 1.0 + nrm(ks[9], (DEPTH, A_GROUPS, CHUNK), 0.02),
        'g_q': 1.0 + nrm(ks[10], (DEPTH, HEAD_DIM), 0.02),
        'g_k': 1.0 + nrm(ks[11], (DEPTH, HEAD_DIM), 0.02),
        'b_gates': gate_base + nrm(ks[12], (DEPTH, 4 * C_HEADS), 0.1),
        'g_head': 1.0 + nrm(ks[13], (DEPTH, C_WIDTH), 0.02),
        'w_out': nrm(ks[14], (DEPTH, D_MIX, D_MODEL), D_MIX ** -0.5),
        'g_final': 1.0 + nrm(ks[15], (D_MODEL,), 0.02),
    }


def reference(x, c, ctx, c_ctx, w_ada, b_ada, g_norm, w_in, w_s, b_s, g_q, g_k, b_gates, g_head, w_out,
              g_final):
    ROWS = x.shape[1] // GRID_W
    tabs = _rope_tables(ROWS)
    silu_c = jax.nn.silu(c)
    silu_cc = jax.nn.silu(c_ctx)
    for l in range(DEPTH):
        mod_lat = silu_c @ w_ada[l] + b_ada[l]
        mod_ctx = silu_cc @ w_ada[l] + b_ada[l]
        x, ctx = _layer(x, ctx, mod_lat, mod_ctx, tabs, g_norm[l], w_in[l], w_s[l], b_s[l], g_q[l], g_k[l],
                        b_gates[l], g_head[l], w_out[l], with_ctx_out=(l < DEPTH - 1))
    return _rms_norm(x, g_final)
```

```python
import functools

import jax
import jax.numpy as jnp
import numpy as np
from jax import lax
from jax.experimental import pallas as pl
from jax.experimental.pallas import tpu as pltpu

HEAD_DIM = 64
LANES = 128
GRID_W = 64
EPS = 1e-6
A_GROUPS = 4
A_WIDTH = A_GROUPS * HEAD_DIM
CHUNK = 128
B_HEADS = 8
B_KV_HEADS = 2
B_GROUP = B_HEADS // B_KV_HEADS
B_WIDTH = B_HEADS * HEAD_DIM
B_KV_WIDTH = B_KV_HEADS * HEAD_DIM
ATTN_SCALE = HEAD_DIM ** -0.5
ROPE_THETA = 10000.0
ROPE_FREQS = HEAD_DIM // 4
C_HEADS = 4
C_WIDTH = C_HEADS * HEAD_DIM
M_CHUNK = 128
N_GATES = 4 * C_HEADS
GATE_ROWS = 8

OFF_AU = 0
OFF_AV = OFF_AU + A_WIDTH
OFF_AZ = OFF_AV + A_WIDTH
OFF_BQ = OFF_AZ + A_WIDTH
OFF_BK = OFF_BQ + B_WIDTH
OFF_BV = OFF_BK + B_KV_WIDTH
OFF_BZ = OFF_BV + B_KV_WIDTH
OFF_CQ = OFF_BZ + B_WIDTH
OFF_CK = OFF_CQ + C_WIDTH
OFF_CV = OFF_CK + C_WIDTH
OFF_CO = OFF_CV + C_WIDTH
OFF_CZ = OFF_CO + C_WIDTH
OFF_CG = OFF_CZ + C_WIDTH
D_IN = OFF_CG + N_GATES

VMEM_LIMIT_BYTES = 56 * 1024 * 1024

F32 = jnp.float32
BF16 = jnp.bfloat16
HIGHEST = lax.Precision.HIGHEST


def _dot(a, b):
    return jnp.dot(a, b, preferred_element_type=F32)


def _dot_nt(a, b):
    return lax.dot_general(a, b, (((1,), (1,)), ((), ())), preferred_element_type=F32)


def _dot_tn(a, b):
    return lax.dot_general(a, b, (((0,), (0,)), ((), ())), preferred_element_type=F32)


def _silu(x):
    return x * jax.nn.sigmoid(x)


def _gelu_tanh(x):
    c = np.float32(np.sqrt(2.0 / np.pi))
    return 0.5 * x * (1.0 + jnp.tanh(c * (x + np.float32(0.044715) * (x * x * x))))


def _log_sigmoid(x):
    return jnp.minimum(x, 0.0) - jnp.log1p(jnp.exp(-jnp.abs(x)))


def _seg_mean(x, pm_ref):
    hi = x.astype(BF16)
    lo = (x - hi.astype(F32)).astype(BF16)
    w = x.shape[-1]
    pm = pm_ref[0:w, 0:w]
    return _dot(hi, pm) + _dot(lo, pm)


def _lane_iota(shape):
    return lax.broadcasted_iota(jnp.int32, shape, len(shape) - 1)


def _mod_kernel(cc_ref, w_ref, b_ref, o_ref):
    s = _silu(cc_ref[...])
    o_ref[0] = jnp.dot(s, w_ref[0], preferred_element_type=F32, precision=HIGHEST) + b_ref[0]


def _modulation(cc, w_ada, b_ada):
    depth, d, d3 = w_ada.shape
    rows = cc.shape[0]
    tn = d
    return pl.pallas_call(
        _mod_kernel,
        grid=(depth, d3 // tn),
        in_specs=[
            pl.BlockSpec((rows, d), lambda l, n: (0, 0)),
            pl.BlockSpec((1, d, tn), lambda l, n: (l, 0, n)),
            pl.BlockSpec((1, 1, tn), lambda l, n: (l, 0, n)),
        ],
        out_specs=pl.BlockSpec((1, rows, tn), lambda l, n: (l, 0, n)),
        out_shape=jax.ShapeDtypeStruct((depth, rows, d3), F32),
        compiler_params=pltpu.CompilerParams(
            dimension_semantics=("arbitrary", "arbitrary"), vmem_limit_bytes=VMEM_LIMIT_BYTES),
        name="modulation",
    )(cc, w_ada, b_ada.reshape(depth, 1, d3))


def _proj_kernel(h_ref, sc_ref, sh_ref, gn_ref, w_ref, wgc_ref, wgr_ref, pm_ref, ws_ref, bs_ref,
                 gq_ref, gk_ref, cos_ref, sin_ref,
                 ya_ref, q_ref, k_ref, vt_ref, zb_ref, cq_ref, ck_ref, cv_ref, og_ref, gc_ref, gr_ref):
    tm = h_ref.shape[1]
    x = h_ref[0]
    ms = jnp.mean(x * x, axis=-1, keepdims=True)
    xn = (x * lax.rsqrt(ms + EPS)) * gn_ref[...]
    xn = xn * (1.0 + sc_ref[0]) + sh_ref[0]
    xb = xn.astype(BF16)

    def proj(off, width):
        return _dot(xb, w_ref[:, off:off + width])

    lane = _lane_iota((tm, LANES))
    lo_half = lane < HEAD_DIM

    gu = _gelu_tanh(proj(OFF_AU, A_WIDTH))
    gv = _gelu_tanh(proj(OFF_AV, A_WIDTH))
    mu = _seg_mean(gv, pm_ref)
    dv = gv - mu
    var = _seg_mean(dv * dv, pm_ref)
    vn = (dv * lax.rsqrt(var + EPS)).astype(BF16)
    sz = _silu(proj(OFF_AZ, A_WIDTH))
    lo_half_chunk = _lane_iota((CHUNK, LANES)) < HEAD_DIM
    for c in range(tm // CHUNK):
        r0, r1 = c * CHUNK, (c + 1) * CHUNK
        for j in range(A_WIDTH // LANES):
            c0, c1 = j * LANES, (j + 1) * LANES
            vs = vn[r0:r1, c0:c1]
            ev = _dot(ws_ref[2 * j], vs)
            od = _dot(ws_ref[2 * j + 1], vs)
            sv = jnp.where(lo_half_chunk, ev, od) + bs_ref[:, c0:c1]
            ya_ref[0, r0:r1, c0:c1] = (gu[r0:r1, c0:c1] * sv * sz[r0:r1, c0:c1]).astype(BF16)

    cos = cos_ref[...]
    sin = sin_ref[...]
    bit4 = (lane & ROPE_FREQS) != 0

    def rope(t):
        partner = jnp.where(bit4, pltpu.roll(t, ROPE_FREQS, 1), pltpu.roll(t, LANES - ROPE_FREQS, 1))
        return t * cos + partner * sin

    q = proj(OFF_BQ, B_WIDTH)
    qn = q * lax.rsqrt(_seg_mean(q * q, pm_ref) + EPS) * gq_ref[...]
    for j in range(B_WIDTH // LANES):
        qs = rope(qn[:, j * LANES:(j + 1) * LANES]) * ATTN_SCALE
        sw = pltpu.roll(qs, HEAD_DIM, 1)
        if (2 * j) // B_GROUP == 0:
            h_even = jnp.where(lo_half, qs, 0.0)
            h_odd = jnp.where(lo_half, sw, 0.0)
        else:
            h_even = jnp.where(lo_half, 0.0, sw)
            h_odd = jnp.where(lo_half, 0.0, qs)
        q_ref[0, :, (2 * j) * LANES:(2 * j + 1) * LANES] = h_even.astype(BF16)
        q_ref[0, :, (2 * j + 1) * LANES:(2 * j + 2) * LANES] = h_odd.astype(BF16)

    k = proj(OFF_BK, B_KV_WIDTH)
    kn = k * lax.rsqrt(_seg_mean(k * k, pm_ref) + EPS) * gk_ref[...]
    k_ref[0] = rope(kn).astype(BF16)
    vt_ref[0] = proj(OFF_BV, B_KV_WIDTH).T.astype(BF16)
    zb_ref[0] = _silu(proj(OFF_BZ, B_WIDTH))

    ones_col = jnp.where(lane == HEAD_DIM, 1.0, 0.0)
    cq = proj(OFF_CQ, C_WIDTH)
    ck = proj(OFF_CK, C_WIDTH) * (HEAD_DIM ** -0.5)
    cv = proj(OFF_CV, C_WIDTH)
    for j in range(C_WIDTH // LANES):
        sl = slice(j * LANES, (j + 1) * LANES)
        e_sl = slice((2 * j) * LANES, (2 * j + 1) * LANES)
        o_sl = slice((2 * j + 1) * LANES, (2 * j + 2) * LANES)
        for src, dst, fill in ((cq, cq_ref, None), (ck, ck_ref, None), (cv, cv_ref, ones_col)):
            s = src[:, sl]
            pad = 0.0 if fill is None else fill
            dst[0, :, e_sl] = jnp.where(lo_half, s, pad).astype(BF16)
            dst[0, :, o_sl] = jnp.where(lo_half, pltpu.roll(s, HEAD_DIM, 1), pad).astype(BF16)
    og_ref[0] = jax.nn.sigmoid(proj(OFF_CO, C_WIDTH)) * _silu(proj(OFF_CZ, C_WIDTH))
    gc_ref[0] = _dot(xb, wgc_ref[...])
    gr = _dot_nt(wgr_ref[...], xb)
    for c in range(tm // CHUNK):
        gr_ref[0, c] = gr[:, c * CHUNK:(c + 1) * CHUNK]


def _proj_call(h, sc, sh, lw, consts, n_ctx_tiles, tm):
    b, t, d = h.shape
    nt = t // tm
    n_mod = sc.shape[0]

    def mod_map(bi, i):
        return (jnp.where(i < n_ctx_tiles, n_mod - 1, bi), 0, 0)

    def tile3(w):
        return pl.BlockSpec((1, tm, w), lambda bi, i: (bi, i, 0))

    def whole(a):
        nd = a.ndim
        return pl.BlockSpec(a.shape, lambda bi, i: (0,) * nd)

    ins = [h, sc, sh, lw["g_norm"], lw["w_main"], lw["w_gc"], lw["w_gr"], consts["pm"], lw["w_s"],
           lw["b_s"], lw["g_q"], lw["g_k"], consts["cos"], consts["sin"]]
    in_specs = [tile3(d),
                pl.BlockSpec((1, 1, d), mod_map), pl.BlockSpec((1, 1, d), mod_map),
                whole(lw["g_norm"]), whole(lw["w_main"]), whole(lw["w_gc"]), whole(lw["w_gr"]),
                whole(consts["pm"]), whole(lw["w_s"]), whole(lw["b_s"]), whole(lw["g_q"]),
                whole(lw["g_k"]),
                pl.BlockSpec((tm, LANES), lambda bi, i: (i, 0)),
                pl.BlockSpec((tm, LANES), lambda bi, i: (i, 0))]
    pad_w = C_HEADS * LANES
    out_shape = [
        jax.ShapeDtypeStruct((b, t, A_WIDTH), BF16),
        jax.ShapeDtypeStruct((b, t, B_HEADS * LANES), BF16),
        jax.ShapeDtypeStruct((b, t, B_KV_WIDTH), BF16),
        jax.ShapeDtypeStruct((b, B_KV_WIDTH, t), BF16),
        jax.ShapeDtypeStruct((b, t, B_WIDTH), F32),
        jax.ShapeDtypeStruct((b, t, pad_w), BF16),
        jax.ShapeDtypeStruct((b, t, pad_w), BF16),
        jax.ShapeDtypeStruct((b, t, pad_w), BF16),
        jax.ShapeDtypeStruct((b, t, C_WIDTH), F32),
        jax.ShapeDtypeStruct((b, t, LANES), F32),
        jax.ShapeDtypeStruct((b, t // CHUNK, 4 * GATE_ROWS, CHUNK), F32),
    ]
    out_specs = [tile3(A_WIDTH), tile3(B_HEADS * LANES), tile3(B_KV_WIDTH),
                 pl.BlockSpec((1, B_KV_WIDTH, tm), lambda bi, i: (bi, 0, i)),
                 tile3(B_WIDTH), tile3(pad_w), tile3(pad_w), tile3(pad_w), tile3(C_WIDTH),
                 tile3(LANES),
                 pl.BlockSpec((1, tm // CHUNK, 4 * GATE_ROWS, CHUNK), lambda bi, i: (bi, i, 0, 0))]
    return pl.pallas_call(
        _proj_kernel,
        grid=(b, nt),
        in_specs=in_specs,
        out_specs=out_specs,
        out_shape=out_shape,
        compiler_params=pltpu.CompilerParams(
            dimension_semantics=("arbitrary", "arbitrary"), vmem_limit_bytes=VMEM_LIMIT_BYTES),
        name="proj",
    )(*ins)


def _attn_kernel(q_ref, k_ref, vt_ref, zb_ref, y_ref, *, n_ctx_tiles, tile_off, ctx_len):
    tq = q_ref.shape[1]
    t_all = k_ref.shape[1]

    def attend(n_keys):
        kk = k_ref[0, 0:n_keys, :]
        vt = vt_ref[0, :, 0:n_keys]
        for j in range(B_HEADS // 2):
            halves = []
            for h in (2 * j, 2 * j + 1):
                g = h // B_GROUP
                s = _dot_nt(kk, q_ref[0, :, h * LANES:(h + 1) * LANES])
                m = jnp.max(s, axis=0, keepdims=True)
                p = jnp.exp(s - m)
                l = jnp.sum(p, axis=0, keepdims=True)
                o = _dot(vt, p.astype(BF16))
                halves.append(o[g * HEAD_DIM:(g + 1) * HEAD_DIM, :] / l)
            o_pair = jnp.concatenate(halves, axis=0).T
            sl = slice(j * LANES, (j + 1) * LANES)
            y_ref[0, :, sl] = (o_pair * zb_ref[0, :, sl]).astype(BF16)

    i = pl.program_id(1) + tile_off
    if n_ctx_tiles > 0 and tile_off == 0:
        @pl.when(i < n_ctx_tiles)
        def _():
            attend(ctx_len)

        @pl.when(i >= n_ctx_tiles)
        def _():
            attend(t_all)
    else:
        attend(t_all)


def _attn_call(q, k, vt, zb, n_ctx_tiles, tq, with_ctx):
    b, t, _ = q.shape
    nt = t // tq
    tile_off = 0 if with_ctx else n_ctx_tiles
    kern = functools.partial(_attn_kernel, n_ctx_tiles=n_ctx_tiles, tile_off=tile_off,
                             ctx_len=n_ctx_tiles * tq)
    return pl.pallas_call(
        kern,
        grid=(b, nt - tile_off),
        in_specs=[
            pl.BlockSpec((1, tq, B_HEADS * LANES), lambda bi, i: (bi, i + tile_off, 0)),
            pl.BlockSpec((1, t, B_KV_WIDTH), lambda bi, i: (bi, 0, 0)),
            pl.BlockSpec((1, B_KV_WIDTH, t), lambda bi, i: (bi, 0, 0)),
            pl.BlockSpec((1, tq, B_WIDTH), lambda bi, i: (bi, i + tile_off, 0)),
        ],
        out_specs=pl.BlockSpec((1, tq, B_WIDTH), lambda bi, i: (bi, i + tile_off, 0)),
        out_shape=jax.ShapeDtypeStruct((b, t, B_WIDTH), BF16),
        compiler_params=pltpu.CompilerParams(
            dimension_semantics=("arbitrary", "arbitrary"), vmem_limit_bytes=VMEM_LIMIT_BYTES),
        name="attention",
    )(q, k, vt, zb)


def _mlstm_kernel(cq_ref, ck_ref, cv_ref, gc_ref, gr_ref, og_ref, bc_ref, br_ref, gh_ref,
                  y_ref, hf_ref, hb_ref, st_ref, m_ref, *, n_ctx_chunks):
    t = cq_ref.shape[1]
    nc = t // M_CHUNK
    L = M_CHUNK
    row = lax.broadcasted_iota(jnp.int32, (L, L), 0)
    col = lax.broadcasted_iota(jnp.int32, (L, L), 1)
    masks = (col <= row, col >= row)
    tri_col = tuple(jnp.where(mk, 1.0, 0.0).astype(F32) for mk in masks)
    tri_row = tuple(jnp.where(mk, 1.0, 0.0).astype(F32) for mk in (row <= col, row >= col))

    st_ref[...] = jnp.zeros(st_ref.shape, F32)
    m_ref[...] = jnp.zeros(m_ref.shape, F32)
    bias_c = bc_ref[...]
    bias_r = br_ref[...]

    def step(i, carry):
        chunks = (i, jnp.where(i < n_ctx_chunks, n_ctx_chunks - 1 - i, nc + n_ctx_chunks - 1 - i))
        for d in range(2):
            c = chunks[d]
            r0 = pl.multiple_of(c * L, L)
            rows = pl.ds(r0, L)
            g_c = gc_ref[0, rows, :] + bias_c
            g_r = gr_ref[0, c] + bias_r
            lf_c = _log_sigmoid(g_c)
            li_r = g_r[2 * d * GATE_ROWS:(2 * d + 1) * GATE_ROWS]
            lf_r = _log_sigmoid(g_r[(2 * d + 1) * GATE_ROWS:(2 * d + 2) * GATE_ROWS])
            b_c = jnp.dot(tri_col[d], lf_c, preferred_element_type=F32, precision=HIGHEST)
            b_r = jnp.dot(lf_r, tri_row[d], preferred_element_type=F32, precision=HIGHEST)
            bend_c = jnp.sum(lf_c, axis=0, keepdims=True)
            bend_r = jnp.sum(lf_r, axis=1, keepdims=True)
            i_col0 = 2 * d * C_HEADS
            f_col0 = (2 * d + 1) * C_HEADS
            dst = hf_ref if d == 0 else hb_ref
            for hd in range(C_HEADS):
                sl = slice(hd * LANES, (hd + 1) * LANES)
                sidx = d * C_HEADS + hd
                qh = cq_ref[0, rows, sl]
                kh = ck_ref[0, rows, sl]
                vh = cv_ref[0, rows, sl]
                state = st_ref[sidx]
                m_old = m_ref[sidx:sidx + 1, 0:1]
                b_ch = b_c[:, f_col0 + hd:f_col0 + hd + 1]
                li_ch = g_c[:, i_col0 + hd:i_col0 + hd + 1]
                b_rh = b_r[hd:hd + 1, :]
                li_rh = li_r[hd:hd + 1, :]
                be_c = bend_c[:, f_col0 + hd:f_col0 + hd + 1]
                be_r = bend_r[hd:hd + 1, :]

                a_c = b_ch + m_old
                dm = jnp.where(masks[d], b_ch + (li_rh - b_rh), -jnp.inf)
                m_t = jnp.maximum(a_c, jnp.max(dm, axis=1, keepdims=True))
                e = jnp.exp(dm - m_t)
                s = _dot_nt(qh, kh) * e
                na = _dot(s.astype(BF16), vh) + jnp.exp(a_c - m_t) * _dot_nt(qh, state.astype(BF16))
                den = na[:, HEAD_DIM:HEAD_DIM + 1]
                dst[rows, sl] = na / jnp.maximum(jnp.abs(den), jnp.exp(-m_t))

                w_end_r = be_r - b_rh + li_rh
                m_new = jnp.maximum(be_r + m_old, jnp.max(w_end_r, axis=1, keepdims=True))
                decay = jnp.exp(be_r + m_old - m_new)
                wk_c = jnp.exp(be_c - b_ch + li_ch - m_new)
                upd = _dot_tn((wk_c * vh.astype(F32)).astype(BF16), kh)
                st_ref[sidx] = decay * state + upd
                m_ref[sidx:sidx + 1, :] = jnp.broadcast_to(m_new, (1, LANES))
        return carry

    lax.fori_loop(0, nc, step, 0)

    lane = _lane_iota((L, LANES))
    lo_half = lane < HEAD_DIM

    def combine(c, carry):
        rows = pl.ds(pl.multiple_of(c * L, L), L)
        normed = []
        for hd in range(C_HEADS):
            sl = slice(hd * LANES, (hd + 1) * LANES)
            hs = jnp.where(lo_half, hf_ref[rows, sl] + hb_ref[rows, sl], 0.0)
            ms = jnp.sum(hs * hs, axis=1, keepdims=True) * (1.0 / HEAD_DIM)
            normed.append(hs * lax.rsqrt(ms + EPS) * gh_ref[:, sl])
        for j in range(C_HEADS // 2):
            pair = jnp.where(lo_half, normed[2 * j], pltpu.roll(normed[2 * j + 1], HEAD_DIM, 1))
            sl = slice(j * LANES, (j + 1) * LANES)
            y_ref[0, rows, sl] = (og_ref[0, rows, sl] * pair).astype(BF16)
        return carry

    lax.fori_loop(0, nc, combine, 0)


def _mlstm_call(cq, ck, cv, gc, gr, og, lw, n_ctx_chunks):
    b, t, pad_w = cq.shape

    def per_b(a):
        nd = a.ndim
        return pl.BlockSpec((1,) + a.shape[1:], lambda bi: (bi,) + (0,) * (nd - 1))

    def whole(a):
        nd = a.ndim
        return pl.BlockSpec(a.shape, lambda bi: (0,) * nd)

    kern = functools.partial(_mlstm_kernel, n_ctx_chunks=n_ctx_chunks)
    return pl.pallas_call(
        kern,
        grid=(b,),
        in_specs=[per_b(cq), per_b(ck), per_b(cv), per_b(gc), per_b(gr), per_b(og),
                  whole(lw["bg_c"]), whole(lw["bg_r"]), whole(lw["g_head"])],
        out_specs=pl.BlockSpec((1, t, C_WIDTH), lambda bi: (bi, 0, 0)),
        out_shape=jax.ShapeDtypeStruct((b, t, C_WIDTH), BF16),
        scratch_shapes=[
            pltpu.VMEM((t, pad_w), F32),
            pltpu.VMEM((t, pad_w), F32),
            pltpu.VMEM((2 * C_HEADS, LANES, LANES), F32),
            pltpu.VMEM((2 * C_HEADS, LANES), F32),
        ],
        compiler_params=pltpu.CompilerParams(
            dimension_semantics=("arbitrary",), vmem_limit_bytes=VMEM_LIMIT_BYTES),
        name="mlstm",
    )(cq, ck, cv, gc, gr, og, lw["bg_c"], lw["bg_r"], lw["g_head"])


def _out_kernel(h_ref, gt_ref, ya_ref, yb_ref, yc_ref, w_ref, gf_ref, o_ref, *, final):
    acc = _dot(ya_ref[0], w_ref[0:A_WIDTH, :])
    acc += _dot(yb_ref[0], w_ref[A_WIDTH:A_WIDTH + B_WIDTH, :])
    acc += _dot(yc_ref[0], w_ref[A_WIDTH + B_WIDTH:, :])
    x = h_ref[0] + gt_ref[0] * acc
    if final:
        ms = jnp.mean(x * x, axis=-1, keepdims=True)
        x = (x * lax.rsqrt(ms + EPS)) * gf_ref[...]
    o_ref[0] = x


def _out_call(h, gt, ya, yb, yc, w_out, g_final, n_ctx_tiles, tm, final):
    b, t, d = h.shape
    nt = t // tm
    n_mod = gt.shape[0]
    off = n_ctx_tiles if final else 0

    def tile3(w):
        return pl.BlockSpec((1, tm, w), lambda bi, i: (bi, i + off, 0))

    def mod_map(bi, i):
        return (jnp.where(i + off < n_ctx_tiles, n_mod - 1, bi), 0, 0)

    if final:
        out_shape = jax.ShapeDtypeStruct((b, t - off * tm, d), F32)
        out_spec = pl.BlockSpec((1, tm, d), lambda bi, i: (bi, i, 0))
        aliases = {}
    else:
        out_shape = jax.ShapeDtypeStruct((b, t, d), F32)
        out_spec = tile3(d)
        aliases = {0: 0}
    return pl.pallas_call(
        functools.partial(_out_kernel, final=final),
        grid=(b, nt - off),
        in_specs=[tile3(d), pl.BlockSpec((1, 1, d), mod_map), tile3(A_WIDTH), tile3(B_WIDTH),
                  tile3(C_WIDTH),
                  pl.BlockSpec(w_out.shape, lambda bi, i: (0, 0)),
                  pl.BlockSpec(g_final.shape, lambda bi, i: (0, 0))],
        out_specs=out_spec,
        out_shape=out_shape,
        input_output_aliases=aliases,
        compiler_params=pltpu.CompilerParams(
            dimension_semantics=("arbitrary", "arbitrary"), vmem_limit_bytes=VMEM_LIMIT_BYTES),
        name="out_proj",
    )(h, gt, ya, yb, yc, w_out, g_final)


def _rope_tables(ctx_len, seq):
    pos = np.arange(seq)
    freqs = ROPE_THETA ** (-np.arange(ROPE_FREQS, dtype=np.float32) / ROPE_FREQS)
    ang_r = (pos // GRID_W).astype(np.float32)[:, None] * freqs
    ang_c = (pos % GRID_W).astype(np.float32)[:, None] * freqs
    cos_h = np.concatenate([np.cos(ang_r), np.cos(ang_r), np.cos(ang_c), np.cos(ang_c)], axis=1)
    sin_h = np.concatenate([-np.sin(ang_r), np.sin(ang_r), -np.sin(ang_c), np.sin(ang_c)], axis=1)
    cos = np.concatenate([np.ones((ctx_len, HEAD_DIM), np.float32), cos_h.astype(np.float32)], axis=0)
    sin = np.concatenate([np.zeros((ctx_len, HEAD_DIM), np.float32), sin_h.astype(np.float32)], axis=0)
    return jnp.asarray(np.tile(cos, (1, 2))), jnp.asarray(np.tile(sin, (1, 2)))


def _layer_weights(l, g_norm, w_in, w_s, b_s, g_q, g_k, b_gates, g_head, w_out):
    d = w_in.shape[1]
    w_l = w_in[l]
    w_g = w_l[:, OFF_CG:]
    w_gr = jnp.zeros((4, GATE_ROWS, d), F32).at[:, :C_HEADS, :].set(w_g.T.reshape(4, C_HEADS, d))
    bg_r = jnp.zeros((4, GATE_ROWS), F32).at[:, :C_HEADS].set(b_gates[l].reshape(4, C_HEADS))
    gh = jnp.pad(g_head[l].reshape(C_HEADS, HEAD_DIM), ((0, 0), (0, LANES - HEAD_DIM)))
    return {
        "g_norm": g_norm[l].reshape(1, d),
        "w_main": w_l[:, :OFF_CG].astype(BF16),
        "w_gc": jnp.pad(w_g, ((0, 0), (0, LANES - N_GATES))).astype(BF16),
        "w_gr": w_gr.reshape(4 * GATE_ROWS, d).astype(BF16),
        "w_s": w_s[l].astype(BF16),
        "b_s": jnp.repeat(b_s[l].T, HEAD_DIM, axis=1),
        "g_q": jnp.tile(g_q[l], B_HEADS).reshape(1, B_WIDTH),
        "g_k": jnp.tile(g_k[l], B_KV_HEADS).reshape(1, B_KV_WIDTH),
        "bg_c": jnp.pad(b_gates[l], (0, LANES - N_GATES)).reshape(1, LANES),
        "bg_r": bg_r.reshape(4 * GATE_ROWS, 1),
        "g_head": gh.reshape(1, C_HEADS * LANES),
        "w_out": w_out[l].astype(BF16),
    }


def kernel(x, c, ctx, c_ctx, w_ada, b_ada, g_norm, w_in, w_s, b_s, g_q, g_k, b_gates, g_head, w_out,
           g_final):
    b, seq, d = x.shape
    ctx_len = ctx.shape[1]
    depth = w_ada.shape[0]
    tm = 256 if (ctx_len % 256 == 0 and seq % 256 == 0) else CHUNK
    n_ctx_tiles = ctx_len // tm
    n_ctx_chunks = ctx_len // M_CHUNK

    n_mod = b + 1
    rows = -(-n_mod // 8) * 8
    cc = jnp.zeros((rows, d), F32).at[:b].set(c).at[b].set(c_ctx)
    mods = _modulation(cc, w_ada, b_ada)[:, :n_mod]
    mods = mods.reshape(depth, n_mod, 3, 1, d)

    cos, sin = _rope_tables(ctx_len, seq)
    seg = np.kron(np.eye(B_HEADS, dtype=np.float32), np.full((HEAD_DIM, HEAD_DIM), 1.0 / HEAD_DIM, np.float32))
    consts = {"cos": cos, "sin": sin, "pm": jnp.asarray(seg, BF16)}
    gf = g_final.reshape(1, d)

    h = jnp.concatenate([ctx, x], axis=1)
    for l in range(depth):
        final = l == depth - 1
        lw = _layer_weights(l, g_norm, w_in, w_s, b_s, g_q, g_k, b_gates, g_head, w_out)
        sh, sc, gt = mods[l, :, 0], mods[l, :, 1], mods[l, :, 2]
        (ya, q, k, vt, zb, cq, ck, cv, og, gc, gr) = _proj_call(h, sc, sh, lw, consts, n_ctx_tiles, tm)
        yb = _attn_call(q, k, vt, zb, n_ctx_tiles, tm, with_ctx=not final)
        yc = _mlstm_call(cq, ck, cv, gc, gr, og, lw, n_ctx_chunks)
        h = _out_call(h, gt, ya, yb, yc, lw["w_out"], gf, n_ctx_tiles, tm, final)
    return h
```

```python
import functools

import jax
import jax.numpy as jnp
import numpy as np
from jax import lax
from jax.experimental import pallas as pl
from jax.experimental.pallas import tpu as pltpu

HEAD_DIM = 64
LANES = 128
GRID_W = 64
EPS = 1e-6
A_GROUPS = 4
A_WIDTH = A_GROUPS * HEAD_DIM
CHUNK = 128
B_HEADS = 8
B_KV_HEADS = 2
B_GROUP = B_HEADS // B_KV_HEADS
B_WIDTH = B_HEADS * HEAD_DIM
B_KV_WIDTH = B_KV_HEADS * HEAD_DIM
ATTN_SCALE = HEAD_DIM ** -0.5
ROPE_THETA = 10000.0
ROPE_FREQS = HEAD_DIM // 4
C_HEADS = 4
C_WIDTH = C_HEADS * HEAD_DIM
M_CHUNK = 128
N_GATES = 4 * C_HEADS
GATE_ROWS = 8

OFF_AU = 0
OFF_AV = OFF_AU + A_WIDTH
OFF_AZ = OFF_AV + A_WIDTH
OFF_BQ = OFF_AZ + A_WIDTH
OFF_BK = OFF_BQ + B_WIDTH
OFF_BV = OFF_BK + B_KV_WIDTH
OFF_BZ = OFF_BV + B_KV_WIDTH
OFF_CQ = OFF_BZ + B_WIDTH
OFF_CK = OFF_CQ + C_WIDTH
OFF_CV = OFF_CK + C_WIDTH
OFF_CO = OFF_CV + C_WIDTH
OFF_CZ = OFF_CO + C_WIDTH
OFF_CG = OFF_CZ + C_WIDTH
D_IN = OFF_CG + N_GATES

VMEM_LIMIT_BYTES = 56 * 1024 * 1024

F32 = jnp.float32
BF16 = jnp.bfloat16
GATE_DTYPE = BF16
HIGHEST = lax.Precision.HIGHEST

QK_PRESCALE = float(ATTN_SCALE * np.log2(np.e))
SAFE_LOGIT_BOUND = 40.0


def _dot(a, b):
    return jnp.dot(a, b, preferred_element_type=F32)


def _dot_nt(a, b):
    return lax.dot_general(a, b, (((1,), (1,)), ((), ())), preferred_element_type=F32)


def _dot_tn(a, b):
    return lax.dot_general(a, b, (((0,), (0,)), ((), ())), preferred_element_type=F32)


def _silu(x):
    return x * jax.nn.sigmoid(x)


def _gelu_tanh(x):
    c = np.float32(np.sqrt(2.0 / np.pi))
    return 0.5 * x * (1.0 + jnp.tanh(c * (x + np.float32(0.044715) * (x * x * x))))


def _log_sigmoid(x):
    return jnp.minimum(x, 0.0) - jnp.log1p(jnp.exp(-jnp.abs(x)))


def _seg_mean(x, pm_ref):
    hi = x.astype(BF16)
    lo = (x - hi.astype(F32)).astype(BF16)
    w = x.shape[-1]
    pm = pm_ref[0:w, 0:w]
    return _dot(hi, pm) + _dot(lo, pm)


def _lane_iota(shape):
    return lax.broadcasted_iota(jnp.int32, shape, len(shape) - 1)


def _mod_kernel(cc_ref, w_ref, b_ref, o_ref):
    s = _silu(cc_ref[...])
    o_ref[0] = jnp.dot(s, w_ref[0], preferred_element_type=F32, precision=HIGHEST) + b_ref[0]


def _modulation(cc, w_ada, b_ada):
    depth, d, d3 = w_ada.shape
    rows = cc.shape[0]
    tn = d
    return pl.pallas_call(
        _mod_kernel,
        grid=(depth, d3 // tn),
        in_specs=[
            pl.BlockSpec((rows, d), lambda l, n: (0, 0)),
            pl.BlockSpec((1, d, tn), lambda l, n: (l, 0, n)),
            pl.BlockSpec((1, 1, tn), lambda l, n: (l, 0, n)),
        ],
        out_specs=pl.BlockSpec((1, rows, tn), lambda l, n: (l, 0, n)),
        out_shape=jax.ShapeDtypeStruct((depth, rows, d3), F32),
        compiler_params=pltpu.CompilerParams(
            dimension_semantics=("arbitrary", "arbitrary"), vmem_limit_bytes=VMEM_LIMIT_BYTES),
        name="modulation",
    )(cc, w_ada, b_ada.reshape(depth, 1, d3))


def _proj_kernel(h_ref, sc_ref, sh_ref, gn_ref, w_ref, wgr_ref, pm_ref, ws_ref, bs_ref,
                 gq_ref, gk_ref, cos_ref, sin_ref,
                 ya_ref, q_ref, k_ref, vt_ref, zb_ref, cq_ref, ckt_ref, cv_ref, og_ref, gr_ref):
    tm = h_ref.shape[1]
    x = h_ref[0]
    ms = jnp.mean(x * x, axis=-1, keepdims=True)
    xn = (x * lax.rsqrt(ms + EPS)) * gn_ref[...]
    xn = xn * (1.0 + sc_ref[0]) + sh_ref[0]
    xb = xn.astype(BF16)

    def proj(off, width):
        return _dot(xb, w_ref[:, off:off + width])

    lane = _lane_iota((tm, LANES))
    lo_half = lane < HEAD_DIM

    gu = _gelu_tanh(proj(OFF_AU, A_WIDTH))
    gv = _gelu_tanh(proj(OFF_AV, A_WIDTH))
    mu = _seg_mean(gv, pm_ref)
    dv = gv - mu
    var = _seg_mean(dv * dv, pm_ref)
    vn = (dv * lax.rsqrt(var + EPS)).astype(BF16)
    sz = _silu(proj(OFF_AZ, A_WIDTH))
    lo_half_chunk = _lane_iota((CHUNK, LANES)) < HEAD_DIM
    for c in range(tm // CHUNK):
        r0, r1 = c * CHUNK, (c + 1) * CHUNK
        for j in range(A_WIDTH // LANES):
            c0, c1 = j * LANES, (j + 1) * LANES
            vs = vn[r0:r1, c0:c1]
            ev = _dot(ws_ref[2 * j], vs)
            od = _dot(ws_ref[2 * j + 1], vs)
            sv = jnp.where(lo_half_chunk, ev, od) + bs_ref[:, c0:c1]
            ya_ref[0, r0:r1, c0:c1] = (gu[r0:r1, c0:c1] * sv * sz[r0:r1, c0:c1]).astype(BF16)

    cos = cos_ref[...]
    sin = sin_ref[...]
    bit4 = (lane & ROPE_FREQS) != 0

    def rope(t):
        partner = jnp.where(bit4, pltpu.roll(t, ROPE_FREQS, 1), pltpu.roll(t, LANES - ROPE_FREQS, 1))
        return t * cos + partner * sin

    half_w = B_WIDTH // 2
    q_halves = []
    for off in (0, half_w):
        qh = proj(OFF_BQ + off, half_w)
        q_halves.append(qh * lax.rsqrt(_seg_mean(qh * qh, pm_ref) + EPS) * gq_ref[:, off:off + half_w])
    for j in range(B_WIDTH // LANES):
        jj = j % (half_w // LANES)
        qs = rope(q_halves[j // (half_w // LANES)][:, jj * LANES:(jj + 1) * LANES]) * QK_PRESCALE
        sw = pltpu.roll(qs, HEAD_DIM, 1)
        if (2 * j) // B_GROUP == 0:
            h_even = jnp.where(lo_half, qs, 0.0)
            h_odd = jnp.where(lo_half, sw, 0.0)
        else:
            h_even = jnp.where(lo_half, 0.0, sw)
            h_odd = jnp.where(lo_half, 0.0, qs)
        q_ref[0, :, (2 * j) * LANES:(2 * j + 1) * LANES] = h_even.astype(BF16)
        q_ref[0, :, (2 * j + 1) * LANES:(2 * j + 2) * LANES] = h_odd.astype(BF16)

    k = proj(OFF_BK, B_KV_WIDTH)
    kn = k * lax.rsqrt(_seg_mean(k * k, pm_ref) + EPS) * gk_ref[...]
    k_ref[0] = rope(kn).astype(BF16)
    vt_ref[0] = proj(OFF_BV, B_KV_WIDTH).T.astype(BF16)
    zb_ref[0] = _silu(proj(OFF_BZ, B_WIDTH)).astype(zb_ref.dtype)

    cq = proj(OFF_CQ, C_WIDTH)
    ck = proj(OFF_CK, C_WIDTH) * (HEAD_DIM ** -0.5)
    cv = proj(OFF_CV, C_WIDTH)
    for j in range(C_WIDTH // LANES):
        sl = slice(j * LANES, (j + 1) * LANES)
        for par in range(2):
            hd = 2 * j + par
            h_sl = slice(hd * LANES, (hd + 1) * LANES)

            def head_tile(src, pad):
                s = src[:, sl] if par == 0 else pltpu.roll(src[:, sl], HEAD_DIM, 1)
                return jnp.where(lo_half, s, pad)

            cq_ref[0, :, h_sl] = head_tile(cq, 0.0).astype(BF16)
            cv_ref[0, :, h_sl] = head_tile(cv, 1.0).astype(BF16)
            kt = head_tile(ck, 0.0)
            for c in range(tm // CHUNK):
                ckt_ref[0, c, h_sl, :] = kt[c * CHUNK:(c + 1) * CHUNK, :].T.astype(BF16)
    og = jax.nn.sigmoid(proj(OFF_CO, C_WIDTH)) * _silu(proj(OFF_CZ, C_WIDTH))
    og_ref[0] = og.astype(og_ref.dtype)
    gr = _dot_nt(wgr_ref[...], xb)
    for c in range(tm // CHUNK):
        gr_ref[0, c] = gr[:, c * CHUNK:(c + 1) * CHUNK]


def _proj_call(h, sc, sh, lw, consts, n_ctx_tiles, tm):
    b, t, d = h.shape
    nt = t // tm
    n_mod = sc.shape[0]

    def mod_map(bi, i):
        return (jnp.where(i < n_ctx_tiles, n_mod - 1, bi), 0, 0)

    def tile3(w):
        return pl.BlockSpec((1, tm, w), lambda bi, i: (bi, i, 0))

    def whole(a):
        nd = a.ndim
        return pl.BlockSpec(a.shape, lambda bi, i: (0,) * nd)

    ins = [h, sc, sh, lw["g_norm"], lw["w_main"], lw["w_gr"], consts["pm"], lw["w_s"],
           lw["b_s"], lw["g_q"], lw["g_k"], consts["cos"], consts["sin"]]
    in_specs = [tile3(d),
                pl.BlockSpec((1, 1, d), mod_map), pl.BlockSpec((1, 1, d), mod_map),
                whole(lw["g_norm"]), whole(lw["w_main"]), whole(lw["w_gr"]),
                whole(consts["pm"]), whole(lw["w_s"]), whole(lw["b_s"]), whole(lw["g_q"]),
                whole(lw["g_k"]),
                pl.BlockSpec((tm, LANES), lambda bi, i: (i, 0)),
                pl.BlockSpec((tm, LANES), lambda bi, i: (i, 0))]
    pad_w = C_HEADS * LANES
    out_shape = [
        jax.ShapeDtypeStruct((b, t, A_WIDTH), BF16),
        jax.ShapeDtypeStruct((b, t, B_HEADS * LANES), BF16),
        jax.ShapeDtypeStruct((b, t, B_KV_WIDTH), BF16),
        jax.ShapeDtypeStruct((b, B_KV_WIDTH, t), BF16),
        jax.ShapeDtypeStruct((b, t, B_WIDTH), GATE_DTYPE),
        jax.ShapeDtypeStruct((b, t, pad_w), BF16),
        jax.ShapeDtypeStruct((b, t // CHUNK, pad_w, CHUNK), BF16),
        jax.ShapeDtypeStruct((b, t, pad_w), BF16),
        jax.ShapeDtypeStruct((b, t, C_WIDTH), GATE_DTYPE),
        jax.ShapeDtypeStruct((b, t // CHUNK, 4 * GATE_ROWS, CHUNK), F32),
    ]
    out_specs = [tile3(A_WIDTH), tile3(B_HEADS * LANES), tile3(B_KV_WIDTH),
                 pl.BlockSpec((1, B_KV_WIDTH, tm), lambda bi, i: (bi, 0, i)),
                 tile3(B_WIDTH), tile3(pad_w),
                 pl.BlockSpec((1, tm // CHUNK, pad_w, CHUNK), lambda bi, i: (bi, i, 0, 0)),
                 tile3(pad_w), tile3(C_WIDTH),
                 pl.BlockSpec((1, tm // CHUNK, 4 * GATE_ROWS, CHUNK), lambda bi, i: (bi, i, 0, 0))]
    return pl.pallas_call(
        _proj_kernel,
        grid=(b, nt),
        in_specs=in_specs,
        out_specs=out_specs,
        out_shape=out_shape,
        compiler_params=pltpu.CompilerParams(
            dimension_semantics=("arbitrary", "arbitrary"), vmem_limit_bytes=VMEM_LIMIT_BYTES),
        name="proj",
    )(*ins)


def _attn_kernel(q_ref, k_ref, vt_ref, zb_ref, gq_ref, gk_ref, y_ref, *, n_ctx_tiles, tile_off, ctx_len):
    t_all = k_ref.shape[1]

    def attend(n_keys, stabilise):
        kk = k_ref[0, 0:n_keys, :]
        vt = vt_ref[0, :, 0:n_keys]
        for j in range(B_HEADS // 2):
            halves = []
            for h in (2 * j, 2 * j + 1):
                g = h // B_GROUP
                s = _dot_nt(kk, q_ref[0, :, h * LANES:(h + 1) * LANES])
                if stabilise:
                    s = s - jnp.max(s, axis=0, keepdims=True)
                p = jnp.exp2(s)
                l = jnp.sum(p, axis=0, keepdims=True)
                o = _dot(vt, p.astype(BF16))
                halves.append(o[g * HEAD_DIM:(g + 1) * HEAD_DIM, :] / l)
            o_pair = jnp.concatenate(halves, axis=0).T
            sl = slice(j * LANES, (j + 1) * LANES)
            y_ref[0, :, sl] = (o_pair * zb_ref[0, :, sl].astype(F32)).astype(BF16)

    bound = (HEAD_DIM * ATTN_SCALE) * jnp.max(jnp.abs(gq_ref[...])) * jnp.max(jnp.abs(gk_ref[...]))
    safe = bound <= SAFE_LOGIT_BOUND
    i = pl.program_id(1) + tile_off
    if n_ctx_tiles > 0 and tile_off == 0:
        branches = ((i < n_ctx_tiles, ctx_len), (i >= n_ctx_tiles, t_all))
    else:
        branches = ((None, t_all),)
    for cond, n_keys in branches:
        for stabilise in (False, True):
            pred = safe if not stabilise else jnp.logical_not(safe)
            if cond is not None:
                pred = jnp.logical_and(pred, cond)
            pl.when(pred)(functools.partial(attend, n_keys, stabilise))


def _attn_call(q, k, vt, zb, g_q, g_k, n_ctx_tiles, tq, with_ctx):
    b, t, _ = q.shape
    nt = t // tq
    tile_off = 0 if with_ctx else n_ctx_tiles
    kern = functools.partial(_attn_kernel, n_ctx_tiles=n_ctx_tiles, tile_off=tile_off,
                             ctx_len=n_ctx_tiles * tq)
    return pl.pallas_call(
        kern,
        grid=(b, nt - tile_off),
        in_specs=[
            pl.BlockSpec((1, tq, B_HEADS * LANES), lambda bi, i: (bi, i + tile_off, 0)),
            pl.BlockSpec((1, t, B_KV_WIDTH), lambda bi, i: (bi, 0, 0)),
            pl.BlockSpec((1, B_KV_WIDTH, t), lambda bi, i: (bi, 0, 0)),
            pl.BlockSpec((1, tq, B_WIDTH), lambda bi, i: (bi, i + tile_off, 0)),
            pl.BlockSpec(g_q.shape, lambda bi, i: (0, 0)),
            pl.BlockSpec(g_k.shape, lambda bi, i: (0, 0)),
        ],
        out_specs=pl.BlockSpec((1, tq, B_WIDTH), lambda bi, i: (bi, i + tile_off, 0)),
        out_shape=jax.ShapeDtypeStruct((b, t, B_WIDTH), BF16),
        compiler_params=pltpu.CompilerParams(
            dimension_semantics=("arbitrary", "arbitrary"), vmem_limit_bytes=VMEM_LIMIT_BYTES),
        name="attention",
    )(q, k, vt, zb, g_q, g_k)


def _mlstm_kernel(cq_ref, ckt_ref, cv_ref, gr_ref, og_ref, br_ref, gh_ref,
                  y_ref, hf_ref, hb_ref, st_ref, m_ref, *, n_ctx_chunks):
    t = cq_ref.shape[1]
    nc = t // M_CHUNK
    L = M_CHUNK
    n_sel = 2 * GATE_ROWS
    row = lax.broadcasted_iota(jnp.int32, (L, L), 0)
    col = lax.broadcasted_iota(jnp.int32, (L, L), 1)
    masks = (col <= row, col >= row)
    tri_row = tuple(jnp.where(mk, 1.0, 0.0).astype(F32) for mk in (row <= col, row >= col))
    ones_ll = jnp.ones((L, L), F32)
    eye_bf = jnp.where(row == col, 1.0, 0.0).astype(BF16)
    srow = lax.broadcasted_iota(jnp.int32, (L, 2 * C_HEADS * L), 0)
    scol = lax.broadcasted_iota(jnp.int32, (L, 2 * C_HEADS * L), 1)
    sel = jnp.where((srow < 3 * n_sel)
                    & (((srow // GATE_ROWS) % 2) == (scol // (C_HEADS * L)))
                    & ((srow % GATE_ROWS) == ((scol // L) % C_HEADS)), 1.0, 0.0).astype(BF16)
    lane8 = _lane_iota((GATE_ROWS, L))

    st_ref[...] = jnp.zeros(st_ref.shape, F32)
    m_ref[...] = jnp.zeros(m_ref.shape, F32)
    bias_r = br_ref[...]

    def running_max(x, d):
        sh = 1
        while sh < L:
            if d == 0:
                x = jnp.maximum(x, jnp.where(lane8 >= sh, pltpu.roll(x, sh, 1), -jnp.inf))
            else:
                x = jnp.maximum(x, jnp.where(lane8 < L - sh, pltpu.roll(x, L - sh, 1), -jnp.inf))
            sh *= 2
        return x

    def step(i, carry):
        chunks = (i, jnp.where(i < n_ctx_chunks, n_ctx_chunks - 1 - i, nc + n_ctx_chunks - 1 - i))
        for d in range(2):
            c = chunks[d]
            rows = pl.ds(pl.multiple_of(c * L, L), L)
            g_r = gr_ref[0, c] + bias_r
            li = g_r[2 * d * GATE_ROWS:(2 * d + 1) * GATE_ROWS]
            lf = _log_sigmoid(g_r[(2 * d + 1) * GATE_ROWS:(2 * d + 2) * GATE_ROWS])
            b = jnp.dot(lf, tri_row[d], preferred_element_type=F32, precision=HIGHEST)
            bend = jnp.dot(lf, ones_ll, preferred_element_type=F32, precision=HIGHEST)
            m_old = m_ref[d]
            g = li - b
            big_m = jnp.maximum(running_max(g, d), m_old)
            w_end = bend - b + li
            m_new = jnp.maximum(bend + m_old, jnp.max(w_end, axis=1, keepdims=True))
            decay = jnp.exp(bend + m_old - m_new)
            wk = jnp.exp(w_end - m_new)
            m_ref[d] = m_new

            r16 = jnp.concatenate([big_m, b + big_m], axis=0)
            hi = r16.astype(BF16)
            r1 = r16 - hi.astype(F32)
            mid = r1.astype(BF16)
            lo = (r1 - mid.astype(F32)).astype(BF16)
            pieces = jnp.concatenate([hi, mid, lo, jnp.zeros((L - 3 * n_sel, L), BF16)], axis=0)
            pieces_t = _dot_nt(eye_bf, pieces).astype(BF16)
            spread = _dot(pieces_t, sel)

            dst = hf_ref if d == 0 else hb_ref
            for hd in range(C_HEADS):
                sl = slice(hd * LANES, (hd + 1) * LANES)
                sidx = d * C_HEADS + hd
                qh = cq_ref[0, rows, sl]
                kt = ckt_ref[0, c, sl, :]
                vh = cv_ref[0, rows, sl]
                st = st_ref[sidx]
                m_mat = spread[:, hd * L:(hd + 1) * L]
                mt_mat = spread[:, (C_HEADS + hd) * L:(C_HEADS + hd + 1) * L]

                e = jnp.exp(jnp.where(masks[d], g[hd:hd + 1, :] - m_mat, -jnp.inf))
                s = _dot(qh, kt) * e
                inter = jnp.exp(m_old[hd:hd + 1, :] - m_mat)
                na = _dot(s.astype(BF16), vh) + inter * _dot(qh, st.astype(BF16))
                den = pltpu.roll(na, HEAD_DIM, 1)
                dst[rows, sl] = na / jnp.maximum(jnp.abs(den), jnp.exp(-mt_mat))

                kw = (kt.astype(F32) * wk[hd:hd + 1, :]).astype(BF16)
                st_ref[sidx] = decay[hd:hd + 1, :] * st + _dot(kw, vh)
        return carry

    lax.fori_loop(0, nc, step, 0)

    lane = _lane_iota((L, LANES))
    lo_half = lane < HEAD_DIM

    def combine(c, carry):
        rows = pl.ds(pl.multiple_of(c * L, L), L)
        normed = []
        for hd in range(C_HEADS):
            sl = slice(hd * LANES, (hd + 1) * LANES)
            hs = jnp.where(lo_half, hf_ref[rows, sl] + hb_ref[rows, sl], 0.0)
            ms = jnp.sum(hs * hs, axis=1, keepdims=True) * (1.0 / HEAD_DIM)
            normed.append(hs * lax.rsqrt(ms + EPS) * gh_ref[:, sl])
        for j in range(C_HEADS // 2):
            pair = jnp.where(lo_half, normed[2 * j], pltpu.roll(normed[2 * j + 1], HEAD_DIM, 1))
            sl = slice(j * LANES, (j + 1) * LANES)
            y_ref[0, rows, sl] = (og_ref[0, rows, sl].astype(F32) * pair).astype(BF16)
        return carry

    lax.fori_loop(0, nc, combine, 0)


def _mlstm_call(cq, ckt, cv, gr, og, lw, n_ctx_chunks):
    b, t, pad_w = cq.shape

    def per_b(a):
        nd = a.ndim
        return pl.BlockSpec((1,) + a.shape[1:], lambda bi: (bi,) + (0,) * (nd - 1))

    def whole(a):
        nd = a.ndim
        return pl.BlockSpec(a.shape, lambda bi: (0,) * nd)

    kern = functools.partial(_mlstm_kernel, n_ctx_chunks=n_ctx_chunks)
    return pl.pallas_call(
        kern,
        grid=(b,),
        in_specs=[per_b(cq), per_b(ckt), per_b(cv), per_b(gr), per_b(og),
                  whole(lw["bg_r"]), whole(lw["g_head"])],
        out_specs=pl.BlockSpec((1, t, C_WIDTH), lambda bi: (bi, 0, 0)),
        out_shape=jax.ShapeDtypeStruct((b, t, C_WIDTH), BF16),
        scratch_shapes=[
            pltpu.VMEM((t, pad_w), F32),
            pltpu.VMEM((t, pad_w), F32),
            pltpu.VMEM((2 * C_HEADS, LANES, LANES), F32),
            pltpu.VMEM((2, GATE_ROWS, LANES), F32),
        ],
        compiler_params=pltpu.CompilerParams(
            dimension_semantics=("arbitrary",), vmem_limit_bytes=VMEM_LIMIT_BYTES),
        name="mlstm",
    )(cq, ckt, cv, gr, og, lw["bg_r"], lw["g_head"])


def _out_kernel(h_ref, gt_ref, ya_ref, yb_ref, yc_ref, w_ref, gf_ref, o_ref, *, final):
    acc = _dot(ya_ref[0], w_ref[0:A_WIDTH, :])
    acc += _dot(yb_ref[0], w_ref[A_WIDTH:A_WIDTH + B_WIDTH, :])
    acc += _dot(yc_ref[0], w_ref[A_WIDTH + B_WIDTH:, :])
    x = h_ref[0] + gt_ref[0] * acc
    if final:
        ms = jnp.mean(x * x, axis=-1, keepdims=True)
        x = (x * lax.rsqrt(ms + EPS)) * gf_ref[...]
    o_ref[0] = x


def _out_call(h, gt, ya, yb, yc, w_out, g_final, n_ctx_tiles, tm, final):
    b, t, d = h.shape
    nt = t // tm
    n_mod = gt.shape[0]
    off = n_ctx_tiles if final else 0

    def tile3(w):
        return pl.BlockSpec((1, tm, w), lambda bi, i: (bi, i + off, 0))

    def mod_map(bi, i):
        return (jnp.where(i + off < n_ctx_tiles, n_mod - 1, bi), 0, 0)

    if final:
        out_shape = jax.ShapeDtypeStruct((b, t - off * tm, d), F32)
        out_spec = pl.BlockSpec((1, tm, d), lambda bi, i: (bi, i, 0))
        aliases = {}
    else:
        out_shape = jax.ShapeDtypeStruct((b, t, d), F32)
        out_spec = tile3(d)
        aliases = {0: 0}
    return pl.pallas_call(
        functools.partial(_out_kernel, final=final),
        grid=(b, nt - off),
        in_specs=[tile3(d), pl.BlockSpec((1, 1, d), mod_map), tile3(A_WIDTH), tile3(B_WIDTH),
                  tile3(C_WIDTH),
                  pl.BlockSpec(w_out.shape, lambda bi, i: (0, 0)),
                  pl.BlockSpec(g_final.shape, lambda bi, i: (0, 0))],
        out_specs=out_spec,
        out_shape=out_shape,
        input_output_aliases=aliases,
        compiler_params=pltpu.CompilerParams(
            dimension_semantics=("arbitrary", "arbitrary"), vmem_limit_bytes=VMEM_LIMIT_BYTES),
        name="out_proj",
    )(h, gt, ya, yb, yc, w_out, g_final)


def _rope_tables(ctx_len, seq):
    pos = np.arange(seq)
    freqs = ROPE_THETA ** (-np.arange(ROPE_FREQS, dtype=np.float32) / ROPE_FREQS)
    ang_r = (pos // GRID_W).astype(np.float32)[:, None] * freqs
    ang_c = (pos % GRID_W).astype(np.float32)[:, None] * freqs
    cos_h = np.concatenate([np.cos(ang_r), np.cos(ang_r), np.cos(ang_c), np.cos(ang_c)], axis=1)
    sin_h = np.concatenate([-np.sin(ang_r), np.sin(ang_r), -np.sin(ang_c), np.sin(ang_c)], axis=1)
    cos = np.concatenate([np.ones((ctx_len, HEAD_DIM), np.float32), cos_h.astype(np.float32)], axis=0)
    sin = np.concatenate([np.zeros((ctx_len, HEAD_DIM), np.float32), sin_h.astype(np.float32)], axis=0)
    return jnp.asarray(np.tile(cos, (1, 2))), jnp.asarray(np.tile(sin, (1, 2)))


def _layer_weights(l, g_norm, w_in, w_s, b_s, g_q, g_k, b_gates, g_head, w_out):
    d = w_in.shape[1]
    w_l = w_in[l]
    w_g = w_l[:, OFF_CG:]
    w_gr = jnp.zeros((4, GATE_ROWS, d), F32).at[:, :C_HEADS, :].set(w_g.T.reshape(4, C_HEADS, d))
    bg_r = jnp.zeros((4, GATE_ROWS), F32).at[:, :C_HEADS].set(b_gates[l].reshape(4, C_HEADS))
    gh = jnp.pad(g_head[l].reshape(C_HEADS, HEAD_DIM), ((0, 0), (0, LANES - HEAD_DIM)))
    return {
        "g_norm": g_norm[l].reshape(1, d),
        "w_main": w_l[:, :OFF_CG].astype(BF16),
        "w_gr": w_gr.reshape(4 * GATE_ROWS, d).astype(BF16),
        "w_s": w_s[l].astype(BF16),
        "b_s": jnp.repeat(b_s[l].T, HEAD_DIM, axis=1),
        "g_q": jnp.tile(g_q[l], B_HEADS).reshape(1, B_WIDTH),
        "g_k": jnp.tile(g_k[l], B_KV_HEADS).reshape(1, B_KV_WIDTH),
        "bg_r": bg_r.reshape(4 * GATE_ROWS, 1),
        "g_head": gh.reshape(1, C_HEADS * LANES),
        "w_out": w_out[l].astype(BF16),
    }


def kernel(x, c, ctx, c_ctx, w_ada, b_ada, g_norm, w_in, w_s, b_s, g_q, g_k, b_gates, g_head, w_out,
           g_final):
    b, seq, d = x.shape
    ctx_len = ctx.shape[1]
    depth = w_ada.shape[0]
    tm = 256 if (ctx_len % 256 == 0 and seq % 256 == 0) else CHUNK
    n_ctx_tiles = ctx_len // tm
    n_ctx_chunks = ctx_len // M_CHUNK

    n_mod = b + 1
    rows = -(-n_mod // 8) * 8
    cc = jnp.zeros((rows, d), F32).at[:b].set(c).at[b].set(c_ctx)
    mods = _modulation(cc, w_ada, b_ada)[:, :n_mod]
    mods = mods.reshape(depth, n_mod, 3, 1, d)

    cos, sin = _rope_tables(ctx_len, seq)
    seg = np.kron(np.eye(B_HEADS, dtype=np.float32), np.full((HEAD_DIM, HEAD_DIM), 1.0 / HEAD_DIM, np.float32))
    consts = {"cos": cos, "sin": sin, "pm": jnp.asarray(seg, BF16)}
    gf = g_final.reshape(1, d)

    h = jnp.concatenate([ctx, x], axis=1)
    for l in range(depth):
        final = l == depth - 1
        lw = _layer_weights(l, g_norm, w_in, w_s, b_s, g_q, g_k, b_gates, g_head, w_out)
        sh, sc, gt = mods[l, :, 0], mods[l, :, 1], mods[l, :, 2]
        (ya, q, k, vt, zb, cq, ckt, cv, og, gr) = _proj_call(h, sc, sh, lw, consts, n_ctx_tiles, tm)
        yb = _attn_call(q, k, vt, zb, lw["g_q"], lw["g_k"], n_ctx_tiles, tm, with_ctx=not final)
        yc = _mlstm_call(cq, ckt, cv, gr, og, lw, n_ctx_chunks)
        h = _out_call(h, gt, ya, yb, yc, lw["w_out"], gf, n_ctx_tiles, tm, final)
    return h
```

```python
import functools

import jax
import jax.numpy as jnp
import numpy as np
from jax import lax
from jax.experimental import pallas as pl
from jax.experimental.pallas import tpu as pltpu

HEAD_DIM = 64
LANES = 128
GRID_W = 64
EPS = 1e-6
A_GROUPS = 4
A_WIDTH = A_GROUPS * HEAD_DIM
CHUNK = 128
B_HEADS = 8
B_KV_HEADS = 2
B_GROUP = B_HEADS // B_KV_HEADS
B_WIDTH = B_HEADS * HEAD_DIM
B_KV_WIDTH = B_KV_HEADS * HEAD_DIM
ATTN_SCALE = HEAD_DIM ** -0.5
ROPE_THETA = 10000.0
ROPE_FREQS = HEAD_DIM // 4
C_HEADS = 4
C_WIDTH = C_HEADS * HEAD_DIM
M_CHUNK = 128
N_GATES = 4 * C_HEADS
GATE_ROWS = 8

OFF_AU = 0
OFF_AV = OFF_AU + A_WIDTH
OFF_AZ = OFF_AV + A_WIDTH
OFF_BQ = OFF_AZ + A_WIDTH
OFF_BK = OFF_BQ + B_WIDTH
OFF_BV = OFF_BK + B_KV_WIDTH
OFF_BZ = OFF_BV + B_KV_WIDTH
OFF_CQ = OFF_BZ + B_WIDTH
OFF_CK = OFF_CQ + C_WIDTH
OFF_CV = OFF_CK + C_WIDTH
OFF_CO = OFF_CV + C_WIDTH
OFF_CZ = OFF_CO + C_WIDTH
OFF_CG = OFF_CZ + C_WIDTH
D_IN = OFF_CG + N_GATES

VMEM_LIMIT_BYTES = 56 * 1024 * 1024

F32 = jnp.float32
BF16 = jnp.bfloat16
GATE_DTYPE = BF16
HIGHEST = lax.Precision.HIGHEST

QK_PRESCALE = float(ATTN_SCALE * np.log2(np.e))
SAFE_LOGIT_BOUND = 40.0


def _dot(a, b):
    return jnp.dot(a, b, preferred_element_type=F32)


def _dot_nt(a, b):
    return lax.dot_general(a, b, (((1,), (1,)), ((), ())), preferred_element_type=F32)


def _dot_tn(a, b):
    return lax.dot_general(a, b, (((0,), (0,)), ((), ())), preferred_element_type=F32)


def _silu(x):
    return x * jax.nn.sigmoid(x)


def _gelu_tanh(x):
    c = np.float32(np.sqrt(2.0 / np.pi))
    return 0.5 * x * (1.0 + jnp.tanh(c * (x + np.float32(0.044715) * (x * x * x))))


def _log_sigmoid(x):
    return jnp.minimum(x, 0.0) - jnp.log1p(jnp.exp(-jnp.abs(x)))


def _seg_mean(x, pm_ref):
    hi = x.astype(BF16)
    lo = (x - hi.astype(F32)).astype(BF16)
    w = x.shape[-1]
    pm = pm_ref[0:w, 0:w]
    return _dot(hi, pm) + _dot(lo, pm)


def _lane_iota(shape):
    return lax.broadcasted_iota(jnp.int32, shape, len(shape) - 1)


def _mod_kernel(cc_ref, w_ref, b_ref, o_ref):
    s = _silu(cc_ref[...])
    o_ref[0] = jnp.dot(s, w_ref[0], preferred_element_type=F32, precision=HIGHEST) + b_ref[0]


def _modulation(cc, w_ada, b_ada):
    depth, d, d3 = w_ada.shape
    rows = cc.shape[0]
    tn = d
    return pl.pallas_call(
        _mod_kernel,
        grid=(depth, d3 // tn),
        in_specs=[
            pl.BlockSpec((rows, d), lambda l, n: (0, 0)),
            pl.BlockSpec((1, d, tn), lambda l, n: (l, 0, n)),
            pl.BlockSpec((1, 1, tn), lambda l, n: (l, 0, n)),
        ],
        out_specs=pl.BlockSpec((1, rows, tn), lambda l, n: (l, 0, n)),
        out_shape=jax.ShapeDtypeStruct((depth, rows, d3), F32),
        compiler_params=pltpu.CompilerParams(
            dimension_semantics=("arbitrary", "arbitrary"), vmem_limit_bytes=VMEM_LIMIT_BYTES),
        name="modulation",
    )(cc, w_ada, b_ada.reshape(depth, 1, d3))


def _proj_kernel(h_ref, sc_ref, sh_ref, gn_ref, w_ref, wgr_ref, pm_ref, ws_ref, bs_ref,
                 gq_ref, gk_ref, cos_ref, sin_ref,
                 ya_ref, q_ref, k_ref, vt_ref, zb_ref, cq_ref, ckt_ref, cv_ref, og_ref, gr_ref):
    tm = h_ref.shape[1]
    x = h_ref[0]
    ms = jnp.mean(x * x, axis=-1, keepdims=True)
    xn = (x * lax.rsqrt(ms + EPS)) * gn_ref[...]
    xn = xn * (1.0 + sc_ref[0]) + sh_ref[0]
    xb = xn.astype(BF16)

    def proj(off, width):
        return _dot(xb, w_ref[:, off:off + width])

    lane = _lane_iota((tm, LANES))
    lo_half = lane < HEAD_DIM

    gu = _gelu_tanh(proj(OFF_AU, A_WIDTH))
    gv = _gelu_tanh(proj(OFF_AV, A_WIDTH))
    mu = _seg_mean(gv, pm_ref)
    dv = gv - mu
    var = _seg_mean(dv * dv, pm_ref)
    vn = (dv * lax.rsqrt(var + EPS)).astype(BF16)
    sz = _silu(proj(OFF_AZ, A_WIDTH))
    lo_half_chunk = _lane_iota((CHUNK, LANES)) < HEAD_DIM
    for c in range(tm // CHUNK):
        r0, r1 = c * CHUNK, (c + 1) * CHUNK
        for j in range(A_WIDTH // LANES):
            c0, c1 = j * LANES, (j + 1) * LANES
            vs = vn[r0:r1, c0:c1]
            ev = _dot(ws_ref[2 * j], vs)
            od = _dot(ws_ref[2 * j + 1], vs)
            sv = jnp.where(lo_half_chunk, ev, od) + bs_ref[:, c0:c1]
            ya_ref[0, r0:r1, c0:c1] = (gu[r0:r1, c0:c1] * sv * sz[r0:r1, c0:c1]).astype(BF16)

    cos = cos_ref[...]
    sin = sin_ref[...]
    bit4 = (lane & ROPE_FREQS) != 0

    def rope(t):
        partner = jnp.where(bit4, pltpu.roll(t, ROPE_FREQS, 1), pltpu.roll(t, LANES - ROPE_FREQS, 1))
        return t * cos + partner * sin

    half_w = B_WIDTH // 2
    q_halves = []
    for off in (0, half_w):
        qh = proj(OFF_BQ + off, half_w)
        q_halves.append(qh * lax.rsqrt(_seg_mean(qh * qh, pm_ref) + EPS) * gq_ref[:, off:off + half_w])
    for j in range(B_WIDTH // LANES):
        jj = j % (half_w // LANES)
        qs = rope(q_halves[j // (half_w // LANES)][:, jj * LANES:(jj + 1) * LANES]) * QK_PRESCALE
        sw = pltpu.roll(qs, HEAD_DIM, 1)
        if (2 * j) // B_GROUP == 0:
            h_even = jnp.where(lo_half, qs, 0.0)
            h_odd = jnp.where(lo_half, sw, 0.0)
        else:
            h_even = jnp.where(lo_half, 0.0, sw)
            h_odd = jnp.where(lo_half, 0.0, qs)
        q_ref[0, :, (2 * j) * LANES:(2 * j + 1) * LANES] = h_even.astype(BF16)
        q_ref[0, :, (2 * j + 1) * LANES:(2 * j + 2) * LANES] = h_odd.astype(BF16)

    k = proj(OFF_BK, B_KV_WIDTH)
    kn = k * lax.rsqrt(_seg_mean(k * k, pm_ref) + EPS) * gk_ref[...]
    k_ref[0] = rope(kn).astype(BF16)
    vt_ref[0] = proj(OFF_BV, B_KV_WIDTH).T.astype(BF16)
    zb_ref[0] = _silu(proj(OFF_BZ, B_WIDTH)).astype(zb_ref.dtype)

    cq = proj(OFF_CQ, C_WIDTH)
    ck = proj(OFF_CK, C_WIDTH) * (HEAD_DIM ** -0.5)
    cv = proj(OFF_CV, C_WIDTH)
    for j in range(C_WIDTH // LANES):
        sl = slice(j * LANES, (j + 1) * LANES)
        for par in range(2):
            hd = 2 * j + par
            h_sl = slice(hd * LANES, (hd + 1) * LANES)

            def head_tile(src, pad):
                s = src[:, sl] if par == 0 else pltpu.roll(src[:, sl], HEAD_DIM, 1)
                return jnp.where(lo_half, s, pad)

            cq_ref[0, :, h_sl] = head_tile(cq, 0.0).astype(BF16)
            cv_ref[0, :, h_sl] = head_tile(cv, 1.0).astype(BF16)
            kt = head_tile(ck, 0.0)
            for c in range(tm // CHUNK):
                ckt_ref[0, c, h_sl, :] = kt[c * CHUNK:(c + 1) * CHUNK, :].T.astype(BF16)
    og = jax.nn.sigmoid(proj(OFF_CO, C_WIDTH)) * _silu(proj(OFF_CZ, C_WIDTH))
    og_ref[0] = og.astype(og_ref.dtype)
    gr = _dot_nt(wgr_ref[...], xb)
    for c in range(tm // CHUNK):
        for ty in range(4):
            gr_ref[0, ty, c * GATE_ROWS:(c + 1) * GATE_ROWS, :] = (
                gr[ty * GATE_ROWS:(ty + 1) * GATE_ROWS, c * CHUNK:(c + 1) * CHUNK])


def _proj_call(h, sc, sh, lw, consts, n_ctx_tiles, tm):
    b, t, d = h.shape
    nt = t // tm
    n_mod = sc.shape[0]

    def mod_map(bi, i):
        return (jnp.where(i < n_ctx_tiles, n_mod - 1, bi), 0, 0)

    def tile3(w):
        return pl.BlockSpec((1, tm, w), lambda bi, i: (bi, i, 0))

    def whole(a):
        nd = a.ndim
        return pl.BlockSpec(a.shape, lambda bi, i: (0,) * nd)

    ins = [h, sc, sh, lw["g_norm"], lw["w_main"], lw["w_gr"], consts["pm"], lw["w_s"],
           lw["b_s"], lw["g_q"], lw["g_k"], consts["cos"], consts["sin"]]
    in_specs = [tile3(d),
                pl.BlockSpec((1, 1, d), mod_map), pl.BlockSpec((1, 1, d), mod_map),
                whole(lw["g_norm"]), whole(lw["w_main"]), whole(lw["w_gr"]),
                whole(consts["pm"]), whole(lw["w_s"]), whole(lw["b_s"]), whole(lw["g_q"]),
                whole(lw["g_k"]),
                pl.BlockSpec((tm, LANES), lambda bi, i: (i, 0)),
                pl.BlockSpec((tm, LANES), lambda bi, i: (i, 0))]
    pad_w = C_HEADS * LANES
    out_shape = [
        jax.ShapeDtypeStruct((b, t, A_WIDTH), BF16),
        jax.ShapeDtypeStruct((b, t, B_HEADS * LANES), BF16),
        jax.ShapeDtypeStruct((b, t, B_KV_WIDTH), BF16),
        jax.ShapeDtypeStruct((b, B_KV_WIDTH, t), BF16),
        jax.ShapeDtypeStruct((b, t, B_WIDTH), GATE_DTYPE),
        jax.ShapeDtypeStruct((b, t, pad_w), BF16),
        jax.ShapeDtypeStruct((b, t // CHUNK, pad_w, CHUNK), BF16),
        jax.ShapeDtypeStruct((b, t, pad_w), BF16),
        jax.ShapeDtypeStruct((b, t, C_WIDTH), GATE_DTYPE),
        jax.ShapeDtypeStruct((b, 4, t // CHUNK * GATE_ROWS, CHUNK), F32),
    ]
    out_specs = [tile3(A_WIDTH), tile3(B_HEADS * LANES), tile3(B_KV_WIDTH),
                 pl.BlockSpec((1, B_KV_WIDTH, tm), lambda bi, i: (bi, 0, i)),
                 tile3(B_WIDTH), tile3(pad_w),
                 pl.BlockSpec((1, tm // CHUNK, pad_w, CHUNK), lambda bi, i: (bi, i, 0, 0)),
                 tile3(pad_w), tile3(C_WIDTH),
                 pl.BlockSpec((1, 4, tm // CHUNK * GATE_ROWS, CHUNK), lambda bi, i: (bi, 0, i, 0))]
    return pl.pallas_call(
        _proj_kernel,
        grid=(b, nt),
        in_specs=in_specs,
        out_specs=out_specs,
        out_shape=out_shape,
        compiler_params=pltpu.CompilerParams(
            dimension_semantics=("arbitrary", "arbitrary"), vmem_limit_bytes=VMEM_LIMIT_BYTES),
        name="proj",
    )(*ins)


def _attn_kernel(q_ref, k_ref, vt_ref, zb_ref, gq_ref, gk_ref, y_ref, *, n_ctx_tiles, tile_off, ctx_len):
    t_all = k_ref.shape[1]

    tq = q_ref.shape[1]

    def attend(n_keys, stabilise):
        kk = k_ref[0, 0:n_keys, :]
        vt = vt_ref[0, :, 0:n_keys]
        for g in range(B_KV_HEADS):
            q_grp = jnp.concatenate(
                [q_ref[0, :, h * LANES:(h + 1) * LANES] for h in range(g * B_GROUP, (g + 1) * B_GROUP)],
                axis=0)
            s = _dot_nt(kk, q_grp)
            if stabilise:
                s = s - jnp.max(s, axis=0, keepdims=True)
            p = jnp.exp2(s)
            l = jnp.sum(p, axis=0, keepdims=True)
            o = _dot(vt, p.astype(BF16))[g * HEAD_DIM:(g + 1) * HEAD_DIM, :] / l
            for jj in range(B_GROUP // 2):
                pair = jnp.concatenate([o[:, (2 * jj) * tq:(2 * jj + 1) * tq],
                                        o[:, (2 * jj + 1) * tq:(2 * jj + 2) * tq]], axis=0).T
                j = g * (B_GROUP // 2) + jj
                sl = slice(j * LANES, (j + 1) * LANES)
                y_ref[0, :, sl] = (pair * zb_ref[0, :, sl].astype(F32)).astype(BF16)

    bound = (HEAD_DIM * ATTN_SCALE) * jnp.max(jnp.abs(gq_ref[...])) * jnp.max(jnp.abs(gk_ref[...]))
    safe = bound <= SAFE_LOGIT_BOUND
    i = pl.program_id(1) + tile_off
    if n_ctx_tiles > 0 and tile_off == 0:
        branches = ((i < n_ctx_tiles, ctx_len), (i >= n_ctx_tiles, t_all))
    else:
        branches = ((None, t_all),)
    for cond, n_keys in branches:
        for stabilise in (False, True):
            pred = safe if not stabilise else jnp.logical_not(safe)
            if cond is not None:
                pred = jnp.logical_and(pred, cond)
            pl.when(pred)(functools.partial(attend, n_keys, stabilise))


def _attn_call(q, k, vt, zb, g_q, g_k, n_ctx_tiles, tq, with_ctx):
    b, t, _ = q.shape
    nt = t // tq
    tile_off = 0 if with_ctx else n_ctx_tiles
    kern = functools.partial(_attn_kernel, n_ctx_tiles=n_ctx_tiles, tile_off=tile_off,
                             ctx_len=n_ctx_tiles * tq)
    return pl.pallas_call(
        kern,
        grid=(b, nt - tile_off),
        in_specs=[
            pl.BlockSpec((1, tq, B_HEADS * LANES), lambda bi, i: (bi, i + tile_off, 0)),
            pl.BlockSpec((1, t, B_KV_WIDTH), lambda bi, i: (bi, 0, 0)),
            pl.BlockSpec((1, B_KV_WIDTH, t), lambda bi, i: (bi, 0, 0)),
            pl.BlockSpec((1, tq, B_WIDTH), lambda bi, i: (bi, i + tile_off, 0)),
            pl.BlockSpec(g_q.shape, lambda bi, i: (0, 0)),
            pl.BlockSpec(g_k.shape, lambda bi, i: (0, 0)),
        ],
        out_specs=pl.BlockSpec((1, tq, B_WIDTH), lambda bi, i: (bi, i + tile_off, 0)),
        out_shape=jax.ShapeDtypeStruct((b, t, B_WIDTH), BF16),
        compiler_params=pltpu.CompilerParams(
            dimension_semantics=("arbitrary", "arbitrary"), vmem_limit_bytes=VMEM_LIMIT_BYTES),
        name="attention",
    )(q, k, vt, zb, g_q, g_k)


def _mlstm_kernel(cq_ref, ckt_ref, cv_ref, gr_ref, og_ref, br_ref, gh_ref,
                  y_ref, hf_ref, hb_ref, st_ref, m_ref, gp_ref, sel_ref, madd_ref, *, n_ctx_chunks):
    t = cq_ref.shape[1]
    nc = t // M_CHUNK
    L = M_CHUNK
    n_units = 2 * C_HEADS
    W = n_units * L
    n_rows = nc * GATE_ROWS
    n_spread = 4 * GATE_ROWS
    row = lax.broadcasted_iota(jnp.int32, (L, L), 0)
    col = lax.broadcasted_iota(jnp.int32, (L, L), 1)
    tri_row = tuple(jnp.where(mk, 1.0, 0.0).astype(F32) for mk in (row <= col, row >= col))
    ones_ll = jnp.ones((L, L), F32)
    eye_bf = jnp.where(row == col, 1.0, 0.0).astype(BF16)
    srow = lax.broadcasted_iota(jnp.int32, (L, 2 * W), 0)
    scol = lax.broadcasted_iota(jnp.int32, (L, 2 * W), 1)
    r_in = srow % n_spread
    sel_ref[...] = jnp.where((srow < 3 * n_spread)
                             & ((r_in // (2 * GATE_ROWS)) == (scol // W))
                             & (((r_in // GATE_ROWS) % 2) == ((scol // (C_HEADS * L)) % 2))
                             & ((r_in % GATE_ROWS) == ((scol // L) % C_HEADS)), 1.0, 0.0).astype(BF16)
    mrow = lax.broadcasted_iota(jnp.int32, (L, W), 0)
    mcol = lax.broadcasted_iota(jnp.int32, (L, W), 1)
    upward = mcol < C_HEADS * L
    hidden = (upward & ((mcol % L) > mrow)) | (jnp.logical_not(upward) & ((mcol % L) < mrow))
    madd_ref[...] = jnp.where(hidden, -jnp.inf, 0.0).astype(F32)

    st_ref[...] = jnp.zeros(st_ref.shape, F32)
    m_ref[...] = jnp.zeros(m_ref.shape, F32)

    lane_r = _lane_iota((n_rows, L))
    for d in range(2):
        li = gr_ref[0, 2 * d] + br_ref[2 * d]
        lf = _log_sigmoid(gr_ref[0, 2 * d + 1] + br_ref[2 * d + 1])
        b = jnp.dot(lf, tri_row[d], preferred_element_type=F32, precision=HIGHEST)
        bend = jnp.dot(lf, ones_ll, preferred_element_type=F32, precision=HIGHEST)
        g = li - b
        cm = g
        sh = 1
        while sh < L:
            if d == 0:
                cm = jnp.maximum(cm, jnp.where(lane_r >= sh, pltpu.roll(cm, sh, 1), -jnp.inf))
            else:
                cm = jnp.maximum(cm, jnp.where(lane_r < L - sh, pltpu.roll(cm, L - sh, 1), -jnp.inf))
            sh *= 2
        w_end = bend - b + li
        mx = jnp.max(w_end, axis=1, keepdims=True)
        gp_ref[d, 0] = g
        gp_ref[d, 1] = cm
        gp_ref[d, 2] = b
        gp_ref[d, 3] = jnp.exp(w_end - mx)
        gp_ref[d, 4] = bend
        gp_ref[d, 5] = jnp.broadcast_to(mx, (n_rows, L))

    def unit_lanes(per_dir):
        return jnp.concatenate([per_dir[d][h:h + 1, :] for d in range(2) for h in range(C_HEADS)], axis=1)

    def step(i, carry):
        chunks = (i, jnp.where(i < n_ctx_chunks, n_ctx_chunks - 1 - i, nc + n_ctx_chunks - 1 - i))
        rows = tuple(pl.ds(pl.multiple_of(c * L, L), L) for c in chunks)
        grow = tuple(pl.ds(pl.multiple_of(c * GATE_ROWS, GATE_ROWS), GATE_ROWS) for c in chunks)
        g, cm, b, wk0, bend, mx = (tuple(gp_ref[d, k, grow[d], :] for d in range(2)) for k in range(6))
        m_old = (m_ref[0], m_ref[1])

        r32 = jnp.concatenate([cm[0], cm[1], b[0], b[1]], axis=0)
        hi = r32.astype(BF16)
        r1 = r32 - hi.astype(F32)
        mid = r1.astype(BF16)
        lo = (r1 - mid.astype(F32)).astype(BF16)
        pieces = jnp.concatenate([hi, mid, lo, jnp.zeros((L - 3 * n_spread, L), BF16)], axis=0)
        pieces_t = _dot_nt(eye_bf, pieces).astype(BF16)
        spread = _dot(pieces_t, sel_ref[...])
        cm_all = spread[:, 0:W]
        b_all = spread[:, W:2 * W]

        e0 = jnp.exp((unit_lanes(g) - cm_all) + madd_ref[...])

        q_d = tuple(cq_ref[0, rows[d], :] for d in range(2))
        v_d = tuple(cv_ref[0, rows[d], :] for d in range(2))
        kt_d = tuple(ckt_ref[0, chunks[d]] for d in range(2))
        units = [(d, h) for d in range(2) for h in range(C_HEADS)]
        st_old, qk, bq = [], [], []
        for u, (d, h) in enumerate(units):
            sl = slice(h * LANES, (h + 1) * LANES)
            st = st_ref[u]
            st_old.append(st)
            r = _dot(q_d[d][:, sl], jnp.concatenate([kt_d[d][sl, :], st.astype(BF16)], axis=1))
            qk.append(r[:, 0:L])
            bq.append(r[:, L:2 * L])
        s0 = (jnp.concatenate(qk, axis=1) * e0).astype(BF16)

        acc, upd = [], []
        for u, (d, h) in enumerate(units):
            sl = slice(h * LANES, (h + 1) * LANES)
            kw = (kt_d[d][sl, :].astype(F32) * wk0[d][h:h + 1, :]).astype(BF16)
            r = _dot(jnp.concatenate([s0[:, u * L:(u + 1) * L], kw], axis=0), v_d[d][:, sl])
            acc.append(r[0:L])
            upd.append(r[L:2 * L])

        m_all = unit_lanes(m_old)
        above = cm_all >= m_all
        x = jnp.exp(-jnp.abs(cm_all - m_all))
        f = jnp.where(above, 1.0, x)
        inter = jnp.where(above, x, 1.0)
        na = f * jnp.concatenate(acc, axis=1) + inter * jnp.concatenate(bq, axis=1)
        den = jnp.concatenate([pltpu.roll(na[:, u * L:(u + 1) * L], HEAD_DIM, 1) for u in range(n_units)],
                              axis=1)
        thr = jnp.exp(-(b_all + jnp.maximum(cm_all, m_all)))
        h_all = na / jnp.maximum(jnp.abs(den), thr)
        hf_ref[rows[0], :] = h_all[:, 0:C_HEADS * L]
        hb_ref[rows[1], :] = h_all[:, C_HEADS * L:W]

        for d in range(2):
            m_new = jnp.maximum(bend[d] + m_old[d], mx[d])
            decay = jnp.exp(bend[d] + m_old[d] - m_new)
            gain = jnp.exp(mx[d] - m_new)
            m_ref[d] = m_new
            for h in range(C_HEADS):
                u = d * C_HEADS + h
                st_ref[u] = decay[h:h + 1, :] * st_old[u] + gain[h:h + 1, :] * upd[u]
        return carry

    lax.fori_loop(0, nc, step, 0)

    lane = _lane_iota((L, LANES))
    lo_half = lane < HEAD_DIM

    def combine(c, carry):
        rows = pl.ds(pl.multiple_of(c * L, L), L)
        normed = []
        for hd in range(C_HEADS):
            sl = slice(hd * LANES, (hd + 1) * LANES)
            hs = jnp.where(lo_half, hf_ref[rows, sl] + hb_ref[rows, sl], 0.0)
            ms = jnp.sum(hs * hs, axis=1, keepdims=True) * (1.0 / HEAD_DIM)
            normed.append(hs * lax.rsqrt(ms + EPS) * gh_ref[:, sl])
        for j in range(C_HEADS // 2):
            pair = jnp.where(lo_half, normed[2 * j], pltpu.roll(normed[2 * j + 1], HEAD_DIM, 1))
            sl = slice(j * LANES, (j + 1) * LANES)
            y_ref[0, rows, sl] = (og_ref[0, rows, sl].astype(F32) * pair).astype(BF16)
        return carry

    lax.fori_loop(0, nc, combine, 0)


def _mlstm_call(cq, ckt, cv, gr, og, lw, n_ctx_chunks):
    b, t, pad_w = cq.shape

    def per_b(a):
        nd = a.ndim
        return pl.BlockSpec((1,) + a.shape[1:], lambda bi: (bi,) + (0,) * (nd - 1))

    def whole(a):
        nd = a.ndim
        return pl.BlockSpec(a.shape, lambda bi: (0,) * nd)

    kern = functools.partial(_mlstm_kernel, n_ctx_chunks=n_ctx_chunks)
    return pl.pallas_call(
        kern,
        grid=(b,),
        in_specs=[per_b(cq), per_b(ckt), per_b(cv), per_b(gr), per_b(og),
                  whole(lw["bg_r"]), whole(lw["g_head"])],
        out_specs=pl.BlockSpec((1, t, C_WIDTH), lambda bi: (bi, 0, 0)),
        out_shape=jax.ShapeDtypeStruct((b, t, C_WIDTH), BF16),
        scratch_shapes=[
            pltpu.VMEM((t, pad_w), F32),
            pltpu.VMEM((t, pad_w), F32),
            pltpu.VMEM((2 * C_HEADS, LANES, LANES), F32),
            pltpu.VMEM((2, GATE_ROWS, LANES), F32),
            pltpu.VMEM((2, 6, t // M_CHUNK * GATE_ROWS, M_CHUNK), F32),
            pltpu.VMEM((M_CHUNK, 2 * 2 * C_HEADS * M_CHUNK), BF16),
            pltpu.VMEM((M_CHUNK, 2 * C_HEADS * M_CHUNK), F32),
        ],
        compiler_params=pltpu.CompilerParams(
            dimension_semantics=("arbitrary",), vmem_limit_bytes=VMEM_LIMIT_BYTES),
        name="mlstm",
    )(cq, ckt, cv, gr, og, lw["bg_r"], lw["g_head"])


def _out_kernel(h_ref, gt_ref, ya_ref, yb_ref, yc_ref, w_ref, gf_ref, o_ref, *, final):
    acc = _dot(ya_ref[0], w_ref[0:A_WIDTH, :])
    acc += _dot(yb_ref[0], w_ref[A_WIDTH:A_WIDTH + B_WIDTH, :])
    acc += _dot(yc_ref[0], w_ref[A_WIDTH + B_WIDTH:, :])
    x = h_ref[0] + gt_ref[0] * acc
    if final:
        ms = jnp.mean(x * x, axis=-1, keepdims=True)
        x = (x * lax.rsqrt(ms + EPS)) * gf_ref[...]
    o_ref[0] = x


def _out_call(h, gt, ya, yb, yc, w_out, g_final, n_ctx_tiles, tm, final):
    b, t, d = h.shape
    nt = t // tm
    n_mod = gt.shape[0]
    off = n_ctx_tiles if final else 0

    def tile3(w):
        return pl.BlockSpec((1, tm, w), lambda bi, i: (bi, i + off, 0))

    def mod_map(bi, i):
        return (jnp.where(i + off < n_ctx_tiles, n_mod - 1, bi), 0, 0)

    if final:
        out_shape = jax.ShapeDtypeStruct((b, t - off * tm, d), F32)
        out_spec = pl.BlockSpec((1, tm, d), lambda bi, i: (bi, i, 0))
        aliases = {}
    else:
        out_shape = jax.ShapeDtypeStruct((b, t, d), F32)
        out_spec = tile3(d)
        aliases = {0: 0}
    return pl.pallas_call(
        functools.partial(_out_kernel, final=final),
        grid=(b, nt - off),
        in_specs=[tile3(d), pl.BlockSpec((1, 1, d), mod_map), tile3(A_WIDTH), tile3(B_WIDTH),
                  tile3(C_WIDTH),
                  pl.BlockSpec(w_out.shape, lambda bi, i: (0, 0)),
                  pl.BlockSpec(g_final.shape, lambda bi, i: (0, 0))],
        out_specs=out_spec,
        out_shape=out_shape,
        input_output_aliases=aliases,
        compiler_params=pltpu.CompilerParams(
            dimension_semantics=("arbitrary", "arbitrary"), vmem_limit_bytes=VMEM_LIMIT_BYTES),
        name="out_proj",
    )(h, gt, ya, yb, yc, w_out, g_final)


def _rope_tables(ctx_len, seq):
    pos = np.arange(seq)
    freqs = ROPE_THETA ** (-np.arange(ROPE_FREQS, dtype=np.float32) / ROPE_FREQS)
    ang_r = (pos // GRID_W).astype(np.float32)[:, None] * freqs
    ang_c = (pos % GRID_W).astype(np.float32)[:, None] * freqs
    cos_h = np.concatenate([np.cos(ang_r), np.cos(ang_r), np.cos(ang_c), np.cos(ang_c)], axis=1)
    sin_h = np.concatenate([-np.sin(ang_r), np.sin(ang_r), -np.sin(ang_c), np.sin(ang_c)], axis=1)
    cos = np.concatenate([np.ones((ctx_len, HEAD_DIM), np.float32), cos_h.astype(np.float32)], axis=0)
    sin = np.concatenate([np.zeros((ctx_len, HEAD_DIM), np.float32), sin_h.astype(np.float32)], axis=0)
    return jnp.asarray(np.tile(cos, (1, 2))), jnp.asarray(np.tile(sin, (1, 2)))


def _layer_weights(l, n_chunks, g_norm, w_in, w_s, b_s, g_q, g_k, b_gates, g_head, w_out):
    d = w_in.shape[1]
    w_l = w_in[l]
    w_g = w_l[:, OFF_CG:]
    w_gr = jnp.zeros((4, GATE_ROWS, d), F32).at[:, :C_HEADS, :].set(w_g.T.reshape(4, C_HEADS, d))
    bg_r = jnp.zeros((4, GATE_ROWS), F32).at[:, :C_HEADS].set(b_gates[l].reshape(4, C_HEADS))
    gh = jnp.pad(g_head[l].reshape(C_HEADS, HEAD_DIM), ((0, 0), (0, LANES - HEAD_DIM)))
    return {
        "g_norm": g_norm[l].reshape(1, d),
        "w_main": w_l[:, :OFF_CG].astype(BF16),
        "w_gr": w_gr.reshape(4 * GATE_ROWS, d).astype(BF16),
        "w_s": w_s[l].astype(BF16),
        "b_s": jnp.repeat(b_s[l].T, HEAD_DIM, axis=1),
        "g_q": jnp.tile(g_q[l], B_HEADS).reshape(1, B_WIDTH),
        "g_k": jnp.tile(g_k[l], B_KV_HEADS).reshape(1, B_KV_WIDTH),
        "bg_r": jnp.broadcast_to(jnp.tile(bg_r, (1, n_chunks))[:, :, None],
                                 (4, n_chunks * GATE_ROWS, M_CHUNK)),
        "g_head": gh.reshape(1, C_HEADS * LANES),
        "w_out": w_out[l].astype(BF16),
    }


def kernel(x, c, ctx, c_ctx, w_ada, b_ada, g_norm, w_in, w_s, b_s, g_q, g_k, b_gates, g_head, w_out,
           g_final):
    b, seq, d = x.shape
    ctx_len = ctx.shape[1]
    depth = w_ada.shape[0]
    tm = 256 if (ctx_len % 256 == 0 and seq % 256 == 0) else CHUNK
    n_ctx_tiles = ctx_len // tm
    n_ctx_chunks = ctx_len // M_CHUNK

    n_mod = b + 1
    rows = -(-n_mod // 8) * 8
    cc = jnp.zeros((rows, d), F32).at[:b].set(c).at[b].set(c_ctx)
    mods = _modulation(cc, w_ada, b_ada)[:, :n_mod]
    mods = mods.reshape(depth, n_mod, 3, 1, d)

    cos, sin = _rope_tables(ctx_len, seq)
    seg = np.kron(np.eye(B_HEADS, dtype=np.float32), np.full((HEAD_DIM, HEAD_DIM), 1.0 / HEAD_DIM, np.float32))
    consts = {"cos": cos, "sin": sin, "pm": jnp.asarray(seg, BF16)}
    gf = g_final.reshape(1, d)

    h = jnp.concatenate([ctx, x], axis=1)
    for l in range(depth):
        final = l == depth - 1
        lw = _layer_weights(l, (ctx_len + seq) // M_CHUNK, g_norm, w_in, w_s, b_s, g_q, g_k, b_gates,
                            g_head, w_out)
        sh, sc, gt = mods[l, :, 0], mods[l, :, 1], mods[l, :, 2]
        (ya, q, k, vt, zb, cq, ckt, cv, og, gr) = _proj_call(h, sc, sh, lw, consts, n_ctx_tiles, tm)
        yb = _attn_call(q, k, vt, zb, lw["g_q"], lw["g_k"], n_ctx_tiles, tm, with_ctx=not final)
        yc = _mlstm_call(cq, ckt, cv, gr, og, lw, n_ctx_chunks)
        h = _out_call(h, gt, ya, yb, yc, lw["w_out"], gf, n_ctx_tiles, tm, final)
    return h
```

```python
import functools

import jax
import jax.numpy as jnp
import numpy as np
from jax import lax
from jax.experimental import pallas as pl
from jax.experimental.pallas import tpu as pltpu

HEAD_DIM = 64
LANES = 128
GRID_W = 64
EPS = 1e-6
A_GROUPS = 4
A_WIDTH = A_GROUPS * HEAD_DIM
CHUNK = 128
B_HEADS = 8
B_KV_HEADS = 2
B_GROUP = B_HEADS // B_KV_HEADS
B_WIDTH = B_HEADS * HEAD_DIM
B_KV_WIDTH = B_KV_HEADS * HEAD_DIM
ATTN_SCALE = HEAD_DIM ** -0.5
ROPE_THETA = 10000.0
ROPE_FREQS = HEAD_DIM // 4
C_HEADS = 4
C_WIDTH = C_HEADS * HEAD_DIM
M_CHUNK = 128
N_GATES = 4 * C_HEADS
GATE_ROWS = 8

OFF_AU = 0
OFF_AV = OFF_AU + A_WIDTH
OFF_AZ = OFF_AV + A_WIDTH
OFF_BQ = OFF_AZ + A_WIDTH
OFF_BK = OFF_BQ + B_WIDTH
OFF_BV = OFF_BK + B_KV_WIDTH
OFF_BZ = OFF_BV + B_KV_WIDTH
OFF_CQ = OFF_BZ + B_WIDTH
OFF_CK = OFF_CQ + C_WIDTH
OFF_CV = OFF_CK + C_WIDTH
OFF_CO = OFF_CV + C_WIDTH
OFF_CZ = OFF_CO + C_WIDTH
OFF_CG = OFF_CZ + C_WIDTH
D_IN = OFF_CG + N_GATES

VMEM_LIMIT_BYTES = 56 * 1024 * 1024

F32 = jnp.float32
BF16 = jnp.bfloat16
GATE_DTYPE = BF16
HIGHEST = lax.Precision.HIGHEST

QK_PRESCALE = float(ATTN_SCALE * np.log2(np.e))
SAFE_LOGIT_BOUND = 40.0


def _dot(a, b):
    return jnp.dot(a, b, preferred_element_type=F32)


def _dot_nt(a, b):
    return lax.dot_general(a, b, (((1,), (1,)), ((), ())), preferred_element_type=F32)


def _dot_tn(a, b):
    return lax.dot_general(a, b, (((0,), (0,)), ((), ())), preferred_element_type=F32)


def _silu(x):
    return x * jax.nn.sigmoid(x)


def _gelu_tanh(x):
    c = np.float32(np.sqrt(2.0 / np.pi))
    return 0.5 * x * (1.0 + jnp.tanh(c * (x + np.float32(0.044715) * (x * x * x))))


def _log_sigmoid(x):
    return jnp.minimum(x, 0.0) - jnp.log1p(jnp.exp(-jnp.abs(x)))


def _seg_mean(x, pm_ref):
    hi = x.astype(BF16)
    lo = (x - hi.astype(F32)).astype(BF16)
    w = x.shape[-1]
    pm = pm_ref[0:w, 0:w]
    return _dot(hi, pm) + _dot(lo, pm)


def _lane_iota(shape):
    return lax.broadcasted_iota(jnp.int32, shape, len(shape) - 1)


def _mod_kernel(cc_ref, w_ref, b_ref, o_ref):
    s = _silu(cc_ref[...])
    o_ref[0] = jnp.dot(s, w_ref[0], preferred_element_type=F32, precision=HIGHEST) + b_ref[0]


def _modulation(cc, w_ada, b_ada):
    depth, d, d3 = w_ada.shape
    rows = cc.shape[0]
    tn = d
    return pl.pallas_call(
        _mod_kernel,
        grid=(depth, d3 // tn),
        in_specs=[
            pl.BlockSpec((rows, d), lambda l, n: (0, 0)),
            pl.BlockSpec((1, d, tn), lambda l, n: (l, 0, n)),
            pl.BlockSpec((1, 1, tn), lambda l, n: (l, 0, n)),
        ],
        out_specs=pl.BlockSpec((1, rows, tn), lambda l, n: (l, 0, n)),
        out_shape=jax.ShapeDtypeStruct((depth, rows, d3), F32),
        compiler_params=pltpu.CompilerParams(
            dimension_semantics=("arbitrary", "arbitrary"), vmem_limit_bytes=VMEM_LIMIT_BYTES),
        name="modulation",
    )(cc, w_ada, b_ada.reshape(depth, 1, d3))


def _proj_body(x, sc_ref, sh_ref, gn_ref, w_ref, wgr_ref, pm_ref, ws_ref, bs_ref,
               gq_ref, gk_ref, cos_ref, sin_ref,
               ya_ref, q_ref, k_ref, vt_ref, zb_ref, cq_ref, ckt_ref, cv_ref, og_ref, gr_ref):
    tm = x.shape[0]
    ms = jnp.mean(x * x, axis=-1, keepdims=True)
    xn = (x * lax.rsqrt(ms + EPS)) * gn_ref[...]
    xn = xn * (1.0 + sc_ref[0]) + sh_ref[0]
    xb = xn.astype(BF16)

    def proj(off, width):
        return _dot(xb, w_ref[:, off:off + width])

    lane = _lane_iota((tm, LANES))
    lo_half = lane < HEAD_DIM

    gu = _gelu_tanh(proj(OFF_AU, A_WIDTH))
    gv = _gelu_tanh(proj(OFF_AV, A_WIDTH))
    mu = _seg_mean(gv, pm_ref)
    dv = gv - mu
    var = _seg_mean(dv * dv, pm_ref)
    vn = (dv * lax.rsqrt(var + EPS)).astype(BF16)
    sz = _silu(proj(OFF_AZ, A_WIDTH))
    lo_half_chunk = _lane_iota((CHUNK, LANES)) < HEAD_DIM
    for c in range(tm // CHUNK):
        r0, r1 = c * CHUNK, (c + 1) * CHUNK
        for j in range(A_WIDTH // LANES):
            c0, c1 = j * LANES, (j + 1) * LANES
            vs = vn[r0:r1, c0:c1]
            ev = _dot(ws_ref[2 * j], vs)
            od = _dot(ws_ref[2 * j + 1], vs)
            sv = jnp.where(lo_half_chunk, ev, od) + bs_ref[:, c0:c1]
            ya_ref[0, r0:r1, c0:c1] = (gu[r0:r1, c0:c1] * sv * sz[r0:r1, c0:c1]).astype(BF16)

    cos = cos_ref[...]
    sin = sin_ref[...]
    bit4 = (lane & ROPE_FREQS) != 0

    def rope(t):
        partner = jnp.where(bit4, pltpu.roll(t, ROPE_FREQS, 1), pltpu.roll(t, LANES - ROPE_FREQS, 1))
        return t * cos + partner * sin

    half_w = B_WIDTH // 2
    q_halves = []
    for off in (0, half_w):
        qh = proj(OFF_BQ + off, half_w)
        q_halves.append(qh * lax.rsqrt(_seg_mean(qh * qh, pm_ref) + EPS) * gq_ref[:, off:off + half_w])
    for j in range(B_WIDTH // LANES):
        jj = j % (half_w // LANES)
        qs = rope(q_halves[j // (half_w // LANES)][:, jj * LANES:(jj + 1) * LANES]) * QK_PRESCALE
        sw = pltpu.roll(qs, HEAD_DIM, 1)
        if (2 * j) // B_GROUP == 0:
            h_even = jnp.where(lo_half, qs, 0.0)
            h_odd = jnp.where(lo_half, sw, 0.0)
        else:
            h_even = jnp.where(lo_half, 0.0, sw)
            h_odd = jnp.where(lo_half, 0.0, qs)
        q_ref[0, :, (2 * j) * LANES:(2 * j + 1) * LANES] = h_even.astype(BF16)
        q_ref[0, :, (2 * j + 1) * LANES:(2 * j + 2) * LANES] = h_odd.astype(BF16)

    k = proj(OFF_BK, B_KV_WIDTH)
    kn = k * lax.rsqrt(_seg_mean(k * k, pm_ref) + EPS) * gk_ref[...]
    k_ref[0] = rope(kn).astype(BF16)
    vt_ref[0] = proj(OFF_BV, B_KV_WIDTH).T.astype(BF16)
    zb_ref[0] = _silu(proj(OFF_BZ, B_WIDTH)).astype(zb_ref.dtype)

    cq = proj(OFF_CQ, C_WIDTH)
    ck = proj(OFF_CK, C_WIDTH) * (HEAD_DIM ** -0.5)
    cv = proj(OFF_CV, C_WIDTH)
    for j in range(C_WIDTH // LANES):
        sl = slice(j * LANES, (j + 1) * LANES)
        for par in range(2):
            hd = 2 * j + par
            h_sl = slice(hd * LANES, (hd + 1) * LANES)

            def head_tile(src, pad):
                s = src[:, sl] if par == 0 else pltpu.roll(src[:, sl], HEAD_DIM, 1)
                return jnp.where(lo_half, s, pad)

            cq_ref[0, :, h_sl] = head_tile(cq, 0.0).astype(BF16)
            cv_ref[0, :, h_sl] = head_tile(cv, 1.0).astype(BF16)
            kt = head_tile(ck, 0.0)
            for c in range(tm // CHUNK):
                ckt_ref[0, c, h_sl, :] = kt[c * CHUNK:(c + 1) * CHUNK, :].T.astype(BF16)
    og = jax.nn.sigmoid(proj(OFF_CO, C_WIDTH)) * _silu(proj(OFF_CZ, C_WIDTH))
    og_ref[0] = og.astype(og_ref.dtype)
    gr = _dot_nt(wgr_ref[...], xb)
    for c in range(tm // CHUNK):
        for ty in range(4):
            gr_ref[0, ty, c * GATE_ROWS:(c + 1) * GATE_ROWS, :] = (
                gr[ty * GATE_ROWS:(ty + 1) * GATE_ROWS, c * CHUNK:(c + 1) * CHUNK])


N_MIX_IN = 5
N_PROJ_IN = 12


def _layer_kernel(*refs, n_src, mix, proj, final, n_ctx_tiles):
    refs = list(refs)
    src = [refs.pop(0) for _ in range(n_src)]
    if n_src == 1:
        x = src[0][0]
    else:
        x = jnp.where(pl.program_id(1) < n_ctx_tiles, src[0][0], src[1][0])
    if mix:
        gt_ref, ya_ref, yb_ref, yc_ref, w_ref = (refs.pop(0) for _ in range(N_MIX_IN))
        acc = _dot(ya_ref[0], w_ref[0:A_WIDTH, :])
        acc += _dot(yb_ref[0], w_ref[A_WIDTH:A_WIDTH + B_WIDTH, :])
        acc += _dot(yc_ref[0], w_ref[A_WIDTH + B_WIDTH:, :])
        x = x + gt_ref[0] * acc
    if final:
        gf_ref = refs.pop(0)
        ms = jnp.mean(x * x, axis=-1, keepdims=True)
        x = (x * lax.rsqrt(ms + EPS)) * gf_ref[...]
    proj_in = [refs.pop(0) for _ in range(N_PROJ_IN)] if proj else []
    if mix:
        refs.pop(0)[0] = x
    if proj:
        _proj_body(x, *proj_in, *refs)


def _layer_call(src, mix, lw, mod, consts, g_final, n_ctx_tiles, tm):
    n_src = len(src)
    final = lw is None
    b, d = src[0].shape[0], src[0].shape[2]
    t = src[0].shape[1] if n_src == 1 else src[0].shape[1] + src[1].shape[1]
    off = n_ctx_tiles if final else 0
    nt = t // tm - off

    def tile3(w):
        return pl.BlockSpec((1, tm, w), lambda bi, i: (bi, i + off, 0))

    def whole(a):
        nd = a.ndim
        return pl.BlockSpec(a.shape, lambda bi, i: (0,) * nd)

    def mod_spec(a):
        n_mod = a.shape[0]
        return pl.BlockSpec((1, 1, d), lambda bi, i: (jnp.where(i + off < n_ctx_tiles, n_mod - 1, bi), 0, 0))

    if n_src == 1:
        ins, in_specs = [src[0]], [tile3(d)]
    else:
        ins = list(src)
        in_specs = [pl.BlockSpec((1, tm, d), lambda bi, i: (bi, jnp.minimum(i, n_ctx_tiles - 1), 0)),
                    pl.BlockSpec((1, tm, d), lambda bi, i: (bi, jnp.maximum(i - n_ctx_tiles, 0), 0))]
    out_shape, out_specs, aliases = [], [], {}
    if mix is not None:
        gt, ya, yb, yc, w_out = mix
        yb_spec = tile3(B_WIDTH) if yb.shape[1] == t else pl.BlockSpec((1, tm, B_WIDTH), lambda bi, i: (bi, i, 0))
        ins += [gt, ya, yb, yc, w_out]
        in_specs += [mod_spec(gt), tile3(A_WIDTH), yb_spec, tile3(C_WIDTH), whole(w_out)]
        if final:
            ins.append(g_final)
            in_specs.append(whole(g_final))
            out_shape.append(jax.ShapeDtypeStruct((b, t - off * tm, d), F32))
            out_specs.append(pl.BlockSpec((1, tm, d), lambda bi, i: (bi, i, 0)))
        else:
            out_shape.append(jax.ShapeDtypeStruct((b, t, d), F32))
            out_specs.append(tile3(d))
            if n_src == 1:
                aliases = {0: 0}
    if not final:
        sc, sh = mod
        ins += [sc, sh, lw["g_norm"], lw["w_main"], lw["w_gr"], consts["pm"], lw["w_s"],
                lw["b_s"], lw["g_q"], lw["g_k"], consts["cos"], consts["sin"]]
        in_specs += [mod_spec(sc), mod_spec(sh),
                     whole(lw["g_norm"]), whole(lw["w_main"]), whole(lw["w_gr"]),
                     whole(consts["pm"]), whole(lw["w_s"]), whole(lw["b_s"]), whole(lw["g_q"]),
                     whole(lw["g_k"]),
                     pl.BlockSpec((tm, LANES), lambda bi, i: (i, 0)),
                     pl.BlockSpec((tm, LANES), lambda bi, i: (i, 0))]
        pad_w = C_HEADS * LANES
        out_shape += [
            jax.ShapeDtypeStruct((b, t, A_WIDTH), BF16),
            jax.ShapeDtypeStruct((b, t, B_HEADS * LANES), BF16),
            jax.ShapeDtypeStruct((b, t, B_KV_WIDTH), BF16),
            jax.ShapeDtypeStruct((b, B_KV_WIDTH, t), BF16),
            jax.ShapeDtypeStruct((b, t, B_WIDTH), GATE_DTYPE),
            jax.ShapeDtypeStruct((b, t, pad_w), BF16),
            jax.ShapeDtypeStruct((b, t // CHUNK, pad_w, CHUNK), BF16),
            jax.ShapeDtypeStruct((b, t, pad_w), BF16),
            jax.ShapeDtypeStruct((b, t, C_WIDTH), GATE_DTYPE),
            jax.ShapeDtypeStruct((b, 4, t // CHUNK * GATE_ROWS, CHUNK), F32),
        ]
        out_specs += [tile3(A_WIDTH), tile3(B_HEADS * LANES), tile3(B_KV_WIDTH),
                      pl.BlockSpec((1, B_KV_WIDTH, tm), lambda bi, i: (bi, 0, i)),
                      tile3(B_WIDTH), tile3(pad_w),
                      pl.BlockSpec((1, tm // CHUNK, pad_w, CHUNK), lambda bi, i: (bi, i, 0, 0)),
                      tile3(pad_w), tile3(C_WIDTH),
                      pl.BlockSpec((1, 4, tm // CHUNK * GATE_ROWS, CHUNK), lambda bi, i: (bi, 0, i, 0))]
    kern = functools.partial(_layer_kernel, n_src=n_src, mix=mix is not None, proj=not final,
                             final=final, n_ctx_tiles=n_ctx_tiles)
    return pl.pallas_call(
        kern,
        grid=(b, nt),
        in_specs=in_specs,
        out_specs=out_specs,
        out_shape=out_shape,
        input_output_aliases=aliases,
        compiler_params=pltpu.CompilerParams(
            dimension_semantics=("arbitrary", "arbitrary"), vmem_limit_bytes=VMEM_LIMIT_BYTES),
        name="layer_tile",
    )(*ins)


def _attn_kernel(q_ref, k_ref, vt_ref, zb_ref, gq_ref, gk_ref, y_ref, *, n_ctx_tiles, tile_off, ctx_len):
    t_all = k_ref.shape[1]

    tq = q_ref.shape[1]

    def attend(n_keys, stabilise):
        kk = k_ref[0, 0:n_keys, :]
        vt = vt_ref[0, :, 0:n_keys]
        for g in range(B_KV_HEADS):
            q_grp = jnp.concatenate(
                [q_ref[0, :, h * LANES:(h + 1) * LANES] for h in range(g * B_GROUP, (g + 1) * B_GROUP)],
                axis=0)
            s = _dot_nt(kk, q_grp)
            if stabilise:
                s = s - jnp.max(s, axis=0, keepdims=True)
            p = jnp.exp2(s)
            l = jnp.sum(p, axis=0, keepdims=True)
            o = _dot(vt, p.astype(BF16))[g * HEAD_DIM:(g + 1) * HEAD_DIM, :] / l
            for jj in range(B_GROUP // 2):
                pair = jnp.concatenate([o[:, (2 * jj) * tq:(2 * jj + 1) * tq],
                                        o[:, (2 * jj + 1) * tq:(2 * jj + 2) * tq]], axis=0).T
                j = g * (B_GROUP // 2) + jj
                sl = slice(j * LANES, (j + 1) * LANES)
                y_ref[0, :, sl] = (pair * zb_ref[0, :, sl].astype(F32)).astype(BF16)

    bound = (HEAD_DIM * ATTN_SCALE) * jnp.max(jnp.abs(gq_ref[...])) * jnp.max(jnp.abs(gk_ref[...]))
    safe = bound <= SAFE_LOGIT_BOUND
    i = pl.program_id(1) + tile_off
    if n_ctx_tiles > 0 and tile_off == 0:
        branches = ((i < n_ctx_tiles, ctx_len), (i >= n_ctx_tiles, t_all))
    else:
        branches = ((None, t_all),)
    for cond, n_keys in branches:
        for stabilise in (False, True):
            pred = safe if not stabilise else jnp.logical_not(safe)
            if cond is not None:
                pred = jnp.logical_and(pred, cond)
            pl.when(pred)(functools.partial(attend, n_keys, stabilise))


def _attn_call(q, k, vt, zb, g_q, g_k, n_ctx_tiles, tq, with_ctx):
    b, t, _ = q.shape
    nt = t // tq
    tile_off = 0 if with_ctx else n_ctx_tiles
    kern = functools.partial(_attn_kernel, n_ctx_tiles=n_ctx_tiles, tile_off=tile_off,
                             ctx_len=n_ctx_tiles * tq)
    return pl.pallas_call(
        kern,
        grid=(b, nt - tile_off),
        in_specs=[
            pl.BlockSpec((1, tq, B_HEADS * LANES), lambda bi, i: (bi, i + tile_off, 0)),
            pl.BlockSpec((1, t, B_KV_WIDTH), lambda bi, i: (bi, 0, 0)),
            pl.BlockSpec((1, B_KV_WIDTH, t), lambda bi, i: (bi, 0, 0)),
            pl.BlockSpec((1, tq, B_WIDTH), lambda bi, i: (bi, i + tile_off, 0)),
            pl.BlockSpec(g_q.shape, lambda bi, i: (0, 0)),
            pl.BlockSpec(g_k.shape, lambda bi, i: (0, 0)),
        ],
        out_specs=pl.BlockSpec((1, tq, B_WIDTH), lambda bi, i: (bi, i, 0)),
        out_shape=jax.ShapeDtypeStruct((b, t - tile_off * tq, B_WIDTH), BF16),
        compiler_params=pltpu.CompilerParams(
            dimension_semantics=("arbitrary", "arbitrary"), vmem_limit_bytes=VMEM_LIMIT_BYTES),
        name="attention",
    )(q, k, vt, zb, g_q, g_k)


def _mlstm_kernel(cq_ref, ckt_ref, cv_ref, gr_ref, og_ref, br_ref, gh_ref,
                  y_ref, hf_ref, hb_ref, st_ref, m_ref, gp_ref, sel_ref, madd_ref, *, n_ctx_chunks):
    t = cq_ref.shape[1]
    nc = t // M_CHUNK
    L = M_CHUNK
    n_units = 2 * C_HEADS
    W = n_units * L
    n_rows = nc * GATE_ROWS
    n_spread = 4 * GATE_ROWS
    row = lax.broadcasted_iota(jnp.int32, (L, L), 0)
    col = lax.broadcasted_iota(jnp.int32, (L, L), 1)
    tri_row = tuple(jnp.where(mk, 1.0, 0.0).astype(F32) for mk in (row <= col, row >= col))
    ones_ll = jnp.ones((L, L), F32)
    eye_bf = jnp.where(row == col, 1.0, 0.0).astype(BF16)
    @pl.when(pl.program_id(0) == 0)
    def _():
        srow = lax.broadcasted_iota(jnp.int32, (L, 2 * W), 0)
        scol = lax.broadcasted_iota(jnp.int32, (L, 2 * W), 1)
        r_in = srow % n_spread
        sel_ref[...] = jnp.where((srow < 3 * n_spread)
                                 & ((r_in // (2 * GATE_ROWS)) == (scol // W))
                                 & (((r_in // GATE_ROWS) % 2) == ((scol // (C_HEADS * L)) % 2))
                                 & ((r_in % GATE_ROWS) == ((scol // L) % C_HEADS)), 1.0, 0.0).astype(BF16)
        mrow = lax.broadcasted_iota(jnp.int32, (L, W), 0)
        mcol = lax.broadcasted_iota(jnp.int32, (L, W), 1)
        upward = mcol < C_HEADS * L
        hidden = (upward & ((mcol % L) > mrow)) | (jnp.logical_not(upward) & ((mcol % L) < mrow))
        madd_ref[...] = jnp.where(hidden, -jnp.inf, 0.0).astype(F32)

    st_ref[...] = jnp.zeros(st_ref.shape, F32)
    m_ref[...] = jnp.zeros(m_ref.shape, F32)

    lane_r = _lane_iota((n_rows, L))
    for d in range(2):
        li = gr_ref[0, 2 * d] + br_ref[2 * d]
        lf = _log_sigmoid(gr_ref[0, 2 * d + 1] + br_ref[2 * d + 1])
        b = jnp.dot(lf, tri_row[d], preferred_element_type=F32, precision=HIGHEST)
        bend = jnp.dot(lf, ones_ll, preferred_element_type=F32, precision=HIGHEST)
        g = li - b
        cm = g
        sh = 1
        while sh < L:
            if d == 0:
                cm = jnp.maximum(cm, jnp.where(lane_r >= sh, pltpu.roll(cm, sh, 1), -jnp.inf))
            else:
                cm = jnp.maximum(cm, jnp.where(lane_r < L - sh, pltpu.roll(cm, L - sh, 1), -jnp.inf))
            sh *= 2
        w_end = bend - b + li
        mx = jnp.max(w_end, axis=1, keepdims=True)
        gp_ref[d, 0] = g
        gp_ref[d, 1] = cm
        gp_ref[d, 2] = b
        gp_ref[d, 3] = jnp.exp(w_end - mx)
        gp_ref[d, 4] = bend
        gp_ref[d, 5] = jnp.broadcast_to(mx, (n_rows, L))

    def unit_lanes(per_dir):
        return jnp.concatenate([per_dir[d][h:h + 1, :] for d in range(2) for h in range(C_HEADS)], axis=1)

    def step(i, carry):
        chunks = (i, jnp.where(i < n_ctx_chunks, n_ctx_chunks - 1 - i, nc + n_ctx_chunks - 1 - i))
        rows = tuple(pl.ds(pl.multiple_of(c * L, L), L) for c in chunks)
        grow = tuple(pl.ds(pl.multiple_of(c * GATE_ROWS, GATE_ROWS), GATE_ROWS) for c in chunks)
        g, cm, b, wk0, bend, mx = (tuple(gp_ref[d, k, grow[d], :] for d in range(2)) for k in range(6))
        m_old = (m_ref[0], m_ref[1])

        r32 = jnp.concatenate([cm[0], cm[1], b[0], b[1]], axis=0)
        hi = r32.astype(BF16)
        r1 = r32 - hi.astype(F32)
        mid = r1.astype(BF16)
        lo = (r1 - mid.astype(F32)).astype(BF16)
        pieces = jnp.concatenate([hi, mid, lo, jnp.zeros((L - 3 * n_spread, L), BF16)], axis=0)
        pieces_t = _dot_nt(eye_bf, pieces).astype(BF16)
        spread = _dot(pieces_t, sel_ref[...])
        cm_all = spread[:, 0:W]
        b_all = spread[:, W:2 * W]

        e0 = jnp.exp((unit_lanes(g) - cm_all) + madd_ref[...])

        q_d = tuple(cq_ref[0, rows[d], :] for d in range(2))
        v_d = tuple(cv_ref[0, rows[d], :] for d in range(2))
        kt_d = tuple(ckt_ref[0, chunks[d]] for d in range(2))
        units = [(d, h) for d in range(2) for h in range(C_HEADS)]
        st_old, qk, bq = [], [], []
        for u, (d, h) in enumerate(units):
            sl = slice(h * LANES, (h + 1) * LANES)
            st = st_ref[u]
            st_old.append(st)
            r = _dot(q_d[d][:, sl], jnp.concatenate([kt_d[d][sl, :], st.astype(BF16)], axis=1))
            qk.append(r[:, 0:L])
            bq.append(r[:, L:2 * L])
        s0 = (jnp.concatenate(qk, axis=1) * e0).astype(BF16)

        acc, upd = [], []
        for u, (d, h) in enumerate(units):
            sl = slice(h * LANES, (h + 1) * LANES)
            kw = (kt_d[d][sl, :].astype(F32) * wk0[d][h:h + 1, :]).astype(BF16)
            r = _dot(jnp.concatenate([s0[:, u * L:(u + 1) * L], kw], axis=0), v_d[d][:, sl])
            acc.append(r[0:L])
            upd.append(r[L:2 * L])

        m_all = unit_lanes(m_old)
        above = cm_all >= m_all
        x = jnp.exp(-jnp.abs(cm_all - m_all))
        f = jnp.where(above, 1.0, x)
        inter = jnp.where(above, x, 1.0)
        na = f * jnp.concatenate(acc, axis=1) + inter * jnp.concatenate(bq, axis=1)
        den = jnp.concatenate([pltpu.roll(na[:, u * L:(u + 1) * L], HEAD_DIM, 1) for u in range(n_units)],
                              axis=1)
        thr = jnp.exp(-(b_all + jnp.maximum(cm_all, m_all)))
        h_all = na / jnp.maximum(jnp.abs(den), thr)
        hf_ref[rows[0], :] = h_all[:, 0:C_HEADS * L]
        hb_ref[rows[1], :] = h_all[:, C_HEADS * L:W]

        for d in range(2):
            m_new = jnp.maximum(bend[d] + m_old[d], mx[d])
            decay = jnp.exp(bend[d] + m_old[d] - m_new)
            gain = jnp.exp(mx[d] - m_new)
            m_ref[d] = m_new
            for h in range(C_HEADS):
                u = d * C_HEADS + h
                st_ref[u] = decay[h:h + 1, :] * st_old[u] + gain[h:h + 1, :] * upd[u]
        return carry

    lax.fori_loop(0, nc, step, 0)

    lane = _lane_iota((L, LANES))
    lo_half = lane < HEAD_DIM

    def combine(c, carry):
        rows = pl.ds(pl.multiple_of(c * L, L), L)
        normed = []
        for hd in range(C_HEADS):
            sl = slice(hd * LANES, (hd + 1) * LANES)
            hs = jnp.where(lo_half, hf_ref[rows, sl] + hb_ref[rows, sl], 0.0)
            ms = jnp.sum(hs * hs, axis=1, keepdims=True) * (1.0 / HEAD_DIM)
            normed.append(hs * lax.rsqrt(ms + EPS) * gh_ref[:, sl])
        for j in range(C_HEADS // 2):
            pair = jnp.where(lo_half, normed[2 * j], pltpu.roll(normed[2 * j + 1], HEAD_DIM, 1))
            sl = slice(j * LANES, (j + 1) * LANES)
            y_ref[0, rows, sl] = (og_ref[0, rows, sl].astype(F32) * pair).astype(BF16)
        return carry

    lax.fori_loop(0, nc, combine, 0)


def _mlstm_call(cq, ckt, cv, gr, og, lw, n_ctx_chunks):
    b, t, pad_w = cq.shape

    def per_b(a):
        nd = a.ndim
        return pl.BlockSpec((1,) + a.shape[1:], lambda bi: (bi,) + (0,) * (nd - 1))

    def whole(a):
        nd = a.ndim
        return pl.BlockSpec(a.shape, lambda bi: (0,) * nd)

    kern = functools.partial(_mlstm_kernel, n_ctx_chunks=n_ctx_chunks)
    return pl.pallas_call(
        kern,
        grid=(b,),
        in_specs=[per_b(cq), per_b(ckt), per_b(cv), per_b(gr), per_b(og),
                  whole(lw["bg_r"]), whole(lw["g_head"])],
        out_specs=pl.BlockSpec((1, t, C_WIDTH), lambda bi: (bi, 0, 0)),
        out_shape=jax.ShapeDtypeStruct((b, t, C_WIDTH), BF16),
        scratch_shapes=[
            pltpu.VMEM((t, pad_w), F32),
            pltpu.VMEM((t, pad_w), F32),
            pltpu.VMEM((2 * C_HEADS, LANES, LANES), F32),
            pltpu.VMEM((2, GATE_ROWS, LANES), F32),
            pltpu.VMEM((2, 6, t // M_CHUNK * GATE_ROWS, M_CHUNK), F32),
            pltpu.VMEM((M_CHUNK, 2 * 2 * C_HEADS * M_CHUNK), BF16),
            pltpu.VMEM((M_CHUNK, 2 * C_HEADS * M_CHUNK), F32),
        ],
        compiler_params=pltpu.CompilerParams(
            dimension_semantics=("arbitrary",), vmem_limit_bytes=VMEM_LIMIT_BYTES),
        name="mlstm",
    )(cq, ckt, cv, gr, og, lw["bg_r"], lw["g_head"])


def _rope_tables(ctx_len, seq):
    pos = np.arange(seq)
    freqs = ROPE_THETA ** (-np.arange(ROPE_FREQS, dtype=np.float32) / ROPE_FREQS)
    ang_r = (pos // GRID_W).astype(np.float32)[:, None] * freqs
    ang_c = (pos % GRID_W).astype(np.float32)[:, None] * freqs
    cos_h = np.concatenate([np.cos(ang_r), np.cos(ang_r), np.cos(ang_c), np.cos(ang_c)], axis=1)
    sin_h = np.concatenate([-np.sin(ang_r), np.sin(ang_r), -np.sin(ang_c), np.sin(ang_c)], axis=1)
    cos = np.concatenate([np.ones((ctx_len, HEAD_DIM), np.float32), cos_h.astype(np.float32)], axis=0)
    sin = np.concatenate([np.zeros((ctx_len, HEAD_DIM), np.float32), sin_h.astype(np.float32)], axis=0)
    return jnp.asarray(np.tile(cos, (1, 2))), jnp.asarray(np.tile(sin, (1, 2)))


def _layer_weights(l, n_chunks, g_norm, w_in, w_s, b_s, g_q, g_k, b_gates, g_head, w_out):
    d = w_in.shape[1]
    w_l = w_in[l]
    w_g = w_l[:, OFF_CG:]
    w_gr = jnp.zeros((4, GATE_ROWS, d), F32).at[:, :C_HEADS, :].set(w_g.T.reshape(4, C_HEADS, d))
    bg_r = jnp.zeros((4, GATE_ROWS), F32).at[:, :C_HEADS].set(b_gates[l].reshape(4, C_HEADS))
    gh = jnp.pad(g_head[l].reshape(C_HEADS, HEAD_DIM), ((0, 0), (0, LANES - HEAD_DIM)))
    return {
        "g_norm": g_norm[l].reshape(1, d),
        "w_main": w_l[:, :OFF_CG].astype(BF16),
        "w_gr": w_gr.reshape(4 * GATE_ROWS, d).astype(BF16),
        "w_s": w_s[l].astype(BF16),
        "b_s": jnp.repeat(b_s[l].T, HEAD_DIM, axis=1),
        "g_q": jnp.tile(g_q[l], B_HEADS).reshape(1, B_WIDTH),
        "g_k": jnp.tile(g_k[l], B_KV_HEADS).reshape(1, B_KV_WIDTH),
        "bg_r": jnp.broadcast_to(jnp.tile(bg_r, (1, n_chunks))[:, :, None],
                                 (4, n_chunks * GATE_ROWS, M_CHUNK)),
        "g_head": gh.reshape(1, C_HEADS * LANES),
        "w_out": w_out[l].astype(BF16),
    }


def kernel(x, c, ctx, c_ctx, w_ada, b_ada, g_norm, w_in, w_s, b_s, g_q, g_k, b_gates, g_head, w_out,
           g_final):
    b, seq, d = x.shape
    ctx_len = ctx.shape[1]
    depth = w_ada.shape[0]
    tm = 256 if (ctx_len % 256 == 0 and seq % 256 == 0) else CHUNK
    n_ctx_tiles = ctx_len // tm
    n_ctx_chunks = ctx_len // M_CHUNK

    n_mod = b + 1
    rows = -(-n_mod // 8) * 8
    cc = jnp.zeros((rows, d), F32).at[:b].set(c).at[b].set(c_ctx)
    mods = _modulation(cc, w_ada, b_ada)[:, :n_mod]
    mods = mods.reshape(depth, n_mod, 3, 1, d)

    cos, sin = _rope_tables(ctx_len, seq)
    seg = np.kron(np.eye(B_HEADS, dtype=np.float32), np.full((HEAD_DIM, HEAD_DIM), 1.0 / HEAD_DIM, np.float32))
    consts = {"cos": cos, "sin": sin, "pm": jnp.asarray(seg, BF16)}
    gf = g_final.reshape(1, d)

    src = (ctx, x)
    mix = None
    for l in range(depth + 1):
        lw = mod = None
        if l < depth:
            lw = _layer_weights(l, (ctx_len + seq) // M_CHUNK, g_norm, w_in, w_s, b_s, g_q, g_k, b_gates,
                                g_head, w_out)
            sh, sc, gt = mods[l, :, 0], mods[l, :, 1], mods[l, :, 2]
            mod = (sc, sh)
        outs = _layer_call(src, mix, lw, mod, consts, gf, n_ctx_tiles, tm)
        if mix is not None:
            src = (outs[0],)
            outs = outs[1:]
        if l == depth:
            return src[0]
        (ya, q, k, vt, zb, cq, ckt, cv, og, gr) = outs
        yb = _attn_call(q, k, vt, zb, lw["g_q"], lw["g_k"], n_ctx_tiles, tm, with_ctx=l < depth - 1)
        yc = _mlstm_call(cq, ckt, cv, gr, og, lw, n_ctx_chunks)
        mix = (gt, ya, yb, yc, lw["w_out"])
```

```python
import functools

import jax
import jax.numpy as jnp
import numpy as np
from jax import lax
from jax.experimental import pallas as pl
from jax.experimental.pallas import tpu as pltpu

HEAD_DIM = 64
LANES = 128
GRID_W = 64
EPS = 1e-6
A_GROUPS = 4
A_WIDTH = A_GROUPS * HEAD_DIM
CHUNK = 128
B_HEADS = 8
B_KV_HEADS = 2
B_GROUP = B_HEADS // B_KV_HEADS
B_WIDTH = B_HEADS * HEAD_DIM
B_KV_WIDTH = B_KV_HEADS * HEAD_DIM
ATTN_SCALE = HEAD_DIM ** -0.5
ROPE_THETA = 10000.0
ROPE_FREQS = HEAD_DIM // 4
C_HEADS = 4
C_WIDTH = C_HEADS * HEAD_DIM
M_CHUNK = 128
N_GATES = 4 * C_HEADS
GATE_ROWS = 8

OFF_AU = 0
OFF_AV = OFF_AU + A_WIDTH
OFF_AZ = OFF_AV + A_WIDTH
OFF_BQ = OFF_AZ + A_WIDTH
OFF_BK = OFF_BQ + B_WIDTH
OFF_BV = OFF_BK + B_KV_WIDTH
OFF_BZ = OFF_BV + B_KV_WIDTH
OFF_CQ = OFF_BZ + B_WIDTH
OFF_CK = OFF_CQ + C_WIDTH
OFF_CV = OFF_CK + C_WIDTH
OFF_CO = OFF_CV + C_WIDTH
OFF_CZ = OFF_CO + C_WIDTH
OFF_CG = OFF_CZ + C_WIDTH
D_IN = OFF_CG + N_GATES

VMEM_LIMIT_BYTES = 56 * 1024 * 1024

F32 = jnp.float32
BF16 = jnp.bfloat16
GATE_DTYPE = BF16
HIGHEST = lax.Precision.HIGHEST

LOG2E = float(np.log2(np.e))
QK_PRESCALE = ATTN_SCALE * LOG2E
SAFE_LOGIT_BOUND = 40.0


def _dot(a, b):
    return jnp.dot(a, b, preferred_element_type=F32)


def _dot_nt(a, b):
    return lax.dot_general(a, b, (((1,), (1,)), ((), ())), preferred_element_type=F32)


def _dot_tn(a, b):
    return lax.dot_general(a, b, (((0,), (0,)), ((), ())), preferred_element_type=F32)


def _silu(x):
    return x * jax.nn.sigmoid(x)


def _gelu_tanh(x):
    c = np.float32(np.sqrt(2.0 / np.pi))
    return 0.5 * x * (1.0 + jnp.tanh(c * (x + np.float32(0.044715) * (x * x * x))))


def _log_sigmoid(x):
    return jnp.minimum(x, 0.0) - jnp.log1p(jnp.exp(-jnp.abs(x)))


def _seg_mean(x, pm_ref):
    hi = x.astype(BF16)
    lo = (x - hi.astype(F32)).astype(BF16)
    w = x.shape[-1]
    pm = pm_ref[0:w, 0:w]
    return _dot(hi, pm) + _dot(lo, pm)


def _lane_iota(shape):
    return lax.broadcasted_iota(jnp.int32, shape, len(shape) - 1)


def _mod_kernel(cc_ref, w_ref, b_ref, o_ref):
    s = _silu(cc_ref[...])
    o_ref[0] = jnp.dot(s, w_ref[0], preferred_element_type=F32, precision=HIGHEST) + b_ref[0]


def _modulation(cc, w_ada, b_ada):
    depth, d, d3 = w_ada.shape
    rows = cc.shape[0]
    tn = d
    return pl.pallas_call(
        _mod_kernel,
        grid=(depth, d3 // tn),
        in_specs=[
            pl.BlockSpec((rows, d), lambda l, n: (0, 0)),
            pl.BlockSpec((1, d, tn), lambda l, n: (l, 0, n)),
            pl.BlockSpec((1, 1, tn), lambda l, n: (l, 0, n)),
        ],
        out_specs=pl.BlockSpec((1, rows, tn), lambda l, n: (l, 0, n)),
        out_shape=jax.ShapeDtypeStruct((depth, rows, d3), F32),
        compiler_params=pltpu.CompilerParams(
            dimension_semantics=("arbitrary", "arbitrary"), vmem_limit_bytes=VMEM_LIMIT_BYTES),
        name="modulation",
    )(cc, w_ada, b_ada.reshape(depth, 1, d3))


def _proj_body(x, sc_ref, sh_ref, gn_ref, w_ref, wgr_ref, pm_ref, ws_ref, bs_ref,
               gq_ref, gk_ref, cos_ref, sin_ref,
               ya_ref, q_ref, k_ref, vt_ref, zb_ref, cq_ref, ckt_ref, cv_ref, og_ref, gr_ref):
    tm = x.shape[0]
    ms = jnp.mean(x * x, axis=-1, keepdims=True)
    xn = (x * lax.rsqrt(ms + EPS)) * gn_ref[...]
    xn = xn * (1.0 + sc_ref[0]) + sh_ref[0]
    xb = xn.astype(BF16)

    wide = {}

    def proj(off, width):
        for lo, hi in ((OFF_AU, OFF_BQ), (OFF_BQ, OFF_CQ), (OFF_CQ, OFF_CG)):
            if lo <= off < hi:
                if lo not in wide:
                    wide[lo] = _dot(xb, w_ref[:, lo:hi])
                return wide[lo][:, off - lo:off - lo + width]
        raise ValueError(off)

    lane = _lane_iota((tm, LANES))
    lo_half = lane < HEAD_DIM

    gu = _gelu_tanh(proj(OFF_AU, A_WIDTH))
    gv = _gelu_tanh(proj(OFF_AV, A_WIDTH))
    mu = _seg_mean(gv, pm_ref)
    dv = gv - mu
    var = _seg_mean(dv * dv, pm_ref)
    vn = (dv * lax.rsqrt(var + EPS)).astype(BF16)
    sz = _silu(proj(OFF_AZ, A_WIDTH))
    lo_half_chunk = _lane_iota((CHUNK, LANES)) < HEAD_DIM
    for c in range(tm // CHUNK):
        r0, r1 = c * CHUNK, (c + 1) * CHUNK
        for j in range(A_WIDTH // LANES):
            c0, c1 = j * LANES, (j + 1) * LANES
            vs = vn[r0:r1, c0:c1]
            ev = _dot(ws_ref[2 * j], vs)
            od = _dot(ws_ref[2 * j + 1], vs)
            sv = jnp.where(lo_half_chunk, ev, od) + bs_ref[:, c0:c1]
            ya_ref[0, r0:r1, c0:c1] = (gu[r0:r1, c0:c1] * sv * sz[r0:r1, c0:c1]).astype(BF16)

    cos = cos_ref[...]
    sin = sin_ref[...]
    bit4 = (lane & ROPE_FREQS) != 0

    def rope(t):
        partner = jnp.where(bit4, pltpu.roll(t, ROPE_FREQS, 1), pltpu.roll(t, LANES - ROPE_FREQS, 1))
        return t * cos + partner * sin

    half_w = B_WIDTH // 2
    q_halves = []
    for off in (0, half_w):
        qh = proj(OFF_BQ + off, half_w)
        q_halves.append(qh * lax.rsqrt(_seg_mean(qh * qh, pm_ref) + EPS) * gq_ref[:, off:off + half_w])
    for j in range(B_WIDTH // LANES):
        jj = j % (half_w // LANES)
        qs = rope(q_halves[j // (half_w // LANES)][:, jj * LANES:(jj + 1) * LANES]) * QK_PRESCALE
        sw = pltpu.roll(qs, HEAD_DIM, 1)
        if (2 * j) // B_GROUP == 0:
            h_even = jnp.where(lo_half, qs, 0.0)
            h_odd = jnp.where(lo_half, sw, 0.0)
        else:
            h_even = jnp.where(lo_half, 0.0, sw)
            h_odd = jnp.where(lo_half, 0.0, qs)
        q_ref[0, :, (2 * j) * LANES:(2 * j + 1) * LANES] = h_even.astype(BF16)
        q_ref[0, :, (2 * j + 1) * LANES:(2 * j + 2) * LANES] = h_odd.astype(BF16)

    k = proj(OFF_BK, B_KV_WIDTH)
    kn = k * lax.rsqrt(_seg_mean(k * k, pm_ref) + EPS) * gk_ref[...]
    k_ref[0] = rope(kn).astype(BF16)
    vt_ref[0] = proj(OFF_BV, B_KV_WIDTH).T.astype(BF16)
    zb_ref[0] = _silu(proj(OFF_BZ, B_WIDTH)).astype(zb_ref.dtype)

    cq = proj(OFF_CQ, C_WIDTH)
    ck = proj(OFF_CK, C_WIDTH) * (HEAD_DIM ** -0.5)
    cv = proj(OFF_CV, C_WIDTH)
    for j in range(C_WIDTH // LANES):
        sl = slice(j * LANES, (j + 1) * LANES)
        for par in range(2):
            hd = 2 * j + par
            h_sl = slice(hd * LANES, (hd + 1) * LANES)

            def head_tile(src, pad):
                s = src[:, sl] if par == 0 else pltpu.roll(src[:, sl], HEAD_DIM, 1)
                return jnp.where(lo_half, s, pad)

            cq_ref[0, :, h_sl] = head_tile(cq, 0.0).astype(BF16)
            cv_ref[0, :, h_sl] = head_tile(cv, 1.0).astype(BF16)
            kt = head_tile(ck, 0.0)
            for c in range(tm // CHUNK):
                ckt_ref[0, c, h_sl, :] = kt[c * CHUNK:(c + 1) * CHUNK, :].T.astype(BF16)
    og = jax.nn.sigmoid(proj(OFF_CO, C_WIDTH)) * _silu(proj(OFF_CZ, C_WIDTH))
    og_ref[0] = og.astype(og_ref.dtype)
    gr = _dot_nt(wgr_ref[...], xb)
    for c in range(tm // CHUNK):
        for ty in range(4):
            gr_ref[0, ty, c * GATE_ROWS:(c + 1) * GATE_ROWS, :] = (
                gr[ty * GATE_ROWS:(ty + 1) * GATE_ROWS, c * CHUNK:(c + 1) * CHUNK])


N_MIX_IN = 5
N_PROJ_IN = 12


def _layer_kernel(*refs, n_src, mix, proj, final, n_ctx_tiles):
    refs = list(refs)
    src = [refs.pop(0) for _ in range(n_src)]
    if n_src == 1:
        x = src[0][0]
    else:
        x = jnp.where(pl.program_id(1) < n_ctx_tiles, src[0][0], src[1][0])
    if mix:
        gt_ref, ya_ref, yb_ref, yc_ref, w_ref = (refs.pop(0) for _ in range(N_MIX_IN))
        acc = _dot(ya_ref[0], w_ref[0:A_WIDTH, :])
        acc += _dot(yb_ref[0], w_ref[A_WIDTH:A_WIDTH + B_WIDTH, :])
        acc += _dot(yc_ref[0], w_ref[A_WIDTH + B_WIDTH:, :])
        x = x + gt_ref[0] * acc
    if final:
        gf_ref = refs.pop(0)
        ms = jnp.mean(x * x, axis=-1, keepdims=True)
        x = (x * lax.rsqrt(ms + EPS)) * gf_ref[...]
    proj_in = [refs.pop(0) for _ in range(N_PROJ_IN)] if proj else []
    if mix:
        refs.pop(0)[0] = x
    if proj:
        _proj_body(x, *proj_in, *refs)


def _layer_call(src, mix, lw, mod, consts, g_final, n_ctx_tiles, tm):
    n_src = len(src)
    final = lw is None
    b, d = src[0].shape[0], src[0].shape[2]
    t = src[0].shape[1] if n_src == 1 else src[0].shape[1] + src[1].shape[1]
    off = n_ctx_tiles if final else 0
    nt = t // tm - off

    def tile3(w):
        return pl.BlockSpec((1, tm, w), lambda bi, i: (bi, i + off, 0))

    def whole(a):
        nd = a.ndim
        return pl.BlockSpec(a.shape, lambda bi, i: (0,) * nd)

    def mod_spec(a):
        n_mod = a.shape[0]
        return pl.BlockSpec((1, 1, d), lambda bi, i: (jnp.where(i + off < n_ctx_tiles, n_mod - 1, bi), 0, 0))

    if n_src == 1:
        ins, in_specs = [src[0]], [tile3(d)]
    else:
        ins = list(src)
        in_specs = [pl.BlockSpec((1, tm, d), lambda bi, i: (bi, jnp.minimum(i, n_ctx_tiles - 1), 0)),
                    pl.BlockSpec((1, tm, d), lambda bi, i: (bi, jnp.maximum(i - n_ctx_tiles, 0), 0))]
    out_shape, out_specs, aliases = [], [], {}
    if mix is not None:
        gt, ya, yb, yc, w_out = mix
        yb_spec = tile3(B_WIDTH) if yb.shape[1] == t else pl.BlockSpec((1, tm, B_WIDTH), lambda bi, i: (bi, i, 0))
        ins += [gt, ya, yb, yc, w_out]
        in_specs += [mod_spec(gt), tile3(A_WIDTH), yb_spec, tile3(C_WIDTH), whole(w_out)]
        if final:
            ins.append(g_final)
            in_specs.append(whole(g_final))
            out_shape.append(jax.ShapeDtypeStruct((b, t - off * tm, d), F32))
            out_specs.append(pl.BlockSpec((1, tm, d), lambda bi, i: (bi, i, 0)))
        else:
            out_shape.append(jax.ShapeDtypeStruct((b, t, d), F32))
            out_specs.append(tile3(d))
            if n_src == 1:
                aliases = {0: 0}
    if not final:
        sc, sh = mod
        ins += [sc, sh, lw["g_norm"], lw["w_main"], lw["w_gr"], consts["pm"], lw["w_s"],
                lw["b_s"], lw["g_q"], lw["g_k"], consts["cos"], consts["sin"]]
        in_specs += [mod_spec(sc), mod_spec(sh),
                     whole(lw["g_norm"]), whole(lw["w_main"]), whole(lw["w_gr"]),
                     whole(consts["pm"]), whole(lw["w_s"]), whole(lw["b_s"]), whole(lw["g_q"]),
                     whole(lw["g_k"]),
                     pl.BlockSpec((tm, LANES), lambda bi, i: (i, 0)),
                     pl.BlockSpec((tm, LANES), lambda bi, i: (i, 0))]
        pad_w = C_HEADS * LANES
        out_shape += [
            jax.ShapeDtypeStruct((b, t, A_WIDTH), BF16),
            jax.ShapeDtypeStruct((b, t, B_HEADS * LANES), BF16),
            jax.ShapeDtypeStruct((b, t, B_KV_WIDTH), BF16),
            jax.ShapeDtypeStruct((b, B_KV_WIDTH, t), BF16),
            jax.ShapeDtypeStruct((b, t, B_WIDTH), GATE_DTYPE),
            jax.ShapeDtypeStruct((b, t, pad_w), BF16),
            jax.ShapeDtypeStruct((b, t // CHUNK, pad_w, CHUNK), BF16),
            jax.ShapeDtypeStruct((b, t, pad_w), BF16),
            jax.ShapeDtypeStruct((b, t, C_WIDTH), GATE_DTYPE),
            jax.ShapeDtypeStruct((b, 4, t // CHUNK * GATE_ROWS, CHUNK), F32),
        ]
        out_specs += [tile3(A_WIDTH), tile3(B_HEADS * LANES), tile3(B_KV_WIDTH),
                      pl.BlockSpec((1, B_KV_WIDTH, tm), lambda bi, i: (bi, 0, i)),
                      tile3(B_WIDTH), tile3(pad_w),
                      pl.BlockSpec((1, tm // CHUNK, pad_w, CHUNK), lambda bi, i: (bi, i, 0, 0)),
                      tile3(pad_w), tile3(C_WIDTH),
                      pl.BlockSpec((1, 4, tm // CHUNK * GATE_ROWS, CHUNK), lambda bi, i: (bi, 0, i, 0))]
    kern = functools.partial(_layer_kernel, n_src=n_src, mix=mix is not None, proj=not final,
                             final=final, n_ctx_tiles=n_ctx_tiles)
    return pl.pallas_call(
        kern,
        grid=(b, nt),
        in_specs=in_specs,
        out_specs=out_specs,
        out_shape=out_shape,
        input_output_aliases=aliases,
        compiler_params=pltpu.CompilerParams(
            dimension_semantics=("arbitrary", "arbitrary"), vmem_limit_bytes=VMEM_LIMIT_BYTES),
        name="layer_tile",
    )(*ins)


def _attn_kernel(q_ref, k_ref, vt_ref, zb_ref, gq_ref, gk_ref, y_ref, *, n_ctx_tiles, tile_off, ctx_len):
    t_all = k_ref.shape[1]

    tq = q_ref.shape[1]

    def attend(n_keys, stabilise):
        kk = k_ref[0, 0:n_keys, :]
        vt = vt_ref[0, :, 0:n_keys]
        for g in range(B_KV_HEADS):
            q_grp = jnp.concatenate(
                [q_ref[0, :, h * LANES:(h + 1) * LANES] for h in range(g * B_GROUP, (g + 1) * B_GROUP)],
                axis=0)
            s = _dot_nt(kk, q_grp)
            if stabilise:
                s = s - jnp.max(s, axis=0, keepdims=True)
            p = jnp.exp2(s)
            l = jnp.sum(p, axis=0, keepdims=True)
            o = _dot(vt, p.astype(BF16))[g * HEAD_DIM:(g + 1) * HEAD_DIM, :] / l
            for jj in range(B_GROUP // 2):
                pair = jnp.concatenate([o[:, (2 * jj) * tq:(2 * jj + 1) * tq],
                                        o[:, (2 * jj + 1) * tq:(2 * jj + 2) * tq]], axis=0).T
                j = g * (B_GROUP // 2) + jj
                sl = slice(j * LANES, (j + 1) * LANES)
                y_ref[0, :, sl] = (pair * zb_ref[0, :, sl].astype(F32)).astype(BF16)

    bound = (HEAD_DIM * ATTN_SCALE) * jnp.max(jnp.abs(gq_ref[...])) * jnp.max(jnp.abs(gk_ref[...]))
    safe = bound <= SAFE_LOGIT_BOUND
    i = pl.program_id(1) + tile_off
    if n_ctx_tiles > 0 and tile_off == 0:
        branches = ((i < n_ctx_tiles, ctx_len), (i >= n_ctx_tiles, t_all))
    else:
        branches = ((None, t_all),)
    for cond, n_keys in branches:
        for stabilise in (False, True):
            pred = safe if not stabilise else jnp.logical_not(safe)
            if cond is not None:
                pred = jnp.logical_and(pred, cond)
            pl.when(pred)(functools.partial(attend, n_keys, stabilise))


def _attn_call(q, k, vt, zb, g_q, g_k, n_ctx_tiles, tq, with_ctx):
    b, t, _ = q.shape
    nt = t // tq
    tile_off = 0 if with_ctx else n_ctx_tiles
    kern = functools.partial(_attn_kernel, n_ctx_tiles=n_ctx_tiles, tile_off=tile_off,
                             ctx_len=n_ctx_tiles * tq)
    return pl.pallas_call(
        kern,
        grid=(b, nt - tile_off),
        in_specs=[
            pl.BlockSpec((1, tq, B_HEADS * LANES), lambda bi, i: (bi, i + tile_off, 0)),
            pl.BlockSpec((1, t, B_KV_WIDTH), lambda bi, i: (bi, 0, 0)),
            pl.BlockSpec((1, B_KV_WIDTH, t), lambda bi, i: (bi, 0, 0)),
            pl.BlockSpec((1, tq, B_WIDTH), lambda bi, i: (bi, i + tile_off, 0)),
            pl.BlockSpec(g_q.shape, lambda bi, i: (0, 0)),
            pl.BlockSpec(g_k.shape, lambda bi, i: (0, 0)),
        ],
        out_specs=pl.BlockSpec((1, tq, B_WIDTH), lambda bi, i: (bi, i, 0)),
        out_shape=jax.ShapeDtypeStruct((b, t - tile_off * tq, B_WIDTH), BF16),
        compiler_params=pltpu.CompilerParams(
            dimension_semantics=("arbitrary", "arbitrary"), vmem_limit_bytes=VMEM_LIMIT_BYTES),
        name="attention",
    )(q, k, vt, zb, g_q, g_k)


def _mlstm_kernel(cq_ref, ckt_ref, cv_ref, gr_ref, og_ref, br_ref, gh_ref,
                  y_ref, hf_ref, hb_ref, st_ref, m_ref, gp_ref, sel_ref, madd_ref, *, n_ctx_chunks):
    t = cq_ref.shape[1]
    nc = t // M_CHUNK
    L = M_CHUNK
    n_units = 2 * C_HEADS
    W = n_units * L
    n_rows = nc * GATE_ROWS
    n_spread = 4 * GATE_ROWS
    row = lax.broadcasted_iota(jnp.int32, (L, L), 0)
    col = lax.broadcasted_iota(jnp.int32, (L, L), 1)
    tri_row = tuple(jnp.where(mk, 1.0, 0.0).astype(F32) for mk in (row <= col, row >= col))
    ones_ll = jnp.ones((L, L), F32)
    eye_bf = jnp.where(row == col, 1.0, 0.0).astype(BF16)
    @pl.when(pl.program_id(0) == 0)
    def _():
        srow = lax.broadcasted_iota(jnp.int32, (L, 2 * W), 0)
        scol = lax.broadcasted_iota(jnp.int32, (L, 2 * W), 1)
        r_in = srow % n_spread
        sel_ref[...] = jnp.where((srow < 3 * n_spread)
                                 & ((r_in // (2 * GATE_ROWS)) == (scol // W))
                                 & (((r_in // GATE_ROWS) % 2) == ((scol // (C_HEADS * L)) % 2))
                                 & ((r_in % GATE_ROWS) == ((scol // L) % C_HEADS)), 1.0, 0.0).astype(BF16)
        mrow = lax.broadcasted_iota(jnp.int32, (L, W), 0)
        mcol = lax.broadcasted_iota(jnp.int32, (L, W), 1)
        upward = mcol < C_HEADS * L
        hidden = (upward & ((mcol % L) > mrow)) | (jnp.logical_not(upward) & ((mcol % L) < mrow))
        madd_ref[...] = jnp.where(hidden, -jnp.inf, 0.0).astype(F32)

    st_ref[...] = jnp.zeros(st_ref.shape, F32)
    m_ref[...] = jnp.zeros(m_ref.shape, F32)

    lane_r = _lane_iota((n_rows, L))
    for d in range(2):
        li = (gr_ref[0, 2 * d] + br_ref[2 * d]) * LOG2E
        lf = _log_sigmoid(gr_ref[0, 2 * d + 1] + br_ref[2 * d + 1]) * LOG2E
        b = jnp.dot(lf, tri_row[d], preferred_element_type=F32, precision=HIGHEST)
        bend = jnp.dot(lf, ones_ll, preferred_element_type=F32, precision=HIGHEST)
        g = li - b
        cm = g
        sh = 1
        while sh < L:
            if d == 0:
                cm = jnp.maximum(cm, jnp.where(lane_r >= sh, pltpu.roll(cm, sh, 1), -jnp.inf))
            else:
                cm = jnp.maximum(cm, jnp.where(lane_r < L - sh, pltpu.roll(cm, L - sh, 1), -jnp.inf))
            sh *= 2
        w_end = bend - b + li
        mx = jnp.max(w_end, axis=1, keepdims=True)
        gp_ref[d, 0] = g
        gp_ref[d, 1] = cm
        gp_ref[d, 2] = -(b + cm)
        gp_ref[d, 3] = jnp.exp2(w_end - mx)
        gp_ref[d, 4] = bend
        gp_ref[d, 5] = jnp.broadcast_to(mx, (n_rows, L))

    def unit_lanes(per_dir):
        return jnp.concatenate([per_dir[d][h:h + 1, :] for d in range(2) for h in range(C_HEADS)], axis=1)

    def chunks_of(i):
        return (i, jnp.where(i < n_ctx_chunks, n_ctx_chunks - 1 - i, nc + n_ctx_chunks - 1 - i))

    def gate_rows(chunks):
        return tuple(pl.ds(pl.multiple_of(c * GATE_ROWS, GATE_ROWS), GATE_ROWS) for c in chunks)

    def step(i, carry):
        chunks = chunks_of(i)
        rows = tuple(pl.ds(pl.multiple_of(c * L, L), L) for c in chunks)
        grow = gate_rows(chunks)
        g, cm, nbc, wk0, bend, mx = (tuple(gp_ref[d, k, grow[d], :] for d in range(2)) for k in range(6))
        m_old = (m_ref[0], m_ref[1])

        decay, wk = [], []
        for d in range(2):
            m_new = jnp.maximum(bend[d] + m_old[d], mx[d])
            decay.append(jnp.exp2(bend[d] + m_old[d] - m_new))
            wk.append(wk0[d] * jnp.exp2(mx[d] - m_new))
            m_ref[d] = m_new

        r32 = jnp.concatenate([cm[0], cm[1], nbc[0], nbc[1]], axis=0)
        hi = r32.astype(BF16)
        r1 = r32 - hi.astype(F32)
        mid = r1.astype(BF16)
        lo = (r1 - mid.astype(F32)).astype(BF16)
        pieces = jnp.concatenate([hi, mid, lo, jnp.zeros((L - 3 * n_spread, L), BF16)], axis=0)
        pieces_t = _dot_nt(eye_bf, pieces).astype(BF16)
        spread = _dot(pieces_t, sel_ref[...])
        cm_all = spread[:, 0:W]
        nbc_all = spread[:, W:2 * W]

        e0 = jnp.exp2((unit_lanes(g) - cm_all) + madd_ref[...])

        q_d = tuple(cq_ref[0, rows[d], :] for d in range(2))
        v_d = tuple(cv_ref[0, rows[d], :] for d in range(2))
        kt_d = tuple(ckt_ref[0, chunks[d]] for d in range(2))
        units = [(d, h) for d in range(2) for h in range(C_HEADS)]
        st_old, qk, bq = [], [], []
        for u, (d, h) in enumerate(units):
            sl = slice(h * LANES, (h + 1) * LANES)
            st = st_ref[u]
            st_old.append(st)
            r = _dot(q_d[d][:, sl], jnp.concatenate([kt_d[d][sl, :], st.astype(BF16)], axis=1))
            qk.append(r[:, 0:L])
            bq.append(r[:, L:2 * L])
        s0 = (jnp.concatenate(qk, axis=1) * e0).astype(BF16)

        acc, upd = [], []
        for u, (d, h) in enumerate(units):
            sl = slice(h * LANES, (h + 1) * LANES)
            kw = (kt_d[d][sl, :].astype(F32) * wk[d][h:h + 1, :]).astype(BF16)
            r = _dot(jnp.concatenate([s0[:, u * L:(u + 1) * L], kw], axis=0), v_d[d][:, sl])
            acc.append(r[0:L])
            upd.append(r[L:2 * L])

        diff = cm_all - unit_lanes(m_old)
        below = jnp.minimum(diff, 0.0)
        f = jnp.exp2(below)
        inter = jnp.exp2(jnp.minimum(-diff, 0.0))
        na = f * jnp.concatenate(acc, axis=1) + inter * jnp.concatenate(bq, axis=1)
        den = jnp.concatenate([pltpu.roll(na[:, u * L:(u + 1) * L], HEAD_DIM, 1) for u in range(n_units)],
                              axis=1)
        thr = jnp.exp2(nbc_all + below)
        h_all = na / jnp.maximum(jnp.abs(den), thr)
        hf_ref[rows[0], :] = h_all[:, 0:C_HEADS * L]
        hb_ref[rows[1], :] = h_all[:, C_HEADS * L:W]

        for u, (d, h) in enumerate(units):
            st_ref[u] = decay[d][h:h + 1, :] * st_old[u] + upd[u]
        return carry

    lax.fori_loop(0, nc, step, 0)

    lane = _lane_iota((L, LANES))
    lo_half = lane < HEAD_DIM

    def combine(c, carry):
        rows = pl.ds(pl.multiple_of(c * L, L), L)
        normed = []
        for hd in range(C_HEADS):
            sl = slice(hd * LANES, (hd + 1) * LANES)
            hs = jnp.where(lo_half, hf_ref[rows, sl] + hb_ref[rows, sl], 0.0)
            ms = jnp.sum(hs * hs, axis=1, keepdims=True) * (1.0 / HEAD_DIM)
            normed.append(hs * lax.rsqrt(ms + EPS) * gh_ref[:, sl])
        for j in range(C_HEADS // 2):
            pair = jnp.where(lo_half, normed[2 * j], pltpu.roll(normed[2 * j + 1], HEAD_DIM, 1))
            sl = slice(j * LANES, (j + 1) * LANES)
            y_ref[0, rows, sl] = (og_ref[0, rows, sl].astype(F32) * pair).astype(BF16)
        return carry

    lax.fori_loop(0, nc, combine, 0)


def _mlstm_call(cq, ckt, cv, gr, og, lw, n_ctx_chunks):
    b, t, pad_w = cq.shape

    def per_b(a):
        nd = a.ndim
        return pl.BlockSpec((1,) + a.shape[1:], lambda bi: (bi,) + (0,) * (nd - 1))

    def whole(a):
        nd = a.ndim
        return pl.BlockSpec(a.shape, lambda bi: (0,) * nd)

    kern = functools.partial(_mlstm_kernel, n_ctx_chunks=n_ctx_chunks)
    return pl.pallas_call(
        kern,
        grid=(b,),
        in_specs=[per_b(cq), per_b(ckt), per_b(cv), per_b(gr), per_b(og),
                  whole(lw["bg_r"]), whole(lw["g_head"])],
        out_specs=pl.BlockSpec((1, t, C_WIDTH), lambda bi: (bi, 0, 0)),
        out_shape=jax.ShapeDtypeStruct((b, t, C_WIDTH), BF16),
        scratch_shapes=[
            pltpu.VMEM((t, pad_w), F32),
            pltpu.VMEM((t, pad_w), F32),
            pltpu.VMEM((2 * C_HEADS, LANES, LANES), F32),
            pltpu.VMEM((2, GATE_ROWS, LANES), F32),
            pltpu.VMEM((2, 6, t // M_CHUNK * GATE_ROWS, M_CHUNK), F32),
            pltpu.VMEM((M_CHUNK, 2 * 2 * C_HEADS * M_CHUNK), BF16),
            pltpu.VMEM((M_CHUNK, 2 * C_HEADS * M_CHUNK), F32),
        ],
        compiler_params=pltpu.CompilerParams(
            dimension_semantics=("arbitrary",), vmem_limit_bytes=VMEM_LIMIT_BYTES),
        name="mlstm",
    )(cq, ckt, cv, gr, og, lw["bg_r"], lw["g_head"])


def _rope_tables(ctx_len, seq):
    pos = np.arange(seq)
    freqs = ROPE_THETA ** (-np.arange(ROPE_FREQS, dtype=np.float32) / ROPE_FREQS)
    ang_r = (pos // GRID_W).astype(np.float32)[:, None] * freqs
    ang_c = (pos % GRID_W).astype(np.float32)[:, None] * freqs
    cos_h = np.concatenate([np.cos(ang_r), np.cos(ang_r), np.cos(ang_c), np.cos(ang_c)], axis=1)
    sin_h = np.concatenate([-np.sin(ang_r), np.sin(ang_r), -np.sin(ang_c), np.sin(ang_c)], axis=1)
    cos = np.concatenate([np.ones((ctx_len, HEAD_DIM), np.float32), cos_h.astype(np.float32)], axis=0)
    sin = np.concatenate([np.zeros((ctx_len, HEAD_DIM), np.float32), sin_h.astype(np.float32)], axis=0)
    return jnp.asarray(np.tile(cos, (1, 2))), jnp.asarray(np.tile(sin, (1, 2)))


def _layer_weights(l, n_chunks, g_norm, w_in, w_s, b_s, g_q, g_k, b_gates, g_head, w_out):
    d = w_in.shape[1]
    w_l = w_in[l]
    w_g = w_l[:, OFF_CG:]
    w_gr = jnp.zeros((4, GATE_ROWS, d), F32).at[:, :C_HEADS, :].set(w_g.T.reshape(4, C_HEADS, d))
    bg_r = jnp.zeros((4, GATE_ROWS), F32).at[:, :C_HEADS].set(b_gates[l].reshape(4, C_HEADS))
    gh = jnp.pad(g_head[l].reshape(C_HEADS, HEAD_DIM), ((0, 0), (0, LANES - HEAD_DIM)))
    return {
        "g_norm": g_norm[l].reshape(1, d),
        "w_main": w_l[:, :OFF_CG].astype(BF16),
        "w_gr": w_gr.reshape(4 * GATE_ROWS, d).astype(BF16),
        "w_s": w_s[l].astype(BF16),
        "b_s": jnp.repeat(b_s[l].T, HEAD_DIM, axis=1),
        "g_q": jnp.tile(g_q[l], B_HEADS).reshape(1, B_WIDTH),
        "g_k": jnp.tile(g_k[l], B_KV_HEADS).reshape(1, B_KV_WIDTH),
        "bg_r": jnp.broadcast_to(jnp.tile(bg_r, (1, n_chunks))[:, :, None],
                                 (4, n_chunks * GATE_ROWS, M_CHUNK)),
        "g_head": gh.reshape(1, C_HEADS * LANES),
        "w_out": w_out[l].astype(BF16),
    }


def kernel(x, c, ctx, c_ctx, w_ada, b_ada, g_norm, w_in, w_s, b_s, g_q, g_k, b_gates, g_head, w_out,
           g_final):
    b, seq, d = x.shape
    ctx_len = ctx.shape[1]
    depth = w_ada.shape[0]
    tm = 256 if (ctx_len % 256 == 0 and seq % 256 == 0) else CHUNK
    n_ctx_tiles = ctx_len // tm
    n_ctx_chunks = ctx_len // M_CHUNK

    n_mod = b + 1
    rows = -(-n_mod // 8) * 8
    cc = jnp.zeros((rows, d), F32).at[:b].set(c).at[b].set(c_ctx)
    mods = _modulation(cc, w_ada, b_ada)[:, :n_mod]
    mods = mods.reshape(depth, n_mod, 3, 1, d)

    cos, sin = _rope_tables(ctx_len, seq)
    seg = np.kron(np.eye(B_HEADS, dtype=np.float32), np.full((HEAD_DIM, HEAD_DIM), 1.0 / HEAD_DIM, np.float32))
    consts = {"cos": cos, "sin": sin, "pm": jnp.asarray(seg, BF16)}
    gf = g_final.reshape(1, d)

    src = (ctx, x)
    mix = None
    for l in range(depth + 1):
        lw = mod = None
        if l < depth:
            lw = _layer_weights(l, (ctx_len + seq) // M_CHUNK, g_norm, w_in, w_s, b_s, g_q, g_k, b_gates,
                                g_head, w_out)
            sh, sc, gt = mods[l, :, 0], mods[l, :, 1], mods[l, :, 2]
            mod = (sc, sh)
        outs = _layer_call(src, mix, lw, mod, consts, gf, n_ctx_tiles, tm)
        if mix is not None:
            src = (outs[0],)
            outs = outs[1:]
        if l == depth:
            return src[0]
        (ya, q, k, vt, zb, cq, ckt, cv, og, gr) = outs
        yb = _attn_call(q, k, vt, zb, lw["g_q"], lw["g_k"], n_ctx_tiles, tm, with_ctx=l < depth - 1)
        yc = _mlstm_call(cq, ckt, cv, gr, og, lw, n_ctx_chunks)
        mix = (gt, ya, yb, yc, lw["w_out"])
```

```python
import functools

import jax
import jax.numpy as jnp
import numpy as np
from jax import lax
from jax.experimental import pallas as pl
from jax.experimental.pallas import tpu as pltpu

HEAD_DIM = 64
LANES = 128
GRID_W = 64
EPS = 1e-6
A_GROUPS = 4
A_WIDTH = A_GROUPS * HEAD_DIM
CHUNK = 128
B_HEADS = 8
B_KV_HEADS = 2
B_GROUP = B_HEADS // B_KV_HEADS
B_WIDTH = B_HEADS * HEAD_DIM
B_KV_WIDTH = B_KV_HEADS * HEAD_DIM
ATTN_SCALE = HEAD_DIM ** -0.5
ROPE_THETA = 10000.0
ROPE_FREQS = HEAD_DIM // 4
C_HEADS = 4
C_WIDTH = C_HEADS * HEAD_DIM
M_CHUNK = 128
N_GATES = 4 * C_HEADS
GATE_ROWS = 8

OFF_AU = 0
OFF_AV = OFF_AU + A_WIDTH
OFF_AZ = OFF_AV + A_WIDTH
OFF_BQ = OFF_AZ + A_WIDTH
OFF_BK = OFF_BQ + B_WIDTH
OFF_BV = OFF_BK + B_KV_WIDTH
OFF_BZ = OFF_BV + B_KV_WIDTH
OFF_CQ = OFF_BZ + B_WIDTH
OFF_CK = OFF_CQ + C_WIDTH
OFF_CV = OFF_CK + C_WIDTH
OFF_CO = OFF_CV + C_WIDTH
OFF_CZ = OFF_CO + C_WIDTH
OFF_CG = OFF_CZ + C_WIDTH
D_IN = OFF_CG + N_GATES

VMEM_LIMIT_BYTES = 56 * 1024 * 1024

F32 = jnp.float32
BF16 = jnp.bfloat16
GATE_DTYPE = BF16
HIGHEST = lax.Precision.HIGHEST

LOG2E = float(np.log2(np.e))
QK_PRESCALE = ATTN_SCALE * LOG2E
SAFE_LOGIT_BOUND = 40.0


def _dot(a, b):
    return jnp.dot(a, b, preferred_element_type=F32)


def _dot_nt(a, b):
    return lax.dot_general(a, b, (((1,), (1,)), ((), ())), preferred_element_type=F32)


def _dot_tn(a, b):
    return lax.dot_general(a, b, (((0,), (0,)), ((), ())), preferred_element_type=F32)


def _silu(x):
    return x * jax.nn.sigmoid(x)


def _gelu_tanh(x):
    c = np.float32(np.sqrt(2.0 / np.pi))
    return 0.5 * x * (1.0 + jnp.tanh(c * (x + np.float32(0.044715) * (x * x * x))))


def _log_sigmoid(x):
    return jnp.minimum(x, 0.0) - jnp.log1p(jnp.exp(-jnp.abs(x)))


def _seg_mean(x, pm_ref):
    hi = x.astype(BF16)
    lo = (x - hi.astype(F32)).astype(BF16)
    w = x.shape[-1]
    pm = pm_ref[0:w, 0:w]
    return _dot(hi, pm) + _dot(lo, pm)


def _lane_iota(shape):
    return lax.broadcasted_iota(jnp.int32, shape, len(shape) - 1)


def _mod_kernel(cc_ref, w_ref, b_ref, o_ref):
    s = _silu(cc_ref[...])
    o_ref[0] = jnp.dot(s, w_ref[0], preferred_element_type=F32, precision=HIGHEST) + b_ref[0]


def _modulation(cc, w_ada, b_ada):
    depth, d, d3 = w_ada.shape
    rows = cc.shape[0]
    tn = d
    return pl.pallas_call(
        _mod_kernel,
        grid=(depth, d3 // tn),
        in_specs=[
            pl.BlockSpec((rows, d), lambda l, n: (0, 0)),
            pl.BlockSpec((1, d, tn), lambda l, n: (l, 0, n)),
            pl.BlockSpec((1, 1, tn), lambda l, n: (l, 0, n)),
        ],
        out_specs=pl.BlockSpec((1, rows, tn), lambda l, n: (l, 0, n)),
        out_shape=jax.ShapeDtypeStruct((depth, rows, d3), F32),
        compiler_params=pltpu.CompilerParams(
            dimension_semantics=("arbitrary", "arbitrary"), vmem_limit_bytes=VMEM_LIMIT_BYTES),
        name="modulation",
    )(cc, w_ada, b_ada.reshape(depth, 1, d3))


def _proj_body(x, sc_ref, sh_ref, gn_ref, w_ref, wgr_ref, pm_ref, ws_ref, bs_ref,
               gq_ref, gk_ref, cos_ref, sin_ref,
               ya_ref, q_ref, k_ref, vt_ref, zb_ref, cqt_ref, ck_ref, cvt_ref, og_ref, gr_ref):
    tm = x.shape[0]
    ms = jnp.mean(x * x, axis=-1, keepdims=True)
    xn = (x * lax.rsqrt(ms + EPS)) * gn_ref[...]
    xn = xn * (1.0 + sc_ref[0]) + sh_ref[0]
    xb = xn.astype(BF16)

    wide = {}

    def proj(off, width):
        for lo, hi in ((OFF_AU, OFF_BQ), (OFF_BQ, OFF_CQ), (OFF_CQ, OFF_CG)):
            if lo <= off < hi:
                if lo not in wide:
                    wide[lo] = _dot(xb, w_ref[:, lo:hi])
                return wide[lo][:, off - lo:off - lo + width]
        raise ValueError(off)

    lane = _lane_iota((tm, LANES))
    lo_half = lane < HEAD_DIM

    gu = _gelu_tanh(proj(OFF_AU, A_WIDTH))
    gv = _gelu_tanh(proj(OFF_AV, A_WIDTH))
    mu = _seg_mean(gv, pm_ref)
    dv = gv - mu
    var = _seg_mean(dv * dv, pm_ref)
    vn = (dv * lax.rsqrt(var + EPS)).astype(BF16)
    sz = _silu(proj(OFF_AZ, A_WIDTH))
    lo_half_chunk = _lane_iota((CHUNK, LANES)) < HEAD_DIM
    for c in range(tm // CHUNK):
        r0, r1 = c * CHUNK, (c + 1) * CHUNK
        for j in range(A_WIDTH // LANES):
            c0, c1 = j * LANES, (j + 1) * LANES
            vs = vn[r0:r1, c0:c1]
            ev = _dot(ws_ref[2 * j], vs)
            od = _dot(ws_ref[2 * j + 1], vs)
            sv = jnp.where(lo_half_chunk, ev, od) + bs_ref[:, c0:c1]
            ya_ref[0, r0:r1, c0:c1] = (gu[r0:r1, c0:c1] * sv * sz[r0:r1, c0:c1]).astype(BF16)

    cos = cos_ref[...]
    sin = sin_ref[...]
    bit4 = (lane & ROPE_FREQS) != 0

    def rope(t):
        partner = jnp.where(bit4, pltpu.roll(t, ROPE_FREQS, 1), pltpu.roll(t, LANES - ROPE_FREQS, 1))
        return t * cos + partner * sin

    half_w = B_WIDTH // 2
    q_halves = []
    for off in (0, half_w):
        qh = proj(OFF_BQ + off, half_w)
        q_halves.append(qh * lax.rsqrt(_seg_mean(qh * qh, pm_ref) + EPS) * gq_ref[:, off:off + half_w])
    for j in range(B_WIDTH // LANES):
        jj = j % (half_w // LANES)
        qs = rope(q_halves[j // (half_w // LANES)][:, jj * LANES:(jj + 1) * LANES]) * QK_PRESCALE
        sw = pltpu.roll(qs, HEAD_DIM, 1)
        if (2 * j) // B_GROUP == 0:
            h_even = jnp.where(lo_half, qs, 0.0)
            h_odd = jnp.where(lo_half, sw, 0.0)
        else:
            h_even = jnp.where(lo_half, 0.0, sw)
            h_odd = jnp.where(lo_half, 0.0, qs)
        q_ref[0, :, (2 * j) * LANES:(2 * j + 1) * LANES] = h_even.astype(BF16)
        q_ref[0, :, (2 * j + 1) * LANES:(2 * j + 2) * LANES] = h_odd.astype(BF16)

    k = proj(OFF_BK, B_KV_WIDTH)
    kn = k * lax.rsqrt(_seg_mean(k * k, pm_ref) + EPS) * gk_ref[...]
    k_ref[0] = rope(kn).astype(BF16)
    vt_ref[0] = proj(OFF_BV, B_KV_WIDTH).T.astype(BF16)
    zb_ref[0] = _silu(proj(OFF_BZ, B_WIDTH)).astype(zb_ref.dtype)

    cq = proj(OFF_CQ, C_WIDTH)
    ck = proj(OFF_CK, C_WIDTH) * (HEAD_DIM ** -0.5)
    cv = proj(OFF_CV, C_WIDTH)
    for j in range(C_WIDTH // LANES):
        sl = slice(j * LANES, (j + 1) * LANES)
        for par in range(2):
            hd = 2 * j + par
            h_sl = slice(hd * LANES, (hd + 1) * LANES)

            def head_tile(src, pad):
                s = src[:, sl] if par == 0 else pltpu.roll(src[:, sl], HEAD_DIM, 1)
                return jnp.where(lo_half, s, pad)

            ck_ref[0, :, h_sl] = head_tile(ck, 0.0).astype(BF16)
            qt = head_tile(cq, 0.0)
            vt = head_tile(cv, 1.0)
            for c in range(tm // CHUNK):
                cqt_ref[0, c, h_sl, :] = qt[c * CHUNK:(c + 1) * CHUNK, :].T.astype(BF16)
                cvt_ref[0, c, h_sl, :] = vt[c * CHUNK:(c + 1) * CHUNK, :].T.astype(BF16)
    og = jax.nn.sigmoid(proj(OFF_CO, C_WIDTH)) * _silu(proj(OFF_CZ, C_WIDTH))
    og_ref[0] = og.astype(og_ref.dtype)
    gr = _dot_nt(wgr_ref[...], xb)
    for c in range(tm // CHUNK):
        for ty in range(4):
            gr_ref[0, ty, c * GATE_ROWS:(c + 1) * GATE_ROWS, :] = (
                gr[ty * GATE_ROWS:(ty + 1) * GATE_ROWS, c * CHUNK:(c + 1) * CHUNK])


N_MIX_IN = 5
N_PROJ_IN = 12


def _layer_kernel(*refs, n_src, mix, proj, final, n_ctx_tiles):
    refs = list(refs)
    src = [refs.pop(0) for _ in range(n_src)]
    if n_src == 1:
        x = src[0][0]
    else:
        x = jnp.where(pl.program_id(1) < n_ctx_tiles, src[0][0], src[1][0])
    if mix:
        gt_ref, ya_ref, yb_ref, yc_ref, w_ref = (refs.pop(0) for _ in range(N_MIX_IN))
        acc = _dot(ya_ref[0], w_ref[0:A_WIDTH, :])
        acc += _dot(yb_ref[0], w_ref[A_WIDTH:A_WIDTH + B_WIDTH, :])
        acc += _dot(yc_ref[0], w_ref[A_WIDTH + B_WIDTH:, :])
        x = x + gt_ref[0] * acc
    if final:
        gf_ref = refs.pop(0)
        ms = jnp.mean(x * x, axis=-1, keepdims=True)
        x = (x * lax.rsqrt(ms + EPS)) * gf_ref[...]
    proj_in = [refs.pop(0) for _ in range(N_PROJ_IN)] if proj else []
    if mix:
        refs.pop(0)[0] = x
    if proj:
        _proj_body(x, *proj_in, *refs)


def _layer_call(src, mix, lw, mod, consts, g_final, n_ctx_tiles, tm):
    n_src = len(src)
    final = lw is None
    b, d = src[0].shape[0], src[0].shape[2]
    t = src[0].shape[1] if n_src == 1 else src[0].shape[1] + src[1].shape[1]
    off = n_ctx_tiles if final else 0
    nt = t // tm - off

    def tile3(w):
        return pl.BlockSpec((1, tm, w), lambda bi, i: (bi, i + off, 0))

    def whole(a):
        nd = a.ndim
        return pl.BlockSpec(a.shape, lambda bi, i: (0,) * nd)

    def mod_spec(a):
        n_mod = a.shape[0]
        return pl.BlockSpec((1, 1, d), lambda bi, i: (jnp.where(i + off < n_ctx_tiles, n_mod - 1, bi), 0, 0))

    if n_src == 1:
        ins, in_specs = [src[0]], [tile3(d)]
    else:
        ins = list(src)
        in_specs = [pl.BlockSpec((1, tm, d), lambda bi, i: (bi, jnp.minimum(i, n_ctx_tiles - 1), 0)),
                    pl.BlockSpec((1, tm, d), lambda bi, i: (bi, jnp.maximum(i - n_ctx_tiles, 0), 0))]
    out_shape, out_specs, aliases = [], [], {}
    if mix is not None:
        gt, ya, yb, yc, w_out = mix
        yb_spec = tile3(B_WIDTH) if yb.shape[1] == t else pl.BlockSpec((1, tm, B_WIDTH), lambda bi, i: (bi, i, 0))
        ins += [gt, ya, yb, yc, w_out]
        in_specs += [mod_spec(gt), tile3(A_WIDTH), yb_spec, tile3(C_WIDTH), whole(w_out)]
        if final:
            ins.append(g_final)
            in_specs.append(whole(g_final))
            out_shape.append(jax.ShapeDtypeStruct((b, t - off * tm, d), F32))
            out_specs.append(pl.BlockSpec((1, tm, d), lambda bi, i: (bi, i, 0)))
        else:
            out_shape.append(jax.ShapeDtypeStruct((b, t, d), F32))
            out_specs.append(tile3(d))
            if n_src == 1:
                aliases = {0: 0}
    if not final:
        sc, sh = mod
        ins += [sc, sh, lw["g_norm"], lw["w_main"], lw["w_gr"], consts["pm"], lw["w_s"],
                lw["b_s"], lw["g_q"], lw["g_k"], consts["cos"], consts["sin"]]
        in_specs += [mod_spec(sc), mod_spec(sh),
                     whole(lw["g_norm"]), whole(lw["w_main"]), whole(lw["w_gr"]),
                     whole(consts["pm"]), whole(lw["w_s"]), whole(lw["b_s"]), whole(lw["g_q"]),
                     whole(lw["g_k"]),
                     pl.BlockSpec((tm, LANES), lambda bi, i: (i, 0)),
                     pl.BlockSpec((tm, LANES), lambda bi, i: (i, 0))]
        pad_w = C_HEADS * LANES
        out_shape += [
            jax.ShapeDtypeStruct((b, t, A_WIDTH), BF16),
            jax.ShapeDtypeStruct((b, t, B_HEADS * LANES), BF16),
            jax.ShapeDtypeStruct((b, t, B_KV_WIDTH), BF16),
            jax.ShapeDtypeStruct((b, B_KV_WIDTH, t), BF16),
            jax.ShapeDtypeStruct((b, t, B_WIDTH), GATE_DTYPE),
            jax.ShapeDtypeStruct((b, t // CHUNK, pad_w, CHUNK), BF16),
            jax.ShapeDtypeStruct((b, t, pad_w), BF16),
            jax.ShapeDtypeStruct((b, t // CHUNK, pad_w, CHUNK), BF16),
            jax.ShapeDtypeStruct((b, t, C_WIDTH), GATE_DTYPE),
            jax.ShapeDtypeStruct((b, 4, t // CHUNK * GATE_ROWS, CHUNK), F32),
        ]
        out_specs += [tile3(A_WIDTH), tile3(B_HEADS * LANES), tile3(B_KV_WIDTH),
                      pl.BlockSpec((1, B_KV_WIDTH, tm), lambda bi, i: (bi, 0, i)),
                      tile3(B_WIDTH),
                      pl.BlockSpec((1, tm // CHUNK, pad_w, CHUNK), lambda bi, i: (bi, i, 0, 0)),
                      tile3(pad_w),
                      pl.BlockSpec((1, tm // CHUNK, pad_w, CHUNK), lambda bi, i: (bi, i, 0, 0)),
                      tile3(C_WIDTH),
                      pl.BlockSpec((1, 4, tm // CHUNK * GATE_ROWS, CHUNK), lambda bi, i: (bi, 0, i, 0))]
    kern = functools.partial(_layer_kernel, n_src=n_src, mix=mix is not None, proj=not final,
                             final=final, n_ctx_tiles=n_ctx_tiles)
    return pl.pallas_call(
        kern,
        grid=(b, nt),
        in_specs=in_specs,
        out_specs=out_specs,
        out_shape=out_shape,
        input_output_aliases=aliases,
        compiler_params=pltpu.CompilerParams(
            dimension_semantics=("arbitrary", "arbitrary"), vmem_limit_bytes=VMEM_LIMIT_BYTES),
        name="layer_tile",
    )(*ins)


def _attn_kernel(q_ref, k_ref, vt_ref, zb_ref, gq_ref, gk_ref, y_ref, *, n_ctx_tiles, tile_off, ctx_len):
    t_all = k_ref.shape[1]

    tq = q_ref.shape[1]

    def attend(n_keys, stabilise):
        kk = k_ref[0, 0:n_keys, :]
        vt = vt_ref[0, :, 0:n_keys]
        for g in range(B_KV_HEADS):
            q_grp = jnp.concatenate(
                [q_ref[0, :, h * LANES:(h + 1) * LANES] for h in range(g * B_GROUP, (g + 1) * B_GROUP)],
                axis=0)
            s = _dot_nt(kk, q_grp)
            if stabilise:
                s = s - jnp.max(s, axis=0, keepdims=True)
            p = jnp.exp2(s)
            l = jnp.sum(p, axis=0, keepdims=True)
            o = _dot(vt, p.astype(BF16))[g * HEAD_DIM:(g + 1) * HEAD_DIM, :] / l
            for jj in range(B_GROUP // 2):
                pair = jnp.concatenate([o[:, (2 * jj) * tq:(2 * jj + 1) * tq],
                                        o[:, (2 * jj + 1) * tq:(2 * jj + 2) * tq]], axis=0).T
                j = g * (B_GROUP // 2) + jj
                sl = slice(j * LANES, (j + 1) * LANES)
                y_ref[0, :, sl] = (pair * zb_ref[0, :, sl].astype(F32)).astype(BF16)

    bound = (HEAD_DIM * ATTN_SCALE) * jnp.max(jnp.abs(gq_ref[...])) * jnp.max(jnp.abs(gk_ref[...]))
    safe = bound <= SAFE_LOGIT_BOUND
    i = pl.program_id(1) + tile_off
    if n_ctx_tiles > 0 and tile_off == 0:
        branches = ((i < n_ctx_tiles, ctx_len), (i >= n_ctx_tiles, t_all))
    else:
        branches = ((None, t_all),)
    for cond, n_keys in branches:
        for stabilise in (False, True):
            pred = safe if not stabilise else jnp.logical_not(safe)
            if cond is not None:
                pred = jnp.logical_and(pred, cond)
            pl.when(pred)(functools.partial(attend, n_keys, stabilise))


def _attn_call(q, k, vt, zb, g_q, g_k, n_ctx_tiles, tq, with_ctx):
    b, t, _ = q.shape
    nt = t // tq
    tile_off = 0 if with_ctx else n_ctx_tiles
    kern = functools.partial(_attn_kernel, n_ctx_tiles=n_ctx_tiles, tile_off=tile_off,
                             ctx_len=n_ctx_tiles * tq)
    return pl.pallas_call(
        kern,
        grid=(b, nt - tile_off),
        in_specs=[
            pl.BlockSpec((1, tq, B_HEADS * LANES), lambda bi, i: (bi, i + tile_off, 0)),
            pl.BlockSpec((1, t, B_KV_WIDTH), lambda bi, i: (bi, 0, 0)),
            pl.BlockSpec((1, B_KV_WIDTH, t), lambda bi, i: (bi, 0, 0)),
            pl.BlockSpec((1, tq, B_WIDTH), lambda bi, i: (bi, i + tile_off, 0)),
            pl.BlockSpec(g_q.shape, lambda bi, i: (0, 0)),
            pl.BlockSpec(g_k.shape, lambda bi, i: (0, 0)),
        ],
        out_specs=pl.BlockSpec((1, tq, B_WIDTH), lambda bi, i: (bi, i, 0)),
        out_shape=jax.ShapeDtypeStruct((b, t - tile_off * tq, B_WIDTH), BF16),
        compiler_params=pltpu.CompilerParams(
            dimension_semantics=("arbitrary", "arbitrary"), vmem_limit_bytes=VMEM_LIMIT_BYTES),
        name="attention",
    )(q, k, vt, zb, g_q, g_k)


def _mlstm_kernel(cqt_ref, ck_ref, cvt_ref, gr_ref, og_ref, br_ref, gh_ref,
                  y_ref, hf_ref, hb_ref, st_ref, m_ref, gp_ref, sel_ref, madd_ref, *, n_ctx_chunks):
    nc = cqt_ref.shape[1]
    L = M_CHUNK
    n_units = 2 * C_HEADS
    W = n_units * L
    n_rows = nc * GATE_ROWS
    n_spread = 2 * GATE_ROWS
    row = lax.broadcasted_iota(jnp.int32, (L, L), 0)
    col = lax.broadcasted_iota(jnp.int32, (L, L), 1)
    tri_row = tuple(jnp.where(mk, 1.0, 0.0).astype(BF16) for mk in (row <= col, row >= col))
    eye_bf = jnp.where(row == col, 1.0, 0.0).astype(BF16)
    @pl.when(pl.program_id(0) == 0)
    def _():
        srow = lax.broadcasted_iota(jnp.int32, (L, W), 0)
        scol = lax.broadcasted_iota(jnp.int32, (L, W), 1)
        sel_ref[...] = jnp.where((srow < 3 * n_spread)
                                 & (((srow // GATE_ROWS) % 2) == (scol // (C_HEADS * L)))
                                 & ((srow % GATE_ROWS) == ((scol // L) % C_HEADS)), 1.0, 0.0).astype(BF16)
        upward = scol < C_HEADS * L
        hidden = (upward & (srow > (scol % L))) | (jnp.logical_not(upward) & (srow < (scol % L)))
        madd_ref[...] = jnp.where(hidden, -jnp.inf, 0.0).astype(F32)

    st_ref[...] = jnp.zeros(st_ref.shape, F32)
    m_ref[...] = jnp.zeros(m_ref.shape, F32)

    lane_r = _lane_iota((n_rows, L))
    for d in range(2):
        li = (gr_ref[0, 2 * d] + br_ref[2 * d]) * LOG2E
        lf = _log_sigmoid(gr_ref[0, 2 * d + 1] + br_ref[2 * d + 1]) * LOG2E
        lf_hi = lf.astype(BF16)
        lf_r1 = lf - lf_hi.astype(F32)
        lf_mid = lf_r1.astype(BF16)
        lf_lo = (lf_r1 - lf_mid.astype(F32)).astype(BF16)
        b = _dot(lf_hi, tri_row[d]) + _dot(lf_mid, tri_row[d]) + _dot(lf_lo, tri_row[d])
        last = L - 1 if d == 0 else 0
        bend = jnp.broadcast_to(b[:, last:last + 1], (n_rows, L))
        g = li - b
        cm = g
        sh = 1
        while sh < L:
            if d == 0:
                cm = jnp.maximum(cm, jnp.where(lane_r >= sh, pltpu.roll(cm, sh, 1), -jnp.inf))
            else:
                cm = jnp.maximum(cm, jnp.where(lane_r < L - sh, pltpu.roll(cm, L - sh, 1), -jnp.inf))
            sh *= 2
        w_end = bend - b + li
        mx = jnp.max(w_end, axis=1, keepdims=True)
        gp_ref[d, 0] = g
        gp_ref[d, 1] = cm
        gp_ref[d, 2] = -(b + cm)
        gp_ref[d, 3] = jnp.exp2(w_end - mx)
        gp_ref[d, 4] = bend
        gp_ref[d, 5] = jnp.broadcast_to(mx, (n_rows, L))

    def unit_lanes(per_dir):
        return jnp.concatenate([per_dir[d][h:h + 1, :] for d in range(2) for h in range(C_HEADS)], axis=1)

    def chunks_of(i):
        return (i, jnp.where(i < n_ctx_chunks, n_ctx_chunks - 1 - i, nc + n_ctx_chunks - 1 - i))

    def gate_rows(chunks):
        return tuple(pl.ds(pl.multiple_of(c * GATE_ROWS, GATE_ROWS), GATE_ROWS) for c in chunks)

    def step(i, carry):
        chunks = chunks_of(i)
        rows = tuple(pl.ds(pl.multiple_of(c * L, L), L) for c in chunks)
        grow = gate_rows(chunks)
        g, cm, nbc, wk0, bend, mx = (tuple(gp_ref[d, k, grow[d], :] for d in range(2)) for k in range(6))
        m_old = (m_ref[0], m_ref[1])

        decay, wk = [], []
        for d in range(2):
            m_new = jnp.maximum(bend[d] + m_old[d], mx[d])
            decay.append(jnp.exp2(bend[d] + m_old[d] - m_new))
            wk.append(wk0[d] * jnp.exp2(mx[d] - m_new))
            m_ref[d] = m_new

        r16 = jnp.concatenate([g[0], g[1]], axis=0)
        hi = r16.astype(BF16)
        r1 = r16 - hi.astype(F32)
        mid = r1.astype(BF16)
        lo = (r1 - mid.astype(F32)).astype(BF16)
        pieces = jnp.concatenate([hi, mid, lo, jnp.zeros((L - 3 * n_spread, L), BF16)], axis=0)
        pieces_t = _dot_nt(eye_bf, pieces).astype(BF16)
        g_all = _dot(pieces_t, sel_ref[...])

        e0 = jnp.exp2((g_all - unit_lanes(cm)) + madd_ref[...])

        k_d = tuple(ck_ref[0, rows[d], :] for d in range(2))
        qt_d = tuple(cqt_ref[0, chunks[d]] for d in range(2))
        vt_d = tuple(cvt_ref[0, chunks[d]] for d in range(2))
        units = [(d, h) for d in range(2) for h in range(C_HEADS)]
        st_old, qk, bq = [], [], []
        for u, (d, h) in enumerate(units):
            sl = slice(h * LANES, (h + 1) * LANES)
            st = st_ref[u]
            st_old.append(st)
            r = _dot(jnp.concatenate([k_d[d][:, sl], st.astype(BF16)], axis=0), qt_d[d][sl, :])
            qk.append(r[0:L])
            bq.append(r[L:2 * L])
        s0 = (jnp.concatenate(qk, axis=1) * e0).astype(BF16)

        acc, upd = [], []
        for u, (d, h) in enumerate(units):
            sl = slice(h * LANES, (h + 1) * LANES)
            vt = vt_d[d][sl, :]
            vw = (vt.astype(F32) * wk[d][h:h + 1, :]).astype(BF16)
            acc.append(_dot(vt, s0[:, u * L:(u + 1) * L]))
            upd.append(_dot(vw, k_d[d][:, sl]))

        f, inter, thr = [], [], []
        for d in range(2):
            diff = cm[d] - m_old[d]
            below = jnp.minimum(diff, 0.0)
            f.append(jnp.exp2(below))
            inter.append(jnp.exp2(jnp.minimum(-diff, 0.0)))
            thr.append(jnp.exp2(nbc[d] + below))
        na = unit_lanes(f) * jnp.concatenate(acc, axis=1) + unit_lanes(inter) * jnp.concatenate(bq, axis=1)
        den = na[HEAD_DIM:HEAD_DIM + 1, :]
        h_all = na[0:HEAD_DIM, :] * (1.0 / jnp.maximum(jnp.abs(den), unit_lanes(thr)))
        for u, (d, h) in enumerate(units):
            dst = hf_ref if d == 0 else hb_ref
            dst[chunks[d], h * HEAD_DIM:(h + 1) * HEAD_DIM, :] = h_all[:, u * L:(u + 1) * L]

        for u, (d, h) in enumerate(units):
            st_ref[u] = decay[d][h:h + 1, :] * st_old[u] + upd[u]
        return carry

    lax.fori_loop(0, nc, step, 0)

    def combine(c, carry):
        rows = pl.ds(pl.multiple_of(c * L, L), L)
        hs = hf_ref[c] + hb_ref[c]
        normed = []
        for hd in range(C_HEADS):
            blk = hs[hd * HEAD_DIM:(hd + 1) * HEAD_DIM, :]
            ms = jnp.sum(blk * blk, axis=0, keepdims=True) * (1.0 / HEAD_DIM)
            normed.append(blk * lax.rsqrt(ms + EPS))
        y = (jnp.concatenate(normed, axis=0) * gh_ref[...]).T
        y_ref[0, rows, :] = (og_ref[0, rows, :].astype(F32) * y).astype(BF16)
        return carry

    lax.fori_loop(0, nc, combine, 0)


def _mlstm_call(cqt, ck, cvt, gr, og, lw, n_ctx_chunks):
    b, t, pad_w = ck.shape

    def per_b(a):
        nd = a.ndim
        return pl.BlockSpec((1,) + a.shape[1:], lambda bi: (bi,) + (0,) * (nd - 1))

    def whole(a):
        nd = a.ndim
        return pl.BlockSpec(a.shape, lambda bi: (0,) * nd)

    kern = functools.partial(_mlstm_kernel, n_ctx_chunks=n_ctx_chunks)
    return pl.pallas_call(
        kern,
        grid=(b,),
        in_specs=[per_b(cqt), per_b(ck), per_b(cvt), per_b(gr), per_b(og),
                  whole(lw["bg_r"]), whole(lw["g_head"])],
        out_specs=pl.BlockSpec((1, t, C_WIDTH), lambda bi: (bi, 0, 0)),
        out_shape=jax.ShapeDtypeStruct((b, t, C_WIDTH), BF16),
        scratch_shapes=[
            pltpu.VMEM((t // M_CHUNK, C_WIDTH, M_CHUNK), F32),
            pltpu.VMEM((t // M_CHUNK, C_WIDTH, M_CHUNK), F32),
            pltpu.VMEM((2 * C_HEADS, LANES, LANES), F32),
            pltpu.VMEM((2, GATE_ROWS, LANES), F32),
            pltpu.VMEM((2, 6, t // M_CHUNK * GATE_ROWS, M_CHUNK), F32),
            pltpu.VMEM((M_CHUNK, 2 * C_HEADS * M_CHUNK), BF16),
            pltpu.VMEM((M_CHUNK, 2 * C_HEADS * M_CHUNK), F32),
        ],
        compiler_params=pltpu.CompilerParams(
            dimension_semantics=("arbitrary",), vmem_limit_bytes=VMEM_LIMIT_BYTES),
        name="mlstm",
    )(cqt, ck, cvt, gr, og, lw["bg_r"], lw["g_head"])


def _rope_tables(ctx_len, seq):
    pos = np.arange(seq)
    freqs = ROPE_THETA ** (-np.arange(ROPE_FREQS, dtype=np.float32) / ROPE_FREQS)
    ang_r = (pos // GRID_W).astype(np.float32)[:, None] * freqs
    ang_c = (pos % GRID_W).astype(np.float32)[:, None] * freqs
    cos_h = np.concatenate([np.cos(ang_r), np.cos(ang_r), np.cos(ang_c), np.cos(ang_c)], axis=1)
    sin_h = np.concatenate([-np.sin(ang_r), np.sin(ang_r), -np.sin(ang_c), np.sin(ang_c)], axis=1)
    cos = np.concatenate([np.ones((ctx_len, HEAD_DIM), np.float32), cos_h.astype(np.float32)], axis=0)
    sin = np.concatenate([np.zeros((ctx_len, HEAD_DIM), np.float32), sin_h.astype(np.float32)], axis=0)
    return jnp.asarray(np.tile(cos, (1, 2))), jnp.asarray(np.tile(sin, (1, 2)))


def _layer_weights(l, n_chunks, g_norm, w_in, w_s, b_s, g_q, g_k, b_gates, g_head, w_out):
    d = w_in.shape[1]
    w_l = w_in[l]
    w_g = w_l[:, OFF_CG:]
    w_gr = jnp.zeros((4, GATE_ROWS, d), F32).at[:, :C_HEADS, :].set(w_g.T.reshape(4, C_HEADS, d))
    bg_r = jnp.zeros((4, GATE_ROWS), F32).at[:, :C_HEADS].set(b_gates[l].reshape(4, C_HEADS))
    return {
        "g_norm": g_norm[l].reshape(1, d),
        "w_main": w_l[:, :OFF_CG].astype(BF16),
        "w_gr": w_gr.reshape(4 * GATE_ROWS, d).astype(BF16),
        "w_s": w_s[l].astype(BF16),
        "b_s": jnp.repeat(b_s[l].T, HEAD_DIM, axis=1),
        "g_q": jnp.tile(g_q[l], B_HEADS).reshape(1, B_WIDTH),
        "g_k": jnp.tile(g_k[l], B_KV_HEADS).reshape(1, B_KV_WIDTH),
        "bg_r": jnp.broadcast_to(jnp.tile(bg_r, (1, n_chunks))[:, :, None],
                                 (4, n_chunks * GATE_ROWS, M_CHUNK)),
        "g_head": jnp.broadcast_to(g_head[l][:, None], (C_WIDTH, M_CHUNK)),
        "w_out": w_out[l].astype(BF16),
    }


def kernel(x, c, ctx, c_ctx, w_ada, b_ada, g_norm, w_in, w_s, b_s, g_q, g_k, b_gates, g_head, w_out,
           g_final):
    b, seq, d = x.shape
    ctx_len = ctx.shape[1]
    depth = w_ada.shape[0]
    tm = 256 if (ctx_len % 256 == 0 and seq % 256 == 0) else CHUNK
    n_ctx_tiles = ctx_len // tm
    n_ctx_chunks = ctx_len // M_CHUNK

    n_mod = b + 1
    rows = -(-n_mod // 8) * 8
    cc = jnp.zeros((rows, d), F32).at[:b].set(c).at[b].set(c_ctx)
    mods = _modulation(cc, w_ada, b_ada)[:, :n_mod]
    mods = mods.reshape(depth, n_mod, 3, 1, d)

    cos, sin = _rope_tables(ctx_len, seq)
    seg = np.kron(np.eye(B_HEADS, dtype=np.float32), np.full((HEAD_DIM, HEAD_DIM), 1.0 / HEAD_DIM, np.float32))
    consts = {"cos": cos, "sin": sin, "pm": jnp.asarray(seg, BF16)}
    gf = g_final.reshape(1, d)

    src = (ctx, x)
    mix = None
    for l in range(depth + 1):
        lw = mod = None
        if l < depth:
            lw = _layer_weights(l, (ctx_len + seq) // M_CHUNK, g_norm, w_in, w_s, b_s, g_q, g_k, b_gates,
                                g_head, w_out)
            sh, sc, gt = mods[l, :, 0], mods[l, :, 1], mods[l, :, 2]
            mod = (sc, sh)
        outs = _layer_call(src, mix, lw, mod, consts, gf, n_ctx_tiles, tm)
        if mix is not None:
            src = (outs[0],)
            outs = outs[1:]
        if l == depth:
            return src[0]
        (ya, q, k, vt, zb, cqt, ck, cvt, og, gr) = outs
        yb = _attn_call(q, k, vt, zb, lw["g_q"], lw["g_k"], n_ctx_tiles, tm, with_ctx=l < depth - 1)
        yc = _mlstm_call(cqt, ck, cvt, gr, og, lw, n_ctx_chunks)
        mix = (gt, ya, yb, yc, lw["w_out"])
```

```python
import functools

import jax
import jax.numpy as jnp
import numpy as np
from jax import lax
from jax.experimental import pallas as pl
from jax.experimental.pallas import tpu as pltpu

HEAD_DIM = 64
LANES = 128
GRID_W = 64
EPS = 1e-6
A_GROUPS = 4
A_WIDTH = A_GROUPS * HEAD_DIM
CHUNK = 128
B_HEADS = 8
B_KV_HEADS = 2
B_GROUP = B_HEADS // B_KV_HEADS
B_WIDTH = B_HEADS * HEAD_DIM
B_KV_WIDTH = B_KV_HEADS * HEAD_DIM
ATTN_SCALE = HEAD_DIM ** -0.5
ROPE_THETA = 10000.0
ROPE_FREQS = HEAD_DIM // 4
C_HEADS = 4
C_WIDTH = C_HEADS * HEAD_DIM
M_CHUNK = 128
N_GATES = 4 * C_HEADS
GATE_ROWS = 8

OFF_AU = 0
OFF_AV = OFF_AU + A_WIDTH
OFF_AZ = OFF_AV + A_WIDTH
OFF_BQ = OFF_AZ + A_WIDTH
OFF_BK = OFF_BQ + B_WIDTH
OFF_BV = OFF_BK + B_KV_WIDTH
OFF_BZ = OFF_BV + B_KV_WIDTH
OFF_CQ = OFF_BZ + B_WIDTH
OFF_CK = OFF_CQ + C_WIDTH
OFF_CV = OFF_CK + C_WIDTH
OFF_CO = OFF_CV + C_WIDTH
OFF_CZ = OFF_CO + C_WIDTH
OFF_CG = OFF_CZ + C_WIDTH
D_IN = OFF_CG + N_GATES

VMEM_LIMIT_BYTES = 56 * 1024 * 1024

F32 = jnp.float32
BF16 = jnp.bfloat16
GATE_DTYPE = BF16

LOG2E = float(np.log2(np.e))
QK_PRESCALE = ATTN_SCALE * LOG2E
SAFE_LOGIT_BOUND = 40.0


def _dot(a, b):
    return jnp.dot(a, b, preferred_element_type=F32)


def _dot_nt(a, b):
    return lax.dot_general(a, b, (((1,), (1,)), ((), ())), preferred_element_type=F32)


def _dot_tn(a, b):
    return lax.dot_general(a, b, (((0,), (0,)), ((), ())), preferred_element_type=F32)


def _silu(x):
    return x * jax.nn.sigmoid(x)


def _gelu_tanh(x):
    c = np.float32(np.sqrt(2.0 / np.pi))
    return 0.5 * x * (1.0 + jnp.tanh(c * (x + np.float32(0.044715) * (x * x * x))))


def _log_sigmoid(x):
    return jnp.minimum(x, 0.0) - jnp.log1p(jnp.exp(-jnp.abs(x)))


def _seg_mean(x, pm_ref):
    w = x.shape[-1]
    pm = pm_ref[0:w, 0:w]
    return _dot(x.astype(BF16), pm)


def _lane_iota(shape):
    return lax.broadcasted_iota(jnp.int32, shape, len(shape) - 1)


def _mod_kernel(cc_ref, w_ref, b_ref, o_ref):
    s = _silu(cc_ref[...])
    o_ref[0] = _dot(s.astype(BF16), w_ref[0].astype(BF16)) + b_ref[0]


def _modulation(cc, w_ada, b_ada):
    depth, d, d3 = w_ada.shape
    rows = cc.shape[0]
    tn = d
    return pl.pallas_call(
        _mod_kernel,
        grid=(depth, d3 // tn),
        in_specs=[
            pl.BlockSpec((rows, d), lambda l, n: (0, 0)),
            pl.BlockSpec((1, d, tn), lambda l, n: (l, 0, n)),
            pl.BlockSpec((1, 1, tn), lambda l, n: (l, 0, n)),
        ],
        out_specs=pl.BlockSpec((1, rows, tn), lambda l, n: (l, 0, n)),
        out_shape=jax.ShapeDtypeStruct((depth, rows, d3), F32),
        compiler_params=pltpu.CompilerParams(
            dimension_semantics=("arbitrary", "arbitrary"), vmem_limit_bytes=VMEM_LIMIT_BYTES),
        name="modulation",
    )(cc, w_ada, b_ada.reshape(depth, 1, d3))


def _proj_body(x, sc_ref, sh_ref, gn_ref, w_ref, wgr_ref, pm_ref, ws_ref, bs_ref,
               gq_ref, gk_ref, cos_ref, sin_ref,
               ya_ref, q_ref, k_ref, vt_ref, zb_ref, cqt_ref, ck_ref, cvt_ref, og_ref, gr_ref):
    tm = x.shape[0]
    ms = jnp.mean(x * x, axis=-1, keepdims=True)
    xn = (x * lax.rsqrt(ms + EPS)) * (gn_ref[...] * (1.0 + sc_ref[0])) + sh_ref[0]
    xb = xn.astype(BF16)

    mixers = ((OFF_AU, OFF_BQ), (OFF_BQ, OFF_CQ), (OFF_CQ, OFF_CG))
    wide = {}

    def proj(off, width):
        for lo, hi in mixers:
            if lo <= off < hi:
                if lo not in wide:
                    wide[lo] = _dot(xb, w_ref[:, lo:hi])
                return wide[lo][:, off - lo:off - lo + width]
        raise ValueError(off)

    lane = _lane_iota((tm, LANES))
    lo_half = lane < HEAD_DIM

    gu = _gelu_tanh(proj(OFF_AU, A_WIDTH))
    gv = _gelu_tanh(proj(OFF_AV, A_WIDTH))
    mu = _seg_mean(gv, pm_ref)
    dv = gv - mu
    var = _seg_mean(dv * dv, pm_ref)
    vn = (dv * lax.rsqrt(var + EPS)).astype(BF16)
    sz = _silu(proj(OFF_AZ, A_WIDTH))
    lo_half_chunk = _lane_iota((CHUNK, LANES)) < HEAD_DIM
    for c in range(tm // CHUNK):
        r0, r1 = c * CHUNK, (c + 1) * CHUNK
        for j in range(A_WIDTH // LANES):
            c0, c1 = j * LANES, (j + 1) * LANES
            vs = vn[r0:r1, c0:c1]
            stacked = jnp.concatenate([jnp.where(lo_half_chunk, vs, 0.0), jnp.where(lo_half_chunk, 0.0, vs)],
                                      axis=0).astype(BF16)
            sv = _dot(ws_ref[j], stacked) + bs_ref[:, c0:c1]
            ya_ref[0, r0:r1, c0:c1] = (gu[r0:r1, c0:c1] * sv * sz[r0:r1, c0:c1]).astype(BF16)

    cos = cos_ref[...]
    sin = sin_ref[...]
    bit4 = (lane & ROPE_FREQS) != 0

    def rope(t):
        partner = jnp.where(bit4, pltpu.roll(t, ROPE_FREQS, 1), pltpu.roll(t, LANES - ROPE_FREQS, 1))
        return t * cos + partner * sin

    half_w = B_WIDTH // 2
    q_halves = []
    for off in (0, half_w):
        qh = proj(OFF_BQ + off, half_w)
        q_halves.append(qh * lax.rsqrt(_seg_mean(qh * qh, pm_ref) + EPS) * gq_ref[:, off:off + half_w])
    for j in range(B_WIDTH // LANES):
        jj = j % (half_w // LANES)
        qs = rope(q_halves[j // (half_w // LANES)][:, jj * LANES:(jj + 1) * LANES]) * QK_PRESCALE
        sw = pltpu.roll(qs, HEAD_DIM, 1)
        if (2 * j) // B_GROUP == 0:
            h_even = jnp.where(lo_half, qs, 0.0)
            h_odd = jnp.where(lo_half, sw, 0.0)
        else:
            h_even = jnp.where(lo_half, 0.0, sw)
            h_odd = jnp.where(lo_half, 0.0, qs)
        q_ref[0, :, (2 * j) * LANES:(2 * j + 1) * LANES] = h_even.astype(BF16)
        q_ref[0, :, (2 * j + 1) * LANES:(2 * j + 2) * LANES] = h_odd.astype(BF16)

    k = proj(OFF_BK, B_KV_WIDTH)
    kn = k * lax.rsqrt(_seg_mean(k * k, pm_ref) + EPS) * gk_ref[...]
    k_ref[0] = rope(kn).astype(BF16)
    vt_ref[0] = proj(OFF_BV, B_KV_WIDTH).T.astype(BF16)
    zb_ref[0] = _silu(proj(OFF_BZ, B_WIDTH)).astype(zb_ref.dtype)

    cq = proj(OFF_CQ, C_WIDTH)
    ck = proj(OFF_CK, C_WIDTH) * (HEAD_DIM ** -0.5)
    cv = proj(OFF_CV, C_WIDTH)
    for j in range(C_WIDTH // LANES):
        sl = slice(j * LANES, (j + 1) * LANES)
        for par in range(2):
            hd = 2 * j + par
            h_sl = slice(hd * LANES, (hd + 1) * LANES)

            def head_tile(src, pad):
                s = src[:, sl] if par == 0 else pltpu.roll(src[:, sl], HEAD_DIM, 1)
                return jnp.where(lo_half, s, pad)

            ck_ref[0, :, h_sl] = head_tile(ck, 0.0).astype(BF16)
            qt = head_tile(cq, 0.0)
            vt = head_tile(cv, 1.0)
            for c in range(tm // CHUNK):
                cqt_ref[0, c, h_sl, :] = qt[c * CHUNK:(c + 1) * CHUNK, :].T.astype(BF16)
                cvt_ref[0, c, h_sl, :] = vt[c * CHUNK:(c + 1) * CHUNK, :].T.astype(BF16)
    og = jax.nn.sigmoid(proj(OFF_CO, C_WIDTH)) * _silu(proj(OFF_CZ, C_WIDTH))
    og_ref[0] = og.astype(og_ref.dtype)
    gr = _dot_nt(wgr_ref[...], xb)
    for c in range(tm // CHUNK):
        for ty in range(4):
            gr_ref[0, ty, c * GATE_ROWS:(c + 1) * GATE_ROWS, :] = (
                gr[ty * GATE_ROWS:(ty + 1) * GATE_ROWS, c * CHUNK:(c + 1) * CHUNK])


N_MIX_IN = 5
N_PROJ_IN = 12


def _layer_kernel(*refs, n_src, mix, proj, final, n_ctx_tiles):
    refs = list(refs)
    src = [refs.pop(0) for _ in range(n_src)]
    if n_src == 1:
        x = src[0][0]
    else:
        x = jnp.where(pl.program_id(1) < n_ctx_tiles, src[0][0], src[1][0])
    if mix:
        gt_ref, ya_ref, yb_ref, yc_ref, w_ref = (refs.pop(0) for _ in range(N_MIX_IN))
        acc = _dot(ya_ref[0], w_ref[0:A_WIDTH, :])
        acc += _dot(yb_ref[0], w_ref[A_WIDTH:A_WIDTH + B_WIDTH, :])
        acc += _dot(yc_ref[0], w_ref[A_WIDTH + B_WIDTH:, :])
        x = x + gt_ref[0] * acc
    if final:
        gf_ref = refs.pop(0)
        ms = jnp.mean(x * x, axis=-1, keepdims=True)
        x = (x * lax.rsqrt(ms + EPS)) * gf_ref[...]
    proj_in = [refs.pop(0) for _ in range(N_PROJ_IN)] if proj else []
    if mix:
        refs.pop(0)[0] = x
    if proj:
        _proj_body(x, *proj_in, *refs)


def _layer_call(src, mix, lw, mod, consts, g_final, n_ctx_tiles, tm):
    n_src = len(src)
    final = lw is None
    b, d = src[0].shape[0], src[0].shape[2]
    t = src[0].shape[1] if n_src == 1 else src[0].shape[1] + src[1].shape[1]
    off = n_ctx_tiles if final else 0
    nt = t // tm - off

    def tile3(w):
        return pl.BlockSpec((1, tm, w), lambda bi, i: (bi, i + off, 0))

    def whole(a):
        nd = a.ndim
        return pl.BlockSpec(a.shape, lambda bi, i: (0,) * nd)

    def mod_spec(a):
        n_mod = a.shape[0]
        return pl.BlockSpec((1, 1, d), lambda bi, i: (jnp.where(i + off < n_ctx_tiles, n_mod - 1, bi), 0, 0))

    if n_src == 1:
        ins, in_specs = [src[0]], [tile3(d)]
    else:
        ins = list(src)
        in_specs = [pl.BlockSpec((1, tm, d), lambda bi, i: (bi, jnp.minimum(i, n_ctx_tiles - 1), 0)),
                    pl.BlockSpec((1, tm, d), lambda bi, i: (bi, jnp.maximum(i - n_ctx_tiles, 0), 0))]
    out_shape, out_specs, aliases = [], [], {}
    if mix is not None:
        gt, ya, yb, yc, w_out = mix
        yb_spec = tile3(B_WIDTH) if yb.shape[1] == t else pl.BlockSpec((1, tm, B_WIDTH), lambda bi, i: (bi, i, 0))
        ins += [gt, ya, yb, yc, w_out]
        in_specs += [mod_spec(gt), tile3(A_WIDTH), yb_spec, tile3(C_WIDTH), whole(w_out)]
        if final:
            ins.append(g_final)
            in_specs.append(whole(g_final))
            out_shape.append(jax.ShapeDtypeStruct((b, t - off * tm, d), F32))
            out_specs.append(pl.BlockSpec((1, tm, d), lambda bi, i: (bi, i, 0)))
        else:
            out_shape.append(jax.ShapeDtypeStruct((b, t, d), F32))
            out_specs.append(tile3(d))
            if n_src == 1:
                aliases = {0: 0}
    if not final:
        sc, sh = mod
        ins += [sc, sh, lw["g_norm"], lw["w_main"], lw["w_gr"], consts["pm"], lw["w_s"],
                lw["b_s"], lw["g_q"], lw["g_k"], consts["cos"], consts["sin"]]
        in_specs += [mod_spec(sc), mod_spec(sh),
                     whole(lw["g_norm"]), whole(lw["w_main"]), whole(lw["w_gr"]),
                     whole(consts["pm"]), whole(lw["w_s"]), whole(lw["b_s"]), whole(lw["g_q"]),
                     whole(lw["g_k"]),
                     pl.BlockSpec((tm, LANES), lambda bi, i: (i, 0)),
                     pl.BlockSpec((tm, LANES), lambda bi, i: (i, 0))]
        pad_w = C_HEADS * LANES
        out_shape += [
            jax.ShapeDtypeStruct((b, t, A_WIDTH), BF16),
            jax.ShapeDtypeStruct((b, t, B_HEADS * LANES), BF16),
            jax.ShapeDtypeStruct((b, t, B_KV_WIDTH), BF16),
            jax.ShapeDtypeStruct((b, B_KV_WIDTH, t), BF16),
            jax.ShapeDtypeStruct((b, t, B_WIDTH), GATE_DTYPE),
            jax.ShapeDtypeStruct((b, t // CHUNK, pad_w, CHUNK), BF16),
            jax.ShapeDtypeStruct((b, t, pad_w), BF16),
            jax.ShapeDtypeStruct((b, t // CHUNK, pad_w, CHUNK), BF16),
            jax.ShapeDtypeStruct((b, t, C_WIDTH), GATE_DTYPE),
            jax.ShapeDtypeStruct((b, 4, t // CHUNK * GATE_ROWS, CHUNK), F32),
        ]
        out_specs += [tile3(A_WIDTH), tile3(B_HEADS * LANES), tile3(B_KV_WIDTH),
                      pl.BlockSpec((1, B_KV_WIDTH, tm), lambda bi, i: (bi, 0, i)),
                      tile3(B_WIDTH),
                      pl.BlockSpec((1, tm // CHUNK, pad_w, CHUNK), lambda bi, i: (bi, i, 0, 0)),
                      tile3(pad_w),
                      pl.BlockSpec((1, tm // CHUNK, pad_w, CHUNK), lambda bi, i: (bi, i, 0, 0)),
                      tile3(C_WIDTH),
                      pl.BlockSpec((1, 4, tm // CHUNK * GATE_ROWS, CHUNK), lambda bi, i: (bi, 0, i, 0))]
    kern = functools.partial(_layer_kernel, n_src=n_src, mix=mix is not None, proj=not final,
                             final=final, n_ctx_tiles=n_ctx_tiles)
    return pl.pallas_call(
        kern,
        grid=(b, nt),
        in_specs=in_specs,
        out_specs=out_specs,
        out_shape=out_shape,
        input_output_aliases=aliases,
        compiler_params=pltpu.CompilerParams(
            dimension_semantics=("arbitrary", "arbitrary"), vmem_limit_bytes=VMEM_LIMIT_BYTES),
        name="layer_tile",
    )(*ins)


def _attn_kernel(q_ref, k_ref, vt_ref, zb_ref, gq_ref, gk_ref, y_ref, *, n_ctx_tiles, tile_off, ctx_len):
    t_all = k_ref.shape[1]

    tq = q_ref.shape[1]

    def attend(n_keys, stabilise):
        kk = k_ref[0, 0:n_keys, :]
        vt = vt_ref[0, :, 0:n_keys]
        for g in range(B_KV_HEADS):
            q_grp = jnp.concatenate(
                [q_ref[0, :, h * LANES:(h + 1) * LANES] for h in range(g * B_GROUP, (g + 1) * B_GROUP)],
                axis=0)
            s = _dot_nt(kk, q_grp)
            if stabilise:
                s = s - jnp.max(s, axis=0, keepdims=True)
            p = jnp.exp2(s)
            l = jnp.sum(p, axis=0, keepdims=True)
            o = _dot(vt, p.astype(BF16))[g * HEAD_DIM:(g + 1) * HEAD_DIM, :] / l
            for jj in range(B_GROUP // 2):
                pair = jnp.concatenate([o[:, (2 * jj) * tq:(2 * jj + 1) * tq],
                                        o[:, (2 * jj + 1) * tq:(2 * jj + 2) * tq]], axis=0).T
                j = g * (B_GROUP // 2) + jj
                sl = slice(j * LANES, (j + 1) * LANES)
                y_ref[0, :, sl] = (pair * zb_ref[0, :, sl].astype(F32)).astype(BF16)

    bound = (HEAD_DIM * ATTN_SCALE) * jnp.max(jnp.abs(gq_ref[...])) * jnp.max(jnp.abs(gk_ref[...]))
    safe = bound <= SAFE_LOGIT_BOUND
    i = pl.program_id(1) + tile_off
    if n_ctx_tiles > 0 and tile_off == 0:
        branches = ((i < n_ctx_tiles, ctx_len), (i >= n_ctx_tiles, t_all))
    else:
        branches = ((None, t_all),)
    for cond, n_keys in branches:
        for stabilise in (False, True):
            pred = safe if not stabilise else jnp.logical_not(safe)
            if cond is not None:
                pred = jnp.logical_and(pred, cond)
            pl.when(pred)(functools.partial(attend, n_keys, stabilise))


def _attn_call(q, k, vt, zb, g_q, g_k, n_ctx_tiles, tq, with_ctx):
    b, t, _ = q.shape
    nt = t // tq
    tile_off = 0 if with_ctx else n_ctx_tiles
    kern = functools.partial(_attn_kernel, n_ctx_tiles=n_ctx_tiles, tile_off=tile_off,
                             ctx_len=n_ctx_tiles * tq)
    return pl.pallas_call(
        kern,
        grid=(b, nt - tile_off),
        in_specs=[
            pl.BlockSpec((1, tq, B_HEADS * LANES), lambda bi, i: (bi, i + tile_off, 0)),
            pl.BlockSpec((1, t, B_KV_WIDTH), lambda bi, i: (bi, 0, 0)),
            pl.BlockSpec((1, B_KV_WIDTH, t), lambda bi, i: (bi, 0, 0)),
            pl.BlockSpec((1, tq, B_WIDTH), lambda bi, i: (bi, i + tile_off, 0)),
            pl.BlockSpec(g_q.shape, lambda bi, i: (0, 0)),
            pl.BlockSpec(g_k.shape, lambda bi, i: (0, 0)),
        ],
        out_specs=pl.BlockSpec((1, tq, B_WIDTH), lambda bi, i: (bi, i, 0)),
        out_shape=jax.ShapeDtypeStruct((b, t - tile_off * tq, B_WIDTH), BF16),
        compiler_params=pltpu.CompilerParams(
            dimension_semantics=("arbitrary", "arbitrary"), vmem_limit_bytes=VMEM_LIMIT_BYTES),
        name="attention",
    )(q, k, vt, zb, g_q, g_k)


def _mlstm_kernel(cqt_ref, ck_ref, cvt_ref, gr_ref, og_ref, br_ref, gh_ref,
                  y_ref, hf_ref, hb_ref, st_ref, m_ref, gp_ref, sel_ref, madd_ref, *, n_ctx_chunks):
    nc = cqt_ref.shape[1]
    L = M_CHUNK
    n_units = 2 * C_HEADS
    W = n_units * L
    n_rows = nc * GATE_ROWS
    n_spread = 2 * GATE_ROWS
    row = lax.broadcasted_iota(jnp.int32, (L, L), 0)
    col = lax.broadcasted_iota(jnp.int32, (L, L), 1)
    tri_row = tuple(jnp.where(mk, 1.0, 0.0).astype(BF16) for mk in (row <= col, row >= col))
    eye_bf = jnp.where(row == col, 1.0, 0.0).astype(BF16)
    @pl.when(pl.program_id(0) == 0)
    def _():
        srow = lax.broadcasted_iota(jnp.int32, (L, W), 0)
        scol = lax.broadcasted_iota(jnp.int32, (L, W), 1)
        sel_ref[...] = jnp.where((srow < 3 * n_spread)
                                 & (((srow // GATE_ROWS) % 2) == (scol // (C_HEADS * L)))
                                 & ((srow % GATE_ROWS) == ((scol // L) % C_HEADS)), 1.0, 0.0).astype(BF16)
        upward = scol < C_HEADS * L
        hidden = (upward & (srow > (scol % L))) | (jnp.logical_not(upward) & (srow < (scol % L)))
        madd_ref[...] = jnp.where(hidden, -jnp.inf, 0.0).astype(F32)

    st_ref[...] = jnp.zeros(st_ref.shape, F32)
    m_ref[...] = jnp.zeros(m_ref.shape, F32)

    lane_r = _lane_iota((n_rows, L))
    li, b, bend = [], [], []
    for d in range(2):
        li.append((gr_ref[0, 2 * d] + br_ref[2 * d]) * LOG2E)
        lf = _log_sigmoid(gr_ref[0, 2 * d + 1] + br_ref[2 * d + 1]) * LOG2E
        lf_hi = lf.astype(BF16)
        lf_r1 = lf - lf_hi.astype(F32)
        lf_mid = lf_r1.astype(BF16)
        lf_lo = (lf_r1 - lf_mid.astype(F32)).astype(BF16)
        b.append(_dot(lf_hi, tri_row[d]) + _dot(lf_mid, tri_row[d]) + _dot(lf_lo, tri_row[d]))
        last = L - 1 if d == 0 else 0
        bend.append(jnp.broadcast_to(b[d][:, last:last + 1], (n_rows, L)))
    g = [li[d] - b[d] for d in range(2)]
    cm = list(g)
    sh = 1
    while sh < L:
        cm[0] = jnp.maximum(cm[0], jnp.where(lane_r >= sh, pltpu.roll(cm[0], sh, 1), -jnp.inf))
        cm[1] = jnp.maximum(cm[1], jnp.where(lane_r < L - sh, pltpu.roll(cm[1], L - sh, 1), -jnp.inf))
        sh *= 2
    for d in range(2):
        w_end = bend[d] - b[d] + li[d]
        mx = jnp.max(w_end, axis=1, keepdims=True)
        gp_ref[d, 0] = g[d]
        gp_ref[d, 1] = cm[d]
        gp_ref[d, 2] = -(b[d] + cm[d])
        gp_ref[d, 3] = jnp.exp2(w_end - mx)
        gp_ref[d, 4] = bend[d]
        gp_ref[d, 5] = jnp.broadcast_to(mx, (n_rows, L))

    def unit_lanes(per_dir):
        return jnp.concatenate([per_dir[d][h:h + 1, :] for d in range(2) for h in range(C_HEADS)], axis=1)

    def chunks_of(i):
        return (i, jnp.where(i < n_ctx_chunks, n_ctx_chunks - 1 - i, nc + n_ctx_chunks - 1 - i))

    def gate_rows(chunks):
        return tuple(pl.ds(pl.multiple_of(c * GATE_ROWS, GATE_ROWS), GATE_ROWS) for c in chunks)

    def step(i, carry):
        chunks = chunks_of(i)
        rows = tuple(pl.ds(pl.multiple_of(c * L, L), L) for c in chunks)
        grow = gate_rows(chunks)
        g, cm, nbc, wk0, bend, mx = (tuple(gp_ref[d, k, grow[d], :] for d in range(2)) for k in range(6))
        m_old = (m_ref[0], m_ref[1])

        decay, wk = [], []
        for d in range(2):
            m_new = jnp.maximum(bend[d] + m_old[d], mx[d])
            decay.append(jnp.exp2(bend[d] + m_old[d] - m_new))
            wk.append(wk0[d] * jnp.exp2(mx[d] - m_new))
            m_ref[d] = m_new

        r16 = jnp.concatenate([g[0], g[1]], axis=0)
        hi = r16.astype(BF16)
        r1 = r16 - hi.astype(F32)
        mid = r1.astype(BF16)
        lo = (r1 - mid.astype(F32)).astype(BF16)
        pieces = jnp.concatenate([hi, mid, lo, jnp.zeros((L - 3 * n_spread, L), BF16)], axis=0)
        pieces_t = _dot_nt(eye_bf, pieces).astype(BF16)
        g_all = _dot(pieces_t, sel_ref[...])

        e0 = jnp.exp2((g_all - unit_lanes(cm)) + madd_ref[...])

        k_d = tuple(ck_ref[0, rows[d], :] for d in range(2))
        qt_d = tuple(cqt_ref[0, chunks[d]] for d in range(2))
        vt_d = tuple(cvt_ref[0, chunks[d]] for d in range(2))
        units = [(d, h) for d in range(2) for h in range(C_HEADS)]
        st_old, qk, bq = [], [], []
        for u, (d, h) in enumerate(units):
            sl = slice(h * LANES, (h + 1) * LANES)
            st = st_ref[u]
            st_old.append(st)
            r = _dot(jnp.concatenate([k_d[d][:, sl], st.astype(BF16)], axis=0), qt_d[d][sl, :])
            qk.append(r[0:L])
            bq.append(r[L:2 * L])
        s0 = (jnp.concatenate(qk, axis=1) * e0).astype(BF16)

        acc, upd = [], []
        for u, (d, h) in enumerate(units):
            sl = slice(h * LANES, (h + 1) * LANES)
            vt = vt_d[d][sl, :]
            vw = (vt.astype(F32) * wk[d][h:h + 1, :]).astype(BF16)
            acc.append(_dot(vt, s0[:, u * L:(u + 1) * L]))
            upd.append(_dot(vw, k_d[d][:, sl]))

        f, inter, thr = [], [], []
        for d in range(2):
            diff = cm[d] - m_old[d]
            below = jnp.minimum(diff, 0.0)
            f.append(jnp.exp2(below))
            inter.append(jnp.exp2(jnp.minimum(-diff, 0.0)))
            thr.append(jnp.exp2(nbc[d] + below))
        na = unit_lanes(f) * jnp.concatenate(acc, axis=1) + unit_lanes(inter) * jnp.concatenate(bq, axis=1)
        den = na[HEAD_DIM:HEAD_DIM + 1, :]
        h_all = na[0:HEAD_DIM, :] * (1.0 / jnp.maximum(jnp.abs(den), unit_lanes(thr)))
        for u, (d, h) in enumerate(units):
            dst = hf_ref if d == 0 else hb_ref
            dst[chunks[d], h * HEAD_DIM:(h + 1) * HEAD_DIM, :] = h_all[:, u * L:(u + 1) * L]

        for u, (d, h) in enumerate(units):
            st_ref[u] = decay[d][h:h + 1, :] * st_old[u] + upd[u]
        return carry

    lax.fori_loop(0, nc, step, 0)

    def combine(c, carry):
        rows = pl.ds(pl.multiple_of(c * L, L), L)
        hs = hf_ref[c] + hb_ref[c]
        normed = []
        for hd in range(C_HEADS):
            blk = hs[hd * HEAD_DIM:(hd + 1) * HEAD_DIM, :]
            ms = jnp.sum(blk * blk, axis=0, keepdims=True) * (1.0 / HEAD_DIM)
            normed.append(blk * lax.rsqrt(ms + EPS))
        y = (jnp.concatenate(normed, axis=0) * gh_ref[...]).T
        y_ref[0, rows, :] = (og_ref[0, rows, :].astype(F32) * y).astype(BF16)
        return carry

    lax.fori_loop(0, nc, combine, 0)


def _mlstm_call(cqt, ck, cvt, gr, og, lw, n_ctx_chunks):
    b, t, pad_w = ck.shape

    def per_b(a):
        nd = a.ndim
        return pl.BlockSpec((1,) + a.shape[1:], lambda bi: (bi,) + (0,) * (nd - 1))

    def whole(a):
        nd = a.ndim
        return pl.BlockSpec(a.shape, lambda bi: (0,) * nd)

    kern = functools.partial(_mlstm_kernel, n_ctx_chunks=n_ctx_chunks)
    return pl.pallas_call(
        kern,
        grid=(b,),
        in_specs=[per_b(cqt), per_b(ck), per_b(cvt), per_b(gr), per_b(og),
                  whole(lw["bg_r"]), whole(lw["g_head"])],
        out_specs=pl.BlockSpec((1, t, C_WIDTH), lambda bi: (bi, 0, 0)),
        out_shape=jax.ShapeDtypeStruct((b, t, C_WIDTH), BF16),
        scratch_shapes=[
            pltpu.VMEM((t // M_CHUNK, C_WIDTH, M_CHUNK), F32),
            pltpu.VMEM((t // M_CHUNK, C_WIDTH, M_CHUNK), F32),
            pltpu.VMEM((2 * C_HEADS, LANES, LANES), F32),
            pltpu.VMEM((2, GATE_ROWS, LANES), F32),
            pltpu.VMEM((2, 6, t // M_CHUNK * GATE_ROWS, M_CHUNK), F32),
            pltpu.VMEM((M_CHUNK, 2 * C_HEADS * M_CHUNK), BF16),
            pltpu.VMEM((M_CHUNK, 2 * C_HEADS * M_CHUNK), F32),
        ],
        compiler_params=pltpu.CompilerParams(
            dimension_semantics=("arbitrary",), vmem_limit_bytes=VMEM_LIMIT_BYTES),
        name="mlstm",
    )(cqt, ck, cvt, gr, og, lw["bg_r"], lw["g_head"])


def _rope_tables(ctx_len, seq):
    pos = np.arange(seq)
    freqs = ROPE_THETA ** (-np.arange(ROPE_FREQS, dtype=np.float32) / ROPE_FREQS)
    ang_r = (pos // GRID_W).astype(np.float32)[:, None] * freqs
    ang_c = (pos % GRID_W).astype(np.float32)[:, None] * freqs
    cos_h = np.concatenate([np.cos(ang_r), np.cos(ang_r), np.cos(ang_c), np.cos(ang_c)], axis=1)
    sin_h = np.concatenate([-np.sin(ang_r), np.sin(ang_r), -np.sin(ang_c), np.sin(ang_c)], axis=1)
    cos = np.concatenate([np.ones((ctx_len, HEAD_DIM), np.float32), cos_h.astype(np.float32)], axis=0)
    sin = np.concatenate([np.zeros((ctx_len, HEAD_DIM), np.float32), sin_h.astype(np.float32)], axis=0)
    return jnp.asarray(np.tile(cos, (1, 2))), jnp.asarray(np.tile(sin, (1, 2)))


def _layer_weights(l, n_chunks, g_norm, w_in, w_s, b_s, g_q, g_k, b_gates, g_head, w_out):
    d = w_in.shape[1]
    w_l = w_in[l]
    w_g = w_l[:, OFF_CG:]
    w_gr = jnp.zeros((4, GATE_ROWS, d), F32).at[:, :C_HEADS, :].set(w_g.T.reshape(4, C_HEADS, d))
    bg_r = jnp.zeros((4, GATE_ROWS), F32).at[:, :C_HEADS].set(b_gates[l].reshape(4, C_HEADS))
    return {
        "g_norm": g_norm[l].reshape(1, d),
        "w_main": w_l[:, :OFF_CG].astype(BF16),
        "w_gr": w_gr.reshape(4 * GATE_ROWS, d).astype(BF16),
        "w_s": jnp.concatenate([w_s[l, 0::2], w_s[l, 1::2]], axis=2).astype(BF16),
        "b_s": jnp.repeat(b_s[l].T, HEAD_DIM, axis=1),
        "g_q": jnp.tile(g_q[l], B_HEADS).reshape(1, B_WIDTH),
        "g_k": jnp.tile(g_k[l], B_KV_HEADS).reshape(1, B_KV_WIDTH),
        "bg_r": jnp.broadcast_to(jnp.tile(bg_r, (1, n_chunks))[:, :, None],
                                 (4, n_chunks * GATE_ROWS, M_CHUNK)),
        "g_head": jnp.broadcast_to(g_head[l][:, None], (C_WIDTH, M_CHUNK)),
        "w_out": w_out[l].astype(BF16),
    }


def kernel(x, c, ctx, c_ctx, w_ada, b_ada, g_norm, w_in, w_s, b_s, g_q, g_k, b_gates, g_head, w_out,
           g_final):
    b, seq, d = x.shape
    ctx_len = ctx.shape[1]
    depth = w_ada.shape[0]
    tm = 256 if (ctx_len % 256 == 0 and seq % 256 == 0) else CHUNK
    n_ctx_tiles = ctx_len // tm
    n_ctx_chunks = ctx_len // M_CHUNK

    n_mod = b + 1
    rows = -(-n_mod // 8) * 8
    cc = jnp.zeros((rows, d), F32).at[:b].set(c).at[b].set(c_ctx)
    mods = _modulation(cc, w_ada, b_ada)[:, :n_mod]
    mods = mods.reshape(depth, n_mod, 3, 1, d)

    cos, sin = _rope_tables(ctx_len, seq)
    seg = np.kron(np.eye(B_HEADS, dtype=np.float32), np.full((HEAD_DIM, HEAD_DIM), 1.0 / HEAD_DIM, np.float32))
    consts = {"cos": cos, "sin": sin, "pm": jnp.asarray(seg, BF16)}
    gf = g_final.reshape(1, d)

    src = (ctx, x)
    mix = None
    for l in range(depth + 1):
        lw = mod = None
        if l < depth:
            lw = _layer_weights(l, (ctx_len + seq) // M_CHUNK, g_norm, w_in, w_s, b_s, g_q, g_k, b_gates,
                                g_head, w_out)
            sh, sc, gt = mods[l, :, 0], mods[l, :, 1], mods[l, :, 2]
            mod = (sc, sh)
        outs = _layer_call(src, mix, lw, mod, consts, gf, n_ctx_tiles, tm)
        if mix is not None:
            src = (outs[0],)
            outs = outs[1:]
        if l == depth:
            return src[0]
        (ya, q, k, vt, zb, cqt, ck, cvt, og, gr) = outs
        yb = _attn_call(q, k, vt, zb, lw["g_q"], lw["g_k"], n_ctx_tiles, tm, with_ctx=l < depth - 1)
        yc = _mlstm_call(cqt, ck, cvt, gr, og, lw, n_ctx_chunks)
        mix = (gt, ya, yb, yc, lw["w_out"])
```

```python
import functools

import jax
import jax.numpy as jnp
import numpy as np
from jax import lax
from jax.experimental import pallas as pl
from jax.experimental.pallas import tpu as pltpu

HEAD_DIM = 64
LANES = 128
GRID_W = 64
EPS = 1e-6
A_GROUPS = 4
A_WIDTH = A_GROUPS * HEAD_DIM
CHUNK = 128
B_HEADS = 8
B_KV_HEADS = 2
B_GROUP = B_HEADS // B_KV_HEADS
B_WIDTH = B_HEADS * HEAD_DIM
B_KV_WIDTH = B_KV_HEADS * HEAD_DIM
ATTN_SCALE = HEAD_DIM ** -0.5
ROPE_THETA = 10000.0
ROPE_FREQS = HEAD_DIM // 4
C_HEADS = 4
C_WIDTH = C_HEADS * HEAD_DIM
M_CHUNK = 128
N_GATES = 4 * C_HEADS
GATE_ROWS = 8

OFF_AU = 0
OFF_AV = OFF_AU + A_WIDTH
OFF_AZ = OFF_AV + A_WIDTH
OFF_BQ = OFF_AZ + A_WIDTH
OFF_BK = OFF_BQ + B_WIDTH
OFF_BV = OFF_BK + B_KV_WIDTH
OFF_BZ = OFF_BV + B_KV_WIDTH
OFF_CQ = OFF_BZ + B_WIDTH
OFF_CK = OFF_CQ + C_WIDTH
OFF_CV = OFF_CK + C_WIDTH
OFF_CO = OFF_CV + C_WIDTH
OFF_CZ = OFF_CO + C_WIDTH
OFF_CG = OFF_CZ + C_WIDTH
D_IN = OFF_CG + N_GATES

VMEM_LIMIT_BYTES = 56 * 1024 * 1024

F32 = jnp.float32
BF16 = jnp.bfloat16
GATE_DTYPE = BF16

LOG2E = float(np.log2(np.e))
QK_PRESCALE = ATTN_SCALE * LOG2E
SAFE_LOGIT_BOUND = 40.0


def _dot(a, b):
    return jnp.dot(a, b, preferred_element_type=F32)


def _dot_nt(a, b):
    return lax.dot_general(a, b, (((1,), (1,)), ((), ())), preferred_element_type=F32)


def _dot_tn(a, b):
    return lax.dot_general(a, b, (((0,), (0,)), ((), ())), preferred_element_type=F32)


def _silu(x):
    return x * jax.nn.sigmoid(x)


def _gelu_tanh(x):
    c = np.float32(np.sqrt(2.0 / np.pi))
    return 0.5 * x * (1.0 + jnp.tanh(c * (x + np.float32(0.044715) * (x * x * x))))


def _log_sigmoid(x):
    return jnp.minimum(x, 0.0) - jnp.log1p(jnp.exp(-jnp.abs(x)))


def _seg_mean(x, pm_ref):
    w = x.shape[-1]
    pm = pm_ref[0:w, 0:w]
    return _dot(x.astype(BF16), pm)


def _lane_iota(shape):
    return lax.broadcasted_iota(jnp.int32, shape, len(shape) - 1)


def _mod_kernel(cc_ref, w_ref, b_ref, o_ref):
    s = _silu(cc_ref[...])
    o_ref[0] = _dot(s.astype(BF16), w_ref[0].astype(BF16)) + b_ref[0]


def _modulation(cc, w_ada, b_ada):
    depth, d, d3 = w_ada.shape
    rows = cc.shape[0]
    tn = d
    return pl.pallas_call(
        _mod_kernel,
        grid=(depth, d3 // tn),
        in_specs=[
            pl.BlockSpec((rows, d), lambda l, n: (0, 0)),
            pl.BlockSpec((1, d, tn), lambda l, n: (l, 0, n)),
            pl.BlockSpec((1, 1, tn), lambda l, n: (l, 0, n)),
        ],
        out_specs=pl.BlockSpec((1, rows, tn), lambda l, n: (l, 0, n)),
        out_shape=jax.ShapeDtypeStruct((depth, rows, d3), F32),
        compiler_params=pltpu.CompilerParams(
            dimension_semantics=("arbitrary", "arbitrary"), vmem_limit_bytes=VMEM_LIMIT_BYTES),
        name="modulation",
    )(cc, w_ada, b_ada.reshape(depth, 1, d3))


def _proj_body(x, sc, sh, gn_ref, w_ref, wgr_ref, pm_ref, ws_ref, bs_ref,
               gq_ref, gk_ref, cos_ref, sin_ref,
               ya_ref, q_ref, k_ref, vt_ref, zb_ref, cqt_ref, ck_ref, cvt_ref, og_ref, gr_ref):
    tm = x.shape[0]
    ms = jnp.mean(x * x, axis=-1, keepdims=True)
    xn = (x * lax.rsqrt(ms + EPS)) * (gn_ref[...] * (1.0 + sc)) + sh
    xb = xn.astype(BF16)

    mixers = ((OFF_AU, OFF_BQ), (OFF_BQ, OFF_CQ), (OFF_CQ, OFF_CG))
    wide = {}

    def proj(off, width):
        for lo, hi in mixers:
            if lo <= off < hi:
                if lo not in wide:
                    wide[lo] = _dot(xb, w_ref[:, lo:hi])
                return wide[lo][:, off - lo:off - lo + width]
        raise ValueError(off)

    lane = _lane_iota((tm, LANES))
    lo_half = lane < HEAD_DIM

    gu = _gelu_tanh(proj(OFF_AU, A_WIDTH))
    gv = _gelu_tanh(proj(OFF_AV, A_WIDTH))
    mu = _seg_mean(gv, pm_ref)
    dv = gv - mu
    var = _seg_mean(dv * dv, pm_ref)
    vn = (dv * lax.rsqrt(var + EPS)).astype(BF16)
    sz = _silu(proj(OFF_AZ, A_WIDTH))
    lo_half_chunk = _lane_iota((CHUNK, LANES)) < HEAD_DIM
    for c in range(tm // CHUNK):
        r0, r1 = c * CHUNK, (c + 1) * CHUNK
        for j in range(A_WIDTH // LANES):
            c0, c1 = j * LANES, (j + 1) * LANES
            vs = vn[r0:r1, c0:c1]
            stacked = jnp.concatenate([jnp.where(lo_half_chunk, vs, 0.0), jnp.where(lo_half_chunk, 0.0, vs)],
                                      axis=0).astype(BF16)
            sv = _dot(ws_ref[j], stacked) + bs_ref[:, c0:c1]
            ya_ref[0, r0:r1, c0:c1] = (gu[r0:r1, c0:c1] * sv * sz[r0:r1, c0:c1]).astype(BF16)

    cos = cos_ref[...]
    sin = sin_ref[...]
    bit4 = (lane & ROPE_FREQS) != 0

    def rope(t):
        partner = jnp.where(bit4, pltpu.roll(t, ROPE_FREQS, 1), pltpu.roll(t, LANES - ROPE_FREQS, 1))
        return t * cos + partner * sin

    half_w = B_WIDTH // 2
    q_halves = []
    for off in (0, half_w):
        qh = proj(OFF_BQ + off, half_w)
        q_halves.append(qh * lax.rsqrt(_seg_mean(qh * qh, pm_ref) + EPS) * gq_ref[:, off:off + half_w])
    for j in range(B_WIDTH // LANES):
        jj = j % (half_w // LANES)
        qs = rope(q_halves[j // (half_w // LANES)][:, jj * LANES:(jj + 1) * LANES]) * QK_PRESCALE
        sw = pltpu.roll(qs, HEAD_DIM, 1)
        if (2 * j) // B_GROUP == 0:
            h_even = jnp.where(lo_half, qs, 0.0)
            h_odd = jnp.where(lo_half, sw, 0.0)
        else:
            h_even = jnp.where(lo_half, 0.0, sw)
            h_odd = jnp.where(lo_half, 0.0, qs)
        q_ref[0, :, (2 * j) * LANES:(2 * j + 1) * LANES] = h_even.astype(BF16)
        q_ref[0, :, (2 * j + 1) * LANES:(2 * j + 2) * LANES] = h_odd.astype(BF16)

    k = proj(OFF_BK, B_KV_WIDTH)
    kn = k * lax.rsqrt(_seg_mean(k * k, pm_ref) + EPS) * gk_ref[...]
    k_ref[0] = rope(kn).astype(BF16)
    vt_ref[0] = proj(OFF_BV, B_KV_WIDTH).T.astype(BF16)
    zb_ref[0] = _silu(proj(OFF_BZ, B_WIDTH)).astype(zb_ref.dtype)

    cq = proj(OFF_CQ, C_WIDTH)
    ck = proj(OFF_CK, C_WIDTH) * (HEAD_DIM ** -0.5)
    cv = proj(OFF_CV, C_WIDTH)
    for j in range(C_WIDTH // LANES):
        sl = slice(j * LANES, (j + 1) * LANES)
        for par in range(2):
            hd = 2 * j + par
            h_sl = slice(hd * LANES, (hd + 1) * LANES)

            def head_tile(src, pad):
                s = src[:, sl] if par == 0 else pltpu.roll(src[:, sl], HEAD_DIM, 1)
                return jnp.where(lo_half, s, pad)

            ck_ref[0, :, h_sl] = head_tile(ck, 0.0).astype(BF16)
            qt = head_tile(cq, 0.0)
            vt = head_tile(cv, 1.0)
            for c in range(tm // CHUNK):
                cqt_ref[0, c, h_sl, :] = qt[c * CHUNK:(c + 1) * CHUNK, :].T.astype(BF16)
                cvt_ref[0, c, h_sl, :] = vt[c * CHUNK:(c + 1) * CHUNK, :].T.astype(BF16)
    og = jax.nn.sigmoid(proj(OFF_CO, C_WIDTH)) * _silu(proj(OFF_CZ, C_WIDTH))
    og_ref[0] = og.astype(og_ref.dtype)
    gr = _dot_nt(wgr_ref[...], xb)
    for c in range(tm // CHUNK):
        for ty in range(4):
            gr_ref[0, ty, c * GATE_ROWS:(c + 1) * GATE_ROWS, :] = (
                gr[ty * GATE_ROWS:(ty + 1) * GATE_ROWS, c * CHUNK:(c + 1) * CHUNK])


N_MIX_IN = 6
N_PROJ_IN = 10
SUB_TILES = 3


def _layer_kernel(*refs, split_src, mix, proj, final, n_ctx_tiles, n_sub, tm, tile_off):
    refs = list(refs)
    if split_src:
        n_ctx_src = min(n_sub, n_ctx_tiles)
        ctx_src = [refs.pop(0) for _ in range(n_ctx_src)]
        lat_src = [refs.pop(0) for _ in range(n_sub)]
    else:
        h_ref = refs.pop(0)
    if mix:
        gt_lat, gt_ctx, ya_ref, yb_ref, yc_ref, w_ref = (refs.pop(0) for _ in range(N_MIX_IN))
    if final:
        gf_ref = refs.pop(0)
    if proj:
        sc_lat, sc_ctx, sh_lat, sh_ctx = (refs.pop(0) for _ in range(4))
        gn_ref, w_in_ref, wgr_ref, pm_ref, ws_ref, bs_ref, gq_ref, gk_ref, cos_ref, sin_ref = (
            refs.pop(0) for _ in range(N_PROJ_IN))
    if mix:
        h_out = refs.pop(0)
    n_chunks = tm // CHUNK

    for j in range(n_sub):
        rs = slice(j * tm, (j + 1) * tm)
        is_ctx = (pl.program_id(1) + tile_off) * n_sub + j < n_ctx_tiles
        if split_src:
            x = lat_src[j][0]
            if j < len(ctx_src):
                x = jnp.where(is_ctx, ctx_src[j][0], x)
        else:
            x = h_ref[0, rs, :]
        if mix:
            acc = _dot(ya_ref[0, rs, :], w_ref[0:A_WIDTH, :])
            acc += _dot(yb_ref[0, rs, :], w_ref[A_WIDTH:A_WIDTH + B_WIDTH, :])
            acc += _dot(yc_ref[0, rs, :], w_ref[A_WIDTH + B_WIDTH:, :])
            x = x + jnp.where(is_ctx, gt_ctx[0], gt_lat[0]) * acc
        if final:
            ms = jnp.mean(x * x, axis=-1, keepdims=True)
            x = (x * lax.rsqrt(ms + EPS)) * gf_ref[...]
        if mix:
            h_out[0, rs, :] = x
        if proj:
            ya, q, k, vt, zb, cqt, ck, cvt, og, gr = refs
            cs = slice(j * n_chunks, (j + 1) * n_chunks)
            _proj_body(
                x, jnp.where(is_ctx, sc_ctx[0], sc_lat[0]), jnp.where(is_ctx, sh_ctx[0], sh_lat[0]),
                gn_ref, w_in_ref, wgr_ref, pm_ref, ws_ref, bs_ref, gq_ref, gk_ref,
                cos_ref.at[rs, :], sin_ref.at[rs, :],
                ya.at[:, rs, :], q.at[:, rs, :], k.at[:, rs, :], vt.at[:, :, rs], zb.at[:, rs, :],
                cqt.at[:, cs], ck.at[:, rs, :], cvt.at[:, cs], og.at[:, rs, :],
                gr.at[:, :, j * n_chunks * GATE_ROWS:(j + 1) * n_chunks * GATE_ROWS, :])


def _layer_call(src, mix, lw, mod, consts, g_final, n_ctx_tiles, tm):
    split_src = len(src) == 2
    final = lw is None
    b, d = src[0].shape[0], src[0].shape[2]
    t = src[0].shape[1] + src[1].shape[1] if split_src else src[0].shape[1]
    off = n_ctx_tiles if final else 0
    n_sub = SUB_TILES if (not final and (t // tm) % SUB_TILES == 0) else 1
    tb = tm * n_sub
    nt = (t // tm - off) // n_sub

    def tile3(w):
        return pl.BlockSpec((1, tb, w), lambda bi, i: (bi, i + off, 0))

    def whole(a):
        nd = a.ndim
        return pl.BlockSpec(a.shape, lambda bi, i: (0,) * nd)

    def mod_specs(a):
        n_mod = a.shape[0]
        return [pl.BlockSpec((1, 1, d), lambda bi, i: (bi, 0, 0)),
                pl.BlockSpec((1, 1, d), lambda bi, i: (n_mod - 1, 0, 0))]

    if split_src:
        ctx, x = src
        n_ctx_src = min(n_sub, n_ctx_tiles)
        ins = [ctx] * n_ctx_src + [x] * n_sub
        in_specs = [pl.BlockSpec((1, tm, d), functools.partial(
            lambda bi, i, j: (bi, jnp.minimum(i * n_sub + j, n_ctx_tiles - 1), 0), j=j))
            for j in range(n_ctx_src)]
        in_specs += [pl.BlockSpec((1, tm, d), functools.partial(
            lambda bi, i, j: (bi, jnp.maximum(i * n_sub + j - n_ctx_tiles, 0), 0), j=j))
            for j in range(n_sub)]
    else:
        ins, in_specs = [src[0]], [tile3(d)]
    out_shape, out_specs, aliases = [], [], {}
    if mix is not None:
        gt, ya, yb, yc, w_out = mix
        yb_spec = tile3(B_WIDTH) if yb.shape[1] == t else pl.BlockSpec((1, tb, B_WIDTH), lambda bi, i: (bi, i, 0))
        ins += [gt, gt, ya, yb, yc, w_out]
        in_specs += mod_specs(gt) + [tile3(A_WIDTH), yb_spec, tile3(C_WIDTH), whole(w_out)]
        if final:
            ins.append(g_final)
            in_specs.append(whole(g_final))
            out_shape.append(jax.ShapeDtypeStruct((b, t - off * tm, d), F32))
            out_specs.append(pl.BlockSpec((1, tb, d), lambda bi, i: (bi, i, 0)))
        else:
            out_shape.append(jax.ShapeDtypeStruct((b, t, d), F32))
            out_specs.append(tile3(d))
            if not split_src:
                aliases = {0: 0}
    if not final:
        sc, sh = mod
        ins += [sc, sc, sh, sh, lw["g_norm"], lw["w_main"], lw["w_gr"], consts["pm"], lw["w_s"],
                lw["b_s"], lw["g_q"], lw["g_k"], consts["cos"], consts["sin"]]
        in_specs += mod_specs(sc) + mod_specs(sh) + [
            whole(lw["g_norm"]), whole(lw["w_main"]), whole(lw["w_gr"]),
            whole(consts["pm"]), whole(lw["w_s"]), whole(lw["b_s"]), whole(lw["g_q"]),
            whole(lw["g_k"]),
            pl.BlockSpec((tb, LANES), lambda bi, i: (i, 0)),
            pl.BlockSpec((tb, LANES), lambda bi, i: (i, 0))]
        pad_w = C_HEADS * LANES
        out_shape += [
            jax.ShapeDtypeStruct((b, t, A_WIDTH), BF16),
            jax.ShapeDtypeStruct((b, t, B_HEADS * LANES), BF16),
            jax.ShapeDtypeStruct((b, t, B_KV_WIDTH), BF16),
            jax.ShapeDtypeStruct((b, B_KV_WIDTH, t), BF16),
            jax.ShapeDtypeStruct((b, t, B_WIDTH), GATE_DTYPE),
            jax.ShapeDtypeStruct((b, t // CHUNK, pad_w, CHUNK), BF16),
            jax.ShapeDtypeStruct((b, t, pad_w), BF16),
            jax.ShapeDtypeStruct((b, t // CHUNK, pad_w, CHUNK), BF16),
            jax.ShapeDtypeStruct((b, t, C_WIDTH), GATE_DTYPE),
            jax.ShapeDtypeStruct((b, 4, t // CHUNK * GATE_ROWS, CHUNK), F32),
        ]
        out_specs += [tile3(A_WIDTH), tile3(B_HEADS * LANES), tile3(B_KV_WIDTH),
                      pl.BlockSpec((1, B_KV_WIDTH, tb), lambda bi, i: (bi, 0, i)),
                      tile3(B_WIDTH),
                      pl.BlockSpec((1, tb // CHUNK, pad_w, CHUNK), lambda bi, i: (bi, i, 0, 0)),
                      tile3(pad_w),
                      pl.BlockSpec((1, tb // CHUNK, pad_w, CHUNK), lambda bi, i: (bi, i, 0, 0)),
                      tile3(C_WIDTH),
                      pl.BlockSpec((1, 4, tb // CHUNK * GATE_ROWS, CHUNK), lambda bi, i: (bi, 0, i, 0))]
    kern = functools.partial(_layer_kernel, split_src=split_src, mix=mix is not None, proj=not final,
                             final=final, n_ctx_tiles=n_ctx_tiles, n_sub=n_sub, tm=tm, tile_off=off)
    return pl.pallas_call(
        kern,
        grid=(b, nt),
        in_specs=in_specs,
        out_specs=out_specs,
        out_shape=out_shape,
        input_output_aliases=aliases,
        compiler_params=pltpu.CompilerParams(
            dimension_semantics=("arbitrary", "arbitrary"), vmem_limit_bytes=VMEM_LIMIT_BYTES),
        name="layer_tile",
    )(*ins)


def _attn_kernel(q_ref, k_ref, vt_ref, zb_ref, gq_ref, gk_ref, y_ref, *, n_ctx_tiles, tile_off, ctx_len):
    t_all = k_ref.shape[1]

    tq = q_ref.shape[1]

    def attend(n_keys, stabilise):
        kk = k_ref[0, 0:n_keys, :]
        vt = vt_ref[0, :, 0:n_keys]
        for g in range(B_KV_HEADS):
            q_grp = jnp.concatenate(
                [q_ref[0, :, h * LANES:(h + 1) * LANES] for h in range(g * B_GROUP, (g + 1) * B_GROUP)],
                axis=0)
            s = _dot_nt(kk, q_grp)
            if stabilise:
                s = s - jnp.max(s, axis=0, keepdims=True)
            p = jnp.exp2(s)
            l = jnp.sum(p, axis=0, keepdims=True)
            o = _dot(vt, p.astype(BF16))[g * HEAD_DIM:(g + 1) * HEAD_DIM, :] / l
            for jj in range(B_GROUP // 2):
                pair = jnp.concatenate([o[:, (2 * jj) * tq:(2 * jj + 1) * tq],
                                        o[:, (2 * jj + 1) * tq:(2 * jj + 2) * tq]], axis=0).T
                j = g * (B_GROUP // 2) + jj
                sl = slice(j * LANES, (j + 1) * LANES)
                y_ref[0, :, sl] = (pair * zb_ref[0, :, sl].astype(F32)).astype(BF16)

    bound = (HEAD_DIM * ATTN_SCALE) * jnp.max(jnp.abs(gq_ref[...])) * jnp.max(jnp.abs(gk_ref[...]))
    safe = bound <= SAFE_LOGIT_BOUND
    i = pl.program_id(1) + tile_off
    if n_ctx_tiles > 0 and tile_off == 0:
        branches = ((i < n_ctx_tiles, ctx_len), (i >= n_ctx_tiles, t_all))
    else:
        branches = ((None, t_all),)
    for cond, n_keys in branches:
        for stabilise in (False, True):
            pred = safe if not stabilise else jnp.logical_not(safe)
            if cond is not None:
                pred = jnp.logical_and(pred, cond)
            pl.when(pred)(functools.partial(attend, n_keys, stabilise))


def _attn_call(q, k, vt, zb, g_q, g_k, n_ctx_tiles, tq, with_ctx):
    b, t, _ = q.shape
    nt = t // tq
    tile_off = 0 if with_ctx else n_ctx_tiles
    kern = functools.partial(_attn_kernel, n_ctx_tiles=n_ctx_tiles, tile_off=tile_off,
                             ctx_len=n_ctx_tiles * tq)
    return pl.pallas_call(
        kern,
        grid=(b, nt - tile_off),
        in_specs=[
            pl.BlockSpec((1, tq, B_HEADS * LANES), lambda bi, i: (bi, i + tile_off, 0)),
            pl.BlockSpec((1, t, B_KV_WIDTH), lambda bi, i: (bi, 0, 0)),
            pl.BlockSpec((1, B_KV_WIDTH, t), lambda bi, i: (bi, 0, 0)),
            pl.BlockSpec((1, tq, B_WIDTH), lambda bi, i: (bi, i + tile_off, 0)),
            pl.BlockSpec(g_q.shape, lambda bi, i: (0, 0)),
            pl.BlockSpec(g_k.shape, lambda bi, i: (0, 0)),
        ],
        out_specs=pl.BlockSpec((1, tq, B_WIDTH), lambda bi, i: (bi, i, 0)),
        out_shape=jax.ShapeDtypeStruct((b, t - tile_off * tq, B_WIDTH), BF16),
        compiler_params=pltpu.CompilerParams(
            dimension_semantics=("arbitrary", "arbitrary"), vmem_limit_bytes=VMEM_LIMIT_BYTES),
        name="attention",
    )(q, k, vt, zb, g_q, g_k)


def _mlstm_kernel(cqt_ref, ck_ref, cvt_ref, gr_ref, og_ref, br_ref, gh_ref,
                  y_ref, hf_ref, hb_ref, st_ref, m_ref, gp_ref, sel_ref, madd_ref, *, n_ctx_chunks):
    nc = cqt_ref.shape[1]
    L = M_CHUNK
    n_units = 2 * C_HEADS
    W = n_units * L
    n_rows = nc * GATE_ROWS
    n_spread = 2 * GATE_ROWS
    row = lax.broadcasted_iota(jnp.int32, (L, L), 0)
    col = lax.broadcasted_iota(jnp.int32, (L, L), 1)
    tri_row = tuple(jnp.where(mk, 1.0, 0.0).astype(BF16) for mk in (row <= col, row >= col))
    eye_bf = jnp.where(row == col, 1.0, 0.0).astype(BF16)
    @pl.when(pl.program_id(0) == 0)
    def _():
        srow = lax.broadcasted_iota(jnp.int32, (L, W), 0)
        scol = lax.broadcasted_iota(jnp.int32, (L, W), 1)
        sel_ref[...] = jnp.where((srow < 3 * n_spread)
                                 & (((srow // GATE_ROWS) % 2) == (scol // (C_HEADS * L)))
                                 & ((srow % GATE_ROWS) == ((scol // L) % C_HEADS)), 1.0, 0.0).astype(BF16)
        upward = scol < C_HEADS * L
        hidden = (upward & (srow > (scol % L))) | (jnp.logical_not(upward) & (srow < (scol % L)))
        madd_ref[...] = jnp.where(hidden, -jnp.inf, 0.0).astype(F32)

    st_ref[...] = jnp.zeros(st_ref.shape, F32)
    m_ref[...] = jnp.zeros(m_ref.shape, F32)

    lane_r = _lane_iota((n_rows, L))
    li, b, bend = [], [], []
    for d in range(2):
        li.append((gr_ref[0, 2 * d] + br_ref[2 * d]) * LOG2E)
        lf = _log_sigmoid(gr_ref[0, 2 * d + 1] + br_ref[2 * d + 1]) * LOG2E
        lf_hi = lf.astype(BF16)
        lf_r1 = lf - lf_hi.astype(F32)
        lf_mid = lf_r1.astype(BF16)
        lf_lo = (lf_r1 - lf_mid.astype(F32)).astype(BF16)
        b.append(_dot(lf_hi, tri_row[d]) + _dot(lf_mid, tri_row[d]) + _dot(lf_lo, tri_row[d]))
        last = L - 1 if d == 0 else 0
        bend.append(jnp.broadcast_to(b[d][:, last:last + 1], (n_rows, L)))
    g = [li[d] - b[d] for d in range(2)]
    cm = list(g)
    sh = 1
    while sh < L:
        cm[0] = jnp.maximum(cm[0], jnp.where(lane_r >= sh, pltpu.roll(cm[0], sh, 1), -jnp.inf))
        cm[1] = jnp.maximum(cm[1], jnp.where(lane_r < L - sh, pltpu.roll(cm[1], L - sh, 1), -jnp.inf))
        sh *= 2
    for d in range(2):
        w_end = bend[d] - b[d] + li[d]
        mx = jnp.max(w_end, axis=1, keepdims=True)
        gp_ref[d, 0] = g[d]
        gp_ref[d, 1] = cm[d]
        gp_ref[d, 2] = -(b[d] + cm[d])
        gp_ref[d, 3] = jnp.exp2(w_end - mx)
        gp_ref[d, 4] = bend[d]
        gp_ref[d, 5] = jnp.broadcast_to(mx, (n_rows, L))

    def unit_lanes(per_dir):
        return jnp.concatenate([per_dir[d][h:h + 1, :] for d in range(2) for h in range(C_HEADS)], axis=1)

    def chunks_of(i):
        return (i, jnp.where(i < n_ctx_chunks, n_ctx_chunks - 1 - i, nc + n_ctx_chunks - 1 - i))

    def gate_rows(chunks):
        return tuple(pl.ds(pl.multiple_of(c * GATE_ROWS, GATE_ROWS), GATE_ROWS) for c in chunks)

    def step(i, carry):
        chunks = chunks_of(i)
        rows = tuple(pl.ds(pl.multiple_of(c * L, L), L) for c in chunks)
        grow = gate_rows(chunks)
        g, cm, nbc, wk0, bend, mx = (tuple(gp_ref[d, k, grow[d], :] for d in range(2)) for k in range(6))
        m_old = (m_ref[0], m_ref[1])

        decay, wk = [], []
        for d in range(2):
            m_new = jnp.maximum(bend[d] + m_old[d], mx[d])
            decay.append(jnp.exp2(bend[d] + m_old[d] - m_new))
            wk.append(wk0[d] * jnp.exp2(mx[d] - m_new))
            m_ref[d] = m_new

        r16 = jnp.concatenate([g[0], g[1]], axis=0)
        hi = r16.astype(BF16)
        r1 = r16 - hi.astype(F32)
        mid = r1.astype(BF16)
        lo = (r1 - mid.astype(F32)).astype(BF16)
        pieces = jnp.concatenate([hi, mid, lo, jnp.zeros((L - 3 * n_spread, L), BF16)], axis=0)
        pieces_t = _dot_nt(eye_bf, pieces).astype(BF16)
        g_all = _dot(pieces_t, sel_ref[...])

        e0 = jnp.exp2((g_all - unit_lanes(cm)) + madd_ref[...])

        k_d = tuple(ck_ref[0, rows[d], :] for d in range(2))
        qt_d = tuple(cqt_ref[0, chunks[d]] for d in range(2))
        vt_d = tuple(cvt_ref[0, chunks[d]] for d in range(2))
        units = [(d, h) for d in range(2) for h in range(C_HEADS)]
        st_old, qk, bq = [], [], []
        for u, (d, h) in enumerate(units):
            sl = slice(h * LANES, (h + 1) * LANES)
            st = st_ref[u]
            st_old.append(st)
            r = _dot(jnp.concatenate([k_d[d][:, sl], st.astype(BF16)], axis=0), qt_d[d][sl, :])
            qk.append(r[0:L])
            bq.append(r[L:2 * L])
        s0 = (jnp.concatenate(qk, axis=1) * e0).astype(BF16)

        acc, upd = [], []
        for u, (d, h) in enumerate(units):
            sl = slice(h * LANES, (h + 1) * LANES)
            vt = vt_d[d][sl, :]
            vw = (vt.astype(F32) * wk[d][h:h + 1, :]).astype(BF16)
            acc.append(_dot(vt, s0[:, u * L:(u + 1) * L]))
            upd.append(_dot(vw, k_d[d][:, sl]))

        f, inter, thr = [], [], []
        for d in range(2):
            diff = cm[d] - m_old[d]
            below = jnp.minimum(diff, 0.0)
            f.append(jnp.exp2(below))
            inter.append(jnp.exp2(jnp.minimum(-diff, 0.0)))
            thr.append(jnp.exp2(nbc[d] + below))
        na = unit_lanes(f) * jnp.concatenate(acc, axis=1) + unit_lanes(inter) * jnp.concatenate(bq, axis=1)
        den = na[HEAD_DIM:HEAD_DIM + 1, :]
        h_all = na[0:HEAD_DIM, :] * (1.0 / jnp.maximum(jnp.abs(den), unit_lanes(thr)))
        for u, (d, h) in enumerate(units):
            dst = hf_ref if d == 0 else hb_ref
            dst[chunks[d], h * HEAD_DIM:(h + 1) * HEAD_DIM, :] = h_all[:, u * L:(u + 1) * L]

        for u, (d, h) in enumerate(units):
            st_ref[u] = decay[d][h:h + 1, :] * st_old[u] + upd[u]
        return carry

    lax.fori_loop(0, nc, step, 0, unroll=6)

    def combine(c, carry):
        rows = pl.ds(pl.multiple_of(c * L, L), L)
        hs = hf_ref[c] + hb_ref[c]
        normed = []
        for hd in range(C_HEADS):
            blk = hs[hd * HEAD_DIM:(hd + 1) * HEAD_DIM, :]
            ms = jnp.sum(blk * blk, axis=0, keepdims=True) * (1.0 / HEAD_DIM)
            normed.append(blk * lax.rsqrt(ms + EPS))
        y = (jnp.concatenate(normed, axis=0) * gh_ref[...]).T
        y_ref[0, rows, :] = (og_ref[0, rows, :].astype(F32) * y).astype(BF16)
        return carry

    lax.fori_loop(0, nc, combine, 0, unroll=3)


def _mlstm_call(cqt, ck, cvt, gr, og, lw, n_ctx_chunks):
    b, t, pad_w = ck.shape

    def per_b(a):
        nd = a.ndim
        return pl.BlockSpec((1,) + a.shape[1:], lambda bi: (bi,) + (0,) * (nd - 1))

    def whole(a):
        nd = a.ndim
        return pl.BlockSpec(a.shape, lambda bi: (0,) * nd)

    kern = functools.partial(_mlstm_kernel, n_ctx_chunks=n_ctx_chunks)
    return pl.pallas_call(
        kern,
        grid=(b,),
        in_specs=[per_b(cqt), per_b(ck), per_b(cvt), per_b(gr), per_b(og),
                  whole(lw["bg_r"]), whole(lw["g_head"])],
        out_specs=pl.BlockSpec((1, t, C_WIDTH), lambda bi: (bi, 0, 0)),
        out_shape=jax.ShapeDtypeStruct((b, t, C_WIDTH), BF16),
        scratch_shapes=[
            pltpu.VMEM((t // M_CHUNK, C_WIDTH, M_CHUNK), F32),
            pltpu.VMEM((t // M_CHUNK, C_WIDTH, M_CHUNK), F32),
            pltpu.VMEM((2 * C_HEADS, LANES, LANES), F32),
            pltpu.VMEM((2, GATE_ROWS, LANES), F32),
            pltpu.VMEM((2, 6, t // M_CHUNK * GATE_ROWS, M_CHUNK), F32),
            pltpu.VMEM((M_CHUNK, 2 * C_HEADS * M_CHUNK), BF16),
            pltpu.VMEM((M_CHUNK, 2 * C_HEADS * M_CHUNK), F32),
        ],
        compiler_params=pltpu.CompilerParams(
            dimension_semantics=("arbitrary",), vmem_limit_bytes=VMEM_LIMIT_BYTES),
        name="mlstm",
    )(cqt, ck, cvt, gr, og, lw["bg_r"], lw["g_head"])


def _rope_tables(ctx_len, seq):
    pos = np.arange(seq)
    freqs = ROPE_THETA ** (-np.arange(ROPE_FREQS, dtype=np.float32) / ROPE_FREQS)
    ang_r = (pos // GRID_W).astype(np.float32)[:, None] * freqs
    ang_c = (pos % GRID_W).astype(np.float32)[:, None] * freqs
    cos_h = np.concatenate([np.cos(ang_r), np.cos(ang_r), np.cos(ang_c), np.cos(ang_c)], axis=1)
    sin_h = np.concatenate([-np.sin(ang_r), np.sin(ang_r), -np.sin(ang_c), np.sin(ang_c)], axis=1)
    cos = np.concatenate([np.ones((ctx_len, HEAD_DIM), np.float32), cos_h.astype(np.float32)], axis=0)
    sin = np.concatenate([np.zeros((ctx_len, HEAD_DIM), np.float32), sin_h.astype(np.float32)], axis=0)
    return jnp.asarray(np.tile(cos, (1, 2))), jnp.asarray(np.tile(sin, (1, 2)))


def _layer_weights(l, n_chunks, g_norm, w_in, w_s, b_s, g_q, g_k, b_gates, g_head, w_out):
    d = w_in.shape[1]
    w_l = w_in[l]
    w_g = w_l[:, OFF_CG:]
    w_gr = jnp.zeros((4, GATE_ROWS, d), F32).at[:, :C_HEADS, :].set(w_g.T.reshape(4, C_HEADS, d))
    bg_r = jnp.zeros((4, GATE_ROWS), F32).at[:, :C_HEADS].set(b_gates[l].reshape(4, C_HEADS))
    return {
        "g_norm": g_norm[l].reshape(1, d),
        "w_main": w_l[:, :OFF_CG].astype(BF16),
        "w_gr": w_gr.reshape(4 * GATE_ROWS, d).astype(BF16),
        "w_s": jnp.concatenate([w_s[l, 0::2], w_s[l, 1::2]], axis=2).astype(BF16),
        "b_s": jnp.repeat(b_s[l].T, HEAD_DIM, axis=1),
        "g_q": jnp.tile(g_q[l], B_HEADS).reshape(1, B_WIDTH),
        "g_k": jnp.tile(g_k[l], B_KV_HEADS).reshape(1, B_KV_WIDTH),
        "bg_r": jnp.broadcast_to(jnp.tile(bg_r, (1, n_chunks))[:, :, None],
                                 (4, n_chunks * GATE_ROWS, M_CHUNK)),
        "g_head": jnp.broadcast_to(g_head[l][:, None], (C_WIDTH, M_CHUNK)),
        "w_out": w_out[l].astype(BF16),
    }


def kernel(x, c, ctx, c_ctx, w_ada, b_ada, g_norm, w_in, w_s, b_s, g_q, g_k, b_gates, g_head, w_out,
           g_final):
    b, seq, d = x.shape
    ctx_len = ctx.shape[1]
    depth = w_ada.shape[0]
    tm = 256 if (ctx_len % 256 == 0 and seq % 256 == 0) else CHUNK
    n_ctx_tiles = ctx_len // tm
    n_ctx_chunks = ctx_len // M_CHUNK

    n_mod = b + 1
    rows = -(-n_mod // 8) * 8
    cc = jnp.zeros((rows, d), F32).at[:b].set(c).at[b].set(c_ctx)
    mods = _modulation(cc, w_ada, b_ada)[:, :n_mod]
    mods = mods.reshape(depth, n_mod, 3, 1, d)

    cos, sin = _rope_tables(ctx_len, seq)
    seg = np.kron(np.eye(B_HEADS, dtype=np.float32), np.full((HEAD_DIM, HEAD_DIM), 1.0 / HEAD_DIM, np.float32))
    consts = {"cos": cos, "sin": sin, "pm": jnp.asarray(seg, BF16)}
    gf = g_final.reshape(1, d)

    src = (ctx, x)
    mix = None
    for l in range(depth + 1):
        lw = mod = None
        if l < depth:
            lw = _layer_weights(l, (ctx_len + seq) // M_CHUNK, g_norm, w_in, w_s, b_s, g_q, g_k, b_gates,
                                g_head, w_out)
            sh, sc, gt = mods[l, :, 0], mods[l, :, 1], mods[l, :, 2]
            mod = (sc, sh)
        outs = _layer_call(src, mix, lw, mod, consts, gf, n_ctx_tiles, tm)
        if mix is not None:
            src = (outs[0],)
            outs = outs[1:]
        if l == depth:
            return src[0]
        (ya, q, k, vt, zb, cqt, ck, cvt, og, gr) = outs
        yb = _attn_call(q, k, vt, zb, lw["g_q"], lw["g_k"], n_ctx_tiles, tm, with_ctx=l < depth - 1)
        yc = _mlstm_call(cqt, ck, cvt, gr, og, lw, n_ctx_chunks)
        mix = (gt, ya, yb, yc, lw["w_out"])
```

```python
import functools

import jax
import jax.numpy as jnp
import numpy as np
from jax import lax
from jax.experimental import pallas as pl
from jax.experimental.pallas import tpu as pltpu

HEAD_DIM = 64
LANES = 128
GRID_W = 64
EPS = 1e-6
A_GROUPS = 4
A_WIDTH = A_GROUPS * HEAD_DIM
CHUNK = 128
B_HEADS = 8
B_KV_HEADS = 2
B_GROUP = B_HEADS // B_KV_HEADS
B_WIDTH = B_HEADS * HEAD_DIM
B_KV_WIDTH = B_KV_HEADS * HEAD_DIM
ATTN_SCALE = HEAD_DIM ** -0.5
ROPE_THETA = 10000.0
ROPE_FREQS = HEAD_DIM // 4
C_HEADS = 4
C_WIDTH = C_HEADS * HEAD_DIM
M_CHUNK = 128
N_GATES = 4 * C_HEADS
GATE_ROWS = 8

OFF_AU = 0
OFF_AV = OFF_AU + A_WIDTH
OFF_AZ = OFF_AV + A_WIDTH
OFF_BQ = OFF_AZ + A_WIDTH
OFF_BK = OFF_BQ + B_WIDTH
OFF_BV = OFF_BK + B_KV_WIDTH
OFF_BZ = OFF_BV + B_KV_WIDTH
OFF_CQ = OFF_BZ + B_WIDTH
OFF_CK = OFF_CQ + C_WIDTH
OFF_CV = OFF_CK + C_WIDTH
OFF_CO = OFF_CV + C_WIDTH
OFF_CZ = OFF_CO + C_WIDTH
OFF_CG = OFF_CZ + C_WIDTH
D_IN = OFF_CG + N_GATES

VMEM_LIMIT_BYTES = 56 * 1024 * 1024

F32 = jnp.float32
BF16 = jnp.bfloat16
GATE_DTYPE = BF16

LOG2E = float(np.log2(np.e))
QK_PRESCALE = ATTN_SCALE * LOG2E
SAFE_LOGIT_BOUND = 40.0


def _dot(a, b):
    return jnp.dot(a, b, preferred_element_type=F32)


def _dot_nt(a, b):
    return lax.dot_general(a, b, (((1,), (1,)), ((), ())), preferred_element_type=F32)


def _dot_tn(a, b):
    return lax.dot_general(a, b, (((0,), (0,)), ((), ())), preferred_element_type=F32)


def _silu(x):
    return x * jax.nn.sigmoid(x)


def _gelu_tanh(x):
    c = np.float32(np.sqrt(2.0 / np.pi))
    return 0.5 * x * (1.0 + jnp.tanh(c * (x + np.float32(0.044715) * (x * x * x))))


def _log_sigmoid(x):
    return jnp.minimum(x, 0.0) - jnp.log1p(jnp.exp(-jnp.abs(x)))


def _seg_mean(x, pm_ref):
    w = x.shape[-1]
    pm = pm_ref[0:w, 0:w]
    return _dot(x.astype(BF16), pm)


def _lane_iota(shape):
    return lax.broadcasted_iota(jnp.int32, shape, len(shape) - 1)


def _mod_kernel(cc_ref, w_ref, b_ref, o_ref):
    s = _silu(cc_ref[...])
    o_ref[0] = _dot(s.astype(BF16), w_ref[0].astype(BF16)) + b_ref[0]


def _modulation(cc, w_ada, b_ada):
    depth, d, d3 = w_ada.shape
    rows = cc.shape[0]
    tn = d
    return pl.pallas_call(
        _mod_kernel,
        grid=(depth, d3 // tn),
        in_specs=[
            pl.BlockSpec((rows, d), lambda l, n: (0, 0)),
            pl.BlockSpec((1, d, tn), lambda l, n: (l, 0, n)),
            pl.BlockSpec((1, 1, tn), lambda l, n: (l, 0, n)),
        ],
        out_specs=pl.BlockSpec((1, rows, tn), lambda l, n: (l, 0, n)),
        out_shape=jax.ShapeDtypeStruct((depth, rows, d3), F32),
        compiler_params=pltpu.CompilerParams(
            dimension_semantics=("arbitrary", "arbitrary"), vmem_limit_bytes=VMEM_LIMIT_BYTES),
        name="modulation",
    )(cc, w_ada, b_ada.reshape(depth, 1, d3))


def _proj_body(x, sc, sh, gn_ref, w_ref, wgr_ref, pm_ref, ws_ref, bs_ref,
               gq_ref, gk_ref, cos_ref, sin_ref,
               ya_ref, q_ref, k_ref, vt_ref, zb_ref, cqt_ref, ck_ref, cvt_ref, og_ref, gr_ref):
    tm = x.shape[0]
    ms = jnp.mean(x * x, axis=-1, keepdims=True)
    xn = (x * lax.rsqrt(ms + EPS)) * (gn_ref[...] * (1.0 + sc)) + sh
    xb = xn.astype(BF16)

    mixers = ((OFF_AU, OFF_BQ), (OFF_BQ, OFF_CQ), (OFF_CQ, OFF_CG))
    wide = {}

    def proj(off, width):
        for lo, hi in mixers:
            if lo <= off < hi:
                if lo not in wide:
                    wide[lo] = _dot(xb, w_ref[:, lo:hi])
                return wide[lo][:, off - lo:off - lo + width]
        raise ValueError(off)

    lane = _lane_iota((tm, LANES))
    lo_half = lane < HEAD_DIM

    gu = _gelu_tanh(proj(OFF_AU, A_WIDTH))
    gv = _gelu_tanh(proj(OFF_AV, A_WIDTH))
    mu = _seg_mean(gv, pm_ref)
    dv = gv - mu
    var = _seg_mean(dv * dv, pm_ref)
    vn = (dv * lax.rsqrt(var + EPS)).astype(BF16)
    sz = _silu(proj(OFF_AZ, A_WIDTH))
    lo_half_chunk = _lane_iota((CHUNK, LANES)) < HEAD_DIM
    for c in range(tm // CHUNK):
        r0, r1 = c * CHUNK, (c + 1) * CHUNK
        for j in range(A_WIDTH // LANES):
            c0, c1 = j * LANES, (j + 1) * LANES
            vs = vn[r0:r1, c0:c1]
            stacked = jnp.concatenate([jnp.where(lo_half_chunk, vs, 0.0), jnp.where(lo_half_chunk, 0.0, vs)],
                                      axis=0).astype(BF16)
            sv = _dot(ws_ref[j], stacked) + bs_ref[:, c0:c1]
            ya_ref[0, r0:r1, c0:c1] = (gu[r0:r1, c0:c1] * sv * sz[r0:r1, c0:c1]).astype(BF16)

    cos = cos_ref[...]
    sin = sin_ref[...]
    bit4 = (lane & ROPE_FREQS) != 0

    def rope(t):
        partner = jnp.where(bit4, pltpu.roll(t, ROPE_FREQS, 1), pltpu.roll(t, LANES - ROPE_FREQS, 1))
        return t * cos + partner * sin

    half_w = B_WIDTH // 2
    q_halves = []
    for off in (0, half_w):
        qh = proj(OFF_BQ + off, half_w)
        q_halves.append(qh * lax.rsqrt(_seg_mean(qh * qh, pm_ref) + EPS) * gq_ref[:, off:off + half_w])
    for j in range(B_WIDTH // LANES):
        jj = j % (half_w // LANES)
        qs = rope(q_halves[j // (half_w // LANES)][:, jj * LANES:(jj + 1) * LANES]) * QK_PRESCALE
        sw = pltpu.roll(qs, HEAD_DIM, 1)
        if (2 * j) // B_GROUP == 0:
            h_even = jnp.where(lo_half, qs, 0.0)
            h_odd = jnp.where(lo_half, sw, 0.0)
        else:
            h_even = jnp.where(lo_half, 0.0, sw)
            h_odd = jnp.where(lo_half, 0.0, qs)
        q_ref[0, :, (2 * j) * LANES:(2 * j + 1) * LANES] = h_even.astype(BF16)
        q_ref[0, :, (2 * j + 1) * LANES:(2 * j + 2) * LANES] = h_odd.astype(BF16)

    k = proj(OFF_BK, B_KV_WIDTH)
    kn = k * lax.rsqrt(_seg_mean(k * k, pm_ref) + EPS) * gk_ref[...]
    k_ref[0] = rope(kn).astype(BF16)
    vt_ref[0] = proj(OFF_BV, B_KV_WIDTH).T.astype(BF16)
    zb_ref[0] = _silu(proj(OFF_BZ, B_WIDTH)).astype(zb_ref.dtype)

    cq = proj(OFF_CQ, C_WIDTH)
    ck = proj(OFF_CK, C_WIDTH) * (HEAD_DIM ** -0.5)
    cv = proj(OFF_CV, C_WIDTH)
    for j in range(C_WIDTH // LANES):
        sl = slice(j * LANES, (j + 1) * LANES)
        for par in range(2):
            hd = 2 * j + par
            h_sl = slice(hd * LANES, (hd + 1) * LANES)

            def head_tile(src, pad):
                s = src[:, sl] if par == 0 else pltpu.roll(src[:, sl], HEAD_DIM, 1)
                return jnp.where(lo_half, s, pad)

            ck_ref[0, :, h_sl] = head_tile(ck, 0.0).astype(BF16)
            qt = head_tile(cq, 0.0)
            vt = head_tile(cv, 1.0)
            for c in range(tm // CHUNK):
                cqt_ref[0, c, h_sl, :] = qt[c * CHUNK:(c + 1) * CHUNK, :].T.astype(BF16)
                cvt_ref[0, c, h_sl, :] = vt[c * CHUNK:(c + 1) * CHUNK, :].T.astype(BF16)
    og = jax.nn.sigmoid(proj(OFF_CO, C_WIDTH)) * _silu(proj(OFF_CZ, C_WIDTH))
    og_ref[0] = og.astype(og_ref.dtype)
    gr = _dot_nt(wgr_ref[...], xb)
    for c in range(tm // CHUNK):
        for ty in range(4):
            gr_ref[0, ty, c * GATE_ROWS:(c + 1) * GATE_ROWS, :] = (
                gr[ty * GATE_ROWS:(ty + 1) * GATE_ROWS, c * CHUNK:(c + 1) * CHUNK])


N_MIX_IN = 6
N_PROJ_IN = 10
SUB_TILES = 3
ATTN_SUB_TILES = 3


def _layer_kernel(*refs, split_src, mix, proj, final, n_ctx_tiles, n_sub, tm, tile_off):
    refs = list(refs)
    if split_src:
        n_ctx_src = min(n_sub, n_ctx_tiles)
        ctx_src = [refs.pop(0) for _ in range(n_ctx_src)]
        lat_src = [refs.pop(0) for _ in range(n_sub)]
    else:
        h_ref = refs.pop(0)
    if mix:
        gt_lat, gt_ctx, ya_ref, yb_ref, yc_ref, w_ref = (refs.pop(0) for _ in range(N_MIX_IN))
    if final:
        gf_ref = refs.pop(0)
    if proj:
        sc_lat, sc_ctx, sh_lat, sh_ctx = (refs.pop(0) for _ in range(4))
        gn_ref, w_in_ref, wgr_ref, pm_ref, ws_ref, bs_ref, gq_ref, gk_ref, cos_ref, sin_ref = (
            refs.pop(0) for _ in range(N_PROJ_IN))
    if mix:
        h_out = refs.pop(0)
    n_chunks = tm // CHUNK

    for j in range(n_sub):
        rs = slice(j * tm, (j + 1) * tm)
        is_ctx = (pl.program_id(1) + tile_off) * n_sub + j < n_ctx_tiles
        if split_src:
            x = lat_src[j][0]
            if j < len(ctx_src):
                x = jnp.where(is_ctx, ctx_src[j][0], x)
        else:
            x = h_ref[0, rs, :]
        if mix:
            acc = _dot(ya_ref[0, rs, :], w_ref[0:A_WIDTH, :])
            acc += _dot(yb_ref[0, rs, :], w_ref[A_WIDTH:A_WIDTH + B_WIDTH, :])
            acc += _dot(yc_ref[0, rs, :], w_ref[A_WIDTH + B_WIDTH:, :])
            x = x + jnp.where(is_ctx, gt_ctx[0], gt_lat[0]) * acc
        if final:
            ms = jnp.mean(x * x, axis=-1, keepdims=True)
            x = (x * lax.rsqrt(ms + EPS)) * gf_ref[...]
        if mix:
            h_out[0, rs, :] = x
        if proj:
            ya, q, k, vt, zb, cqt, ck, cvt, og, gr = refs
            cs = slice(j * n_chunks, (j + 1) * n_chunks)
            _proj_body(
                x, jnp.where(is_ctx, sc_ctx[0], sc_lat[0]), jnp.where(is_ctx, sh_ctx[0], sh_lat[0]),
                gn_ref, w_in_ref, wgr_ref, pm_ref, ws_ref, bs_ref, gq_ref, gk_ref,
                cos_ref.at[rs, :], sin_ref.at[rs, :],
                ya.at[:, rs, :], q.at[:, rs, :], k.at[:, rs, :], vt.at[:, :, rs], zb.at[:, rs, :],
                cqt.at[:, cs], ck.at[:, rs, :], cvt.at[:, cs], og.at[:, rs, :],
                gr.at[:, :, j * n_chunks * GATE_ROWS:(j + 1) * n_chunks * GATE_ROWS, :])


def _layer_call(src, mix, lw, mod, consts, g_final, n_ctx_tiles, tm):
    split_src = len(src) == 2
    final = lw is None
    b, d = src[0].shape[0], src[0].shape[2]
    t = src[0].shape[1] + src[1].shape[1] if split_src else src[0].shape[1]
    off = n_ctx_tiles if final else 0
    n_sub = SUB_TILES if (not final and (t // tm) % SUB_TILES == 0) else 1
    tb = tm * n_sub
    nt = (t // tm - off) // n_sub

    def tile3(w):
        return pl.BlockSpec((1, tb, w), lambda bi, i: (bi, i + off, 0))

    def whole(a):
        nd = a.ndim
        return pl.BlockSpec(a.shape, lambda bi, i: (0,) * nd)

    def mod_specs(a):
        n_mod = a.shape[0]
        return [pl.BlockSpec((1, 1, d), lambda bi, i: (bi, 0, 0)),
                pl.BlockSpec((1, 1, d), lambda bi, i: (n_mod - 1, 0, 0))]

    if split_src:
        ctx, x = src
        n_ctx_src = min(n_sub, n_ctx_tiles)
        ins = [ctx] * n_ctx_src + [x] * n_sub
        in_specs = [pl.BlockSpec((1, tm, d), functools.partial(
            lambda bi, i, j: (bi, jnp.minimum(i * n_sub + j, n_ctx_tiles - 1), 0), j=j))
            for j in range(n_ctx_src)]
        in_specs += [pl.BlockSpec((1, tm, d), functools.partial(
            lambda bi, i, j: (bi, jnp.maximum(i * n_sub + j - n_ctx_tiles, 0), 0), j=j))
            for j in range(n_sub)]
    else:
        ins, in_specs = [src[0]], [tile3(d)]
    out_shape, out_specs, aliases = [], [], {}
    if mix is not None:
        gt, ya, yb, yc, w_out = mix
        yb_spec = tile3(B_WIDTH) if yb.shape[1] == t else pl.BlockSpec((1, tb, B_WIDTH), lambda bi, i: (bi, i, 0))
        ins += [gt, gt, ya, yb, yc, w_out]
        in_specs += mod_specs(gt) + [tile3(A_WIDTH), yb_spec, tile3(C_WIDTH), whole(w_out)]
        if final:
            ins.append(g_final)
            in_specs.append(whole(g_final))
            out_shape.append(jax.ShapeDtypeStruct((b, t - off * tm, d), F32))
            out_specs.append(pl.BlockSpec((1, tb, d), lambda bi, i: (bi, i, 0)))
        else:
            out_shape.append(jax.ShapeDtypeStruct((b, t, d), F32))
            out_specs.append(tile3(d))
            if not split_src:
                aliases = {0: 0}
    if not final:
        sc, sh = mod
        ins += [sc, sc, sh, sh, lw["g_norm"], lw["w_main"], lw["w_gr"], consts["pm"], lw["w_s"],
                lw["b_s"], lw["g_q"], lw["g_k"], consts["cos"], consts["sin"]]
        in_specs += mod_specs(sc) + mod_specs(sh) + [
            whole(lw["g_norm"]), whole(lw["w_main"]), whole(lw["w_gr"]),
            whole(consts["pm"]), whole(lw["w_s"]), whole(lw["b_s"]), whole(lw["g_q"]),
            whole(lw["g_k"]),
            pl.BlockSpec((tb, LANES), lambda bi, i: (i, 0)),
            pl.BlockSpec((tb, LANES), lambda bi, i: (i, 0))]
        pad_w = C_HEADS * LANES
        out_shape += [
            jax.ShapeDtypeStruct((b, t, A_WIDTH), BF16),
            jax.ShapeDtypeStruct((b, t, B_HEADS * LANES), BF16),
            jax.ShapeDtypeStruct((b, t, B_KV_WIDTH), BF16),
            jax.ShapeDtypeStruct((b, B_KV_WIDTH, t), BF16),
            jax.ShapeDtypeStruct((b, t, B_WIDTH), GATE_DTYPE),
            jax.ShapeDtypeStruct((b, t // CHUNK, pad_w, CHUNK), BF16),
            jax.ShapeDtypeStruct((b, t, pad_w), BF16),
            jax.ShapeDtypeStruct((b, t // CHUNK, pad_w, CHUNK), BF16),
            jax.ShapeDtypeStruct((b, t, C_WIDTH), GATE_DTYPE),
            jax.ShapeDtypeStruct((b, 4, t // CHUNK * GATE_ROWS, CHUNK), F32),
        ]
        out_specs += [tile3(A_WIDTH), tile3(B_HEADS * LANES), tile3(B_KV_WIDTH),
                      pl.BlockSpec((1, B_KV_WIDTH, tb), lambda bi, i: (bi, 0, i)),
                      tile3(B_WIDTH),
                      pl.BlockSpec((1, tb // CHUNK, pad_w, CHUNK), lambda bi, i: (bi, i, 0, 0)),
                      tile3(pad_w),
                      pl.BlockSpec((1, tb // CHUNK, pad_w, CHUNK), lambda bi, i: (bi, i, 0, 0)),
                      tile3(C_WIDTH),
                      pl.BlockSpec((1, 4, tb // CHUNK * GATE_ROWS, CHUNK), lambda bi, i: (bi, 0, i, 0))]
    kern = functools.partial(_layer_kernel, split_src=split_src, mix=mix is not None, proj=not final,
                             final=final, n_ctx_tiles=n_ctx_tiles, n_sub=n_sub, tm=tm, tile_off=off)
    return pl.pallas_call(
        kern,
        grid=(b, nt),
        in_specs=in_specs,
        out_specs=out_specs,
        out_shape=out_shape,
        input_output_aliases=aliases,
        compiler_params=pltpu.CompilerParams(
            dimension_semantics=("arbitrary", "arbitrary"), vmem_limit_bytes=VMEM_LIMIT_BYTES),
        name="layer_tile",
    )(*ins)


def _attn_kernel(*refs, n_sub, n_ctx_tiles, tile_off, ctx_len, stabilise):
    q_refs = refs[0:n_sub]
    k_ref, vt_ref = refs[n_sub:n_sub + 2]
    zb_refs = refs[n_sub + 2:2 * n_sub + 2]
    y_ref = refs[2 * n_sub + 2]
    t_all = k_ref.shape[1]
    tq = q_refs[0].shape[1]

    def attend(sub, n_keys):
        q_ref, zb_ref = q_refs[sub], zb_refs[sub]
        rs = slice(sub * tq, (sub + 1) * tq)
        kk = k_ref[0, 0:n_keys, :]
        vt = vt_ref[0, :, 0:n_keys]
        for g in range(B_KV_HEADS):
            q_grp = jnp.concatenate(
                [q_ref[0, :, h * LANES:(h + 1) * LANES] for h in range(g * B_GROUP, (g + 1) * B_GROUP)],
                axis=0)
            s = _dot_nt(kk, q_grp)
            if stabilise:
                s = s - jnp.max(s, axis=0, keepdims=True)
            p = jnp.exp2(s)
            l = jnp.sum(p, axis=0, keepdims=True)
            o = _dot(vt, p.astype(BF16))[g * HEAD_DIM:(g + 1) * HEAD_DIM, :] / l
            for jj in range(B_GROUP // 2):
                pair = jnp.concatenate([o[:, (2 * jj) * tq:(2 * jj + 1) * tq],
                                        o[:, (2 * jj + 1) * tq:(2 * jj + 2) * tq]], axis=0).T
                j = g * (B_GROUP // 2) + jj
                sl = slice(j * LANES, (j + 1) * LANES)
                y_ref[0, rs, sl] = (pair * zb_ref[0, :, sl].astype(F32)).astype(BF16)

    for sub in range(n_sub):
        tile = pl.program_id(1) * n_sub + sub + tile_off
        if tile_off == 0 and sub < n_ctx_tiles:
            pl.when(tile < n_ctx_tiles)(functools.partial(attend, sub, ctx_len))
            pl.when(tile >= n_ctx_tiles)(functools.partial(attend, sub, t_all))
        else:
            attend(sub, t_all)


def _attn_call(q, k, vt, zb, *, n_ctx_tiles, tq, with_ctx, stabilise):
    b, t, _ = q.shape
    tile_off = 0 if with_ctx else n_ctx_tiles
    n_tiles = t // tq - tile_off
    n_sub = next(n for n in (ATTN_SUB_TILES, 2, 1) if n_tiles % n == 0)
    kern = functools.partial(_attn_kernel, n_sub=n_sub, n_ctx_tiles=n_ctx_tiles, tile_off=tile_off,
                             ctx_len=n_ctx_tiles * tq, stabilise=stabilise)

    def tile_spec(w, sub):
        return pl.BlockSpec((1, tq, w), lambda bi, i: (bi, i * n_sub + sub + tile_off, 0))

    return pl.pallas_call(
        kern,
        grid=(b, n_tiles // n_sub),
        in_specs=([tile_spec(B_HEADS * LANES, sub) for sub in range(n_sub)]
                  + [pl.BlockSpec((1, t, B_KV_WIDTH), lambda bi, i: (bi, 0, 0)),
                     pl.BlockSpec((1, B_KV_WIDTH, t), lambda bi, i: (bi, 0, 0))]
                  + [tile_spec(B_WIDTH, sub) for sub in range(n_sub)]),
        out_specs=pl.BlockSpec((1, tq * n_sub, B_WIDTH), lambda bi, i: (bi, i, 0)),
        out_shape=jax.ShapeDtypeStruct((b, t - tile_off * tq, B_WIDTH), BF16),
        compiler_params=pltpu.CompilerParams(
            dimension_semantics=("arbitrary", "arbitrary"), vmem_limit_bytes=VMEM_LIMIT_BYTES),
        name="attention",
    )(*([q] * n_sub + [k, vt] + [zb] * n_sub))


def _mlstm_kernel(cqt_ref, ck_ref, cvt_ref, gr_ref, og_ref, br_ref, gh_ref,
                  y_ref, hf_ref, hb_ref, st_ref, m_ref, gp_ref, sel_ref, madd_ref, *, n_ctx_chunks):
    nc = cqt_ref.shape[1]
    L = M_CHUNK
    n_units = 2 * C_HEADS
    W = n_units * L
    n_rows = nc * GATE_ROWS
    n_spread = 2 * GATE_ROWS
    row = lax.broadcasted_iota(jnp.int32, (L, L), 0)
    col = lax.broadcasted_iota(jnp.int32, (L, L), 1)
    tri_row = tuple(jnp.where(mk, 1.0, 0.0).astype(BF16) for mk in (row <= col, row >= col))
    eye_bf = jnp.where(row == col, 1.0, 0.0).astype(BF16)
    @pl.when(pl.program_id(0) == 0)
    def _():
        srow = lax.broadcasted_iota(jnp.int32, (L, W), 0)
        scol = lax.broadcasted_iota(jnp.int32, (L, W), 1)
        sel_ref[...] = jnp.where((srow < 3 * n_spread)
                                 & (((srow // GATE_ROWS) % 2) == (scol // (C_HEADS * L)))
                                 & ((srow % GATE_ROWS) == ((scol // L) % C_HEADS)), 1.0, 0.0).astype(BF16)
        upward = scol < C_HEADS * L
        hidden = (upward & (srow > (scol % L))) | (jnp.logical_not(upward) & (srow < (scol % L)))
        madd_ref[...] = jnp.where(hidden, -jnp.inf, 0.0).astype(F32)

    st_ref[...] = jnp.zeros(st_ref.shape, F32)
    m_ref[...] = jnp.zeros(m_ref.shape, F32)

    lane_r = _lane_iota((n_rows, L))
    li, b, bend = [], [], []
    for d in range(2):
        li.append((gr_ref[0, 2 * d] + br_ref[2 * d]) * LOG2E)
        lf = _log_sigmoid(gr_ref[0, 2 * d + 1] + br_ref[2 * d + 1]) * LOG2E
        lf_hi = lf.astype(BF16)
        lf_r1 = lf - lf_hi.astype(F32)
        lf_mid = lf_r1.astype(BF16)
        lf_lo = (lf_r1 - lf_mid.astype(F32)).astype(BF16)
        b.append(_dot(lf_hi, tri_row[d]) + _dot(lf_mid, tri_row[d]) + _dot(lf_lo, tri_row[d]))
        last = L - 1 if d == 0 else 0
        bend.append(jnp.broadcast_to(b[d][:, last:last + 1], (n_rows, L)))
    g = [li[d] - b[d] for d in range(2)]
    cm = list(g)
    sh = 1
    while sh < L:
        cm[0] = jnp.maximum(cm[0], jnp.where(lane_r >= sh, pltpu.roll(cm[0], sh, 1), -jnp.inf))
        cm[1] = jnp.maximum(cm[1], jnp.where(lane_r < L - sh, pltpu.roll(cm[1], L - sh, 1), -jnp.inf))
        sh *= 2
    for d in range(2):
        w_end = bend[d] - b[d] + li[d]
        mx = jnp.max(w_end, axis=1, keepdims=True)
        gp_ref[d, 0] = g[d]
        gp_ref[d, 1] = cm[d]
        gp_ref[d, 2] = -(b[d] + cm[d])
        gp_ref[d, 3] = jnp.exp2(w_end - mx)
        gp_ref[d, 4] = bend[d]
        gp_ref[d, 5] = jnp.broadcast_to(mx, (n_rows, L))

    def unit_lanes(per_dir):
        return jnp.concatenate([per_dir[d][h:h + 1, :] for d in range(2) for h in range(C_HEADS)], axis=1)

    def chunks_of(i):
        return (i, jnp.where(i < n_ctx_chunks, n_ctx_chunks - 1 - i, nc + n_ctx_chunks - 1 - i))

    def gate_rows(chunks):
        return tuple(pl.ds(pl.multiple_of(c * GATE_ROWS, GATE_ROWS), GATE_ROWS) for c in chunks)

    def step(i, carry):
        chunks = chunks_of(i)
        rows = tuple(pl.ds(pl.multiple_of(c * L, L), L) for c in chunks)
        grow = gate_rows(chunks)
        g, cm, nbc, wk0, bend, mx = (tuple(gp_ref[d, k, grow[d], :] for d in range(2)) for k in range(6))
        m_old = (m_ref[0], m_ref[1])

        decay, wk = [], []
        for d in range(2):
            m_new = jnp.maximum(bend[d] + m_old[d], mx[d])
            decay.append(jnp.exp2(bend[d] + m_old[d] - m_new))
            wk.append(wk0[d] * jnp.exp2(mx[d] - m_new))
            m_ref[d] = m_new

        r16 = jnp.concatenate([g[0], g[1]], axis=0)
        hi = r16.astype(BF16)
        r1 = r16 - hi.astype(F32)
        mid = r1.astype(BF16)
        lo = (r1 - mid.astype(F32)).astype(BF16)
        pieces = jnp.concatenate([hi, mid, lo, jnp.zeros((L - 3 * n_spread, L), BF16)], axis=0)
        pieces_t = _dot_nt(eye_bf, pieces).astype(BF16)
        g_all = _dot(pieces_t, sel_ref[...])

        e0 = jnp.exp2((g_all - unit_lanes(cm)) + madd_ref[...])

        k_d = tuple(ck_ref[0, rows[d], :] for d in range(2))
        qt_d = tuple(cqt_ref[0, chunks[d]] for d in range(2))
        vt_d = tuple(cvt_ref[0, chunks[d]] for d in range(2))
        units = [(d, h) for d in range(2) for h in range(C_HEADS)]
        st_old, qk, bq = [], [], []
        for u, (d, h) in enumerate(units):
            sl = slice(h * LANES, (h + 1) * LANES)
            st = st_ref[u]
            st_old.append(st)
            r = _dot(jnp.concatenate([k_d[d][:, sl], st.astype(BF16)], axis=0), qt_d[d][sl, :])
            qk.append(r[0:L])
            bq.append(r[L:2 * L])
        s0 = (jnp.concatenate(qk, axis=1) * e0).astype(BF16)

        acc, upd = [], []
        for u, (d, h) in enumerate(units):
            sl = slice(h * LANES, (h + 1) * LANES)
            vt = vt_d[d][sl, :]
            vw = (vt.astype(F32) * wk[d][h:h + 1, :]).astype(BF16)
            acc.append(_dot(vt, s0[:, u * L:(u + 1) * L]))
            upd.append(_dot(vw, k_d[d][:, sl]))

        f, inter, thr = [], [], []
        for d in range(2):
            diff = cm[d] - m_old[d]
            below = jnp.minimum(diff, 0.0)
            f.append(jnp.exp2(below))
            inter.append(jnp.exp2(jnp.minimum(-diff, 0.0)))
            thr.append(jnp.exp2(nbc[d] + below))
        na = unit_lanes(f) * jnp.concatenate(acc, axis=1) + unit_lanes(inter) * jnp.concatenate(bq, axis=1)
        den = na[HEAD_DIM:HEAD_DIM + 1, :]
        h_all = na[0:HEAD_DIM, :] * (1.0 / jnp.maximum(jnp.abs(den), unit_lanes(thr)))
        for u, (d, h) in enumerate(units):
            dst = hf_ref if d == 0 else hb_ref
            dst[chunks[d], h * HEAD_DIM:(h + 1) * HEAD_DIM, :] = h_all[:, u * L:(u + 1) * L]

        for u, (d, h) in enumerate(units):
            st_ref[u] = decay[d][h:h + 1, :] * st_old[u] + upd[u]
        return carry

    lax.fori_loop(0, nc, step, 0, unroll=6)

    def combine(c, carry):
        rows = pl.ds(pl.multiple_of(c * L, L), L)
        hs = hf_ref[c] + hb_ref[c]
        normed = []
        for hd in range(C_HEADS):
            blk = hs[hd * HEAD_DIM:(hd + 1) * HEAD_DIM, :]
            ms = jnp.sum(blk * blk, axis=0, keepdims=True) * (1.0 / HEAD_DIM)
            normed.append(blk * lax.rsqrt(ms + EPS))
        y = (jnp.concatenate(normed, axis=0) * gh_ref[...]).T
        y_ref[0, rows, :] = (og_ref[0, rows, :].astype(F32) * y).astype(BF16)
        return carry

    lax.fori_loop(0, nc, combine, 0, unroll=3)


def _mlstm_call(cqt, ck, cvt, gr, og, lw, n_ctx_chunks):
    b, t, pad_w = ck.shape

    def per_b(a):
        nd = a.ndim
        return pl.BlockSpec((1,) + a.shape[1:], lambda bi: (bi,) + (0,) * (nd - 1))

    def whole(a):
        nd = a.ndim
        return pl.BlockSpec(a.shape, lambda bi: (0,) * nd)

    kern = functools.partial(_mlstm_kernel, n_ctx_chunks=n_ctx_chunks)
    return pl.pallas_call(
        kern,
        grid=(b,),
        in_specs=[per_b(cqt), per_b(ck), per_b(cvt), per_b(gr), per_b(og),
                  whole(lw["bg_r"]), whole(lw["g_head"])],
        out_specs=pl.BlockSpec((1, t, C_WIDTH), lambda bi: (bi, 0, 0)),
        out_shape=jax.ShapeDtypeStruct((b, t, C_WIDTH), BF16),
        scratch_shapes=[
            pltpu.VMEM((t // M_CHUNK, C_WIDTH, M_CHUNK), F32),
            pltpu.VMEM((t // M_CHUNK, C_WIDTH, M_CHUNK), F32),
            pltpu.VMEM((2 * C_HEADS, LANES, LANES), F32),
            pltpu.VMEM((2, GATE_ROWS, LANES), F32),
            pltpu.VMEM((2, 6, t // M_CHUNK * GATE_ROWS, M_CHUNK), F32),
            pltpu.VMEM((M_CHUNK, 2 * C_HEADS * M_CHUNK), BF16),
            pltpu.VMEM((M_CHUNK, 2 * C_HEADS * M_CHUNK), F32),
        ],
        compiler_params=pltpu.CompilerParams(
            dimension_semantics=("arbitrary",), vmem_limit_bytes=VMEM_LIMIT_BYTES),
        name="mlstm",
    )(cqt, ck, cvt, gr, og, lw["bg_r"], lw["g_head"])


def _rope_tables(ctx_len, seq):
    pos = np.arange(seq)
    freqs = ROPE_THETA ** (-np.arange(ROPE_FREQS, dtype=np.float32) / ROPE_FREQS)
    ang_r = (pos // GRID_W).astype(np.float32)[:, None] * freqs
    ang_c = (pos % GRID_W).astype(np.float32)[:, None] * freqs
    cos_h = np.concatenate([np.cos(ang_r), np.cos(ang_r), np.cos(ang_c), np.cos(ang_c)], axis=1)
    sin_h = np.concatenate([-np.sin(ang_r), np.sin(ang_r), -np.sin(ang_c), np.sin(ang_c)], axis=1)
    cos = np.concatenate([np.ones((ctx_len, HEAD_DIM), np.float32), cos_h.astype(np.float32)], axis=0)
    sin = np.concatenate([np.zeros((ctx_len, HEAD_DIM), np.float32), sin_h.astype(np.float32)], axis=0)
    return jnp.asarray(np.tile(cos, (1, 2))), jnp.asarray(np.tile(sin, (1, 2)))


def _layer_weights(l, n_chunks, g_norm, w_in, w_s, b_s, g_q, g_k, b_gates, g_head, w_out):
    d = w_in.shape[1]
    w_l = w_in[l]
    w_g = w_l[:, OFF_CG:]
    w_gr = jnp.zeros((4, GATE_ROWS, d), F32).at[:, :C_HEADS, :].set(w_g.T.reshape(4, C_HEADS, d))
    bg_r = jnp.zeros((4, GATE_ROWS), F32).at[:, :C_HEADS].set(b_gates[l].reshape(4, C_HEADS))
    return {
        "g_norm": g_norm[l].reshape(1, d),
        "w_main": w_l[:, :OFF_CG].astype(BF16),
        "w_gr": w_gr.reshape(4 * GATE_ROWS, d).astype(BF16),
        "w_s": jnp.concatenate([w_s[l, 0::2], w_s[l, 1::2]], axis=2).astype(BF16),
        "b_s": jnp.repeat(b_s[l].T, HEAD_DIM, axis=1),
        "g_q": jnp.tile(g_q[l], B_HEADS).reshape(1, B_WIDTH),
        "g_k": jnp.tile(g_k[l], B_KV_HEADS).reshape(1, B_KV_WIDTH),
        "bg_r": jnp.broadcast_to(jnp.tile(bg_r, (1, n_chunks))[:, :, None],
                                 (4, n_chunks * GATE_ROWS, M_CHUNK)),
        "g_head": jnp.broadcast_to(g_head[l][:, None], (C_WIDTH, M_CHUNK)),
        "w_out": w_out[l].astype(BF16),
    }


def kernel(x, c, ctx, c_ctx, w_ada, b_ada, g_norm, w_in, w_s, b_s, g_q, g_k, b_gates, g_head, w_out,
           g_final):
    b, seq, d = x.shape
    ctx_len = ctx.shape[1]
    depth = w_ada.shape[0]
    tm = 256 if (ctx_len % 256 == 0 and seq % 256 == 0) else CHUNK
    n_ctx_tiles = ctx_len // tm
    n_ctx_chunks = ctx_len // M_CHUNK

    n_mod = b + 1
    rows = -(-n_mod // 8) * 8
    cc = jnp.zeros((rows, d), F32).at[:b].set(c).at[b].set(c_ctx)
    mods = _modulation(cc, w_ada, b_ada)[:, :n_mod]
    mods = mods.reshape(depth, n_mod, 3, 1, d)

    cos, sin = _rope_tables(ctx_len, seq)
    seg = np.kron(np.eye(B_HEADS, dtype=np.float32), np.full((HEAD_DIM, HEAD_DIM), 1.0 / HEAD_DIM, np.float32))
    consts = {"cos": cos, "sin": sin, "pm": jnp.asarray(seg, BF16)}
    gf = g_final.reshape(1, d)

    src = (ctx, x)
    mix = None
    for l in range(depth + 1):
        lw = mod = None
        if l < depth:
            lw = _layer_weights(l, (ctx_len + seq) // M_CHUNK, g_norm, w_in, w_s, b_s, g_q, g_k, b_gates,
                                g_head, w_out)
            sh, sc, gt = mods[l, :, 0], mods[l, :, 1], mods[l, :, 2]
            mod = (sc, sh)
        outs = _layer_call(src, mix, lw, mod, consts, gf, n_ctx_tiles, tm)
        if mix is not None:
            src = (outs[0],)
            outs = outs[1:]
        if l == depth:
            return src[0]
        (ya, q, k, vt, zb, cqt, ck, cvt, og, gr) = outs
        bound = (HEAD_DIM * ATTN_SCALE) * jnp.max(jnp.abs(g_q[l])) * jnp.max(jnp.abs(g_k[l]))
        attn = functools.partial(_attn_call, n_ctx_tiles=n_ctx_tiles, tq=tm, with_ctx=l < depth - 1)
        yb = lax.cond(bound <= SAFE_LOGIT_BOUND, functools.partial(attn, stabilise=False),
                      functools.partial(attn, stabilise=True), q, k, vt, zb)
        yc = _mlstm_call(cqt, ck, cvt, gr, og, lw, n_ctx_chunks)
        mix = (gt, ya, yb, yc, lw["w_out"])
```

```python
import functools

import jax
import jax.numpy as jnp
import numpy as np
from jax import lax
from jax.experimental import pallas as pl
from jax.experimental.pallas import tpu as pltpu

HEAD_DIM = 64
LANES = 128
GRID_W = 64
EPS = 1e-6
A_GROUPS = 4
A_WIDTH = A_GROUPS * HEAD_DIM
CHUNK = 128
B_HEADS = 8
B_KV_HEADS = 2
B_GROUP = B_HEADS // B_KV_HEADS
B_WIDTH = B_HEADS * HEAD_DIM
B_KV_WIDTH = B_KV_HEADS * HEAD_DIM
ATTN_SCALE = HEAD_DIM ** -0.5
ROPE_THETA = 10000.0
ROPE_FREQS = HEAD_DIM // 4
C_HEADS = 4
C_WIDTH = C_HEADS * HEAD_DIM
M_CHUNK = 128
N_GATES = 4 * C_HEADS
GATE_ROWS = 8

OFF_AU = 0
OFF_AV = OFF_AU + A_WIDTH
OFF_AZ = OFF_AV + A_WIDTH
OFF_BQ = OFF_AZ + A_WIDTH
OFF_BK = OFF_BQ + B_WIDTH
OFF_BV = OFF_BK + B_KV_WIDTH
OFF_BZ = OFF_BV + B_KV_WIDTH
OFF_CQ = OFF_BZ + B_WIDTH
OFF_CK = OFF_CQ + C_WIDTH
OFF_CV = OFF_CK + C_WIDTH
OFF_CO = OFF_CV + C_WIDTH
OFF_CZ = OFF_CO + C_WIDTH
OFF_CG = OFF_CZ + C_WIDTH

VMEM_LIMIT_BYTES = 56 * 1024 * 1024

F32 = jnp.float32
BF16 = jnp.bfloat16
GATE_DTYPE = BF16

LOG2E = float(np.log2(np.e))
QK_PRESCALE = ATTN_SCALE * LOG2E
SAFE_LOGIT_BOUND = 40.0


def _dot(a, b):
    return jnp.dot(a, b, preferred_element_type=F32)


def _dot_nt(a, b):
    return lax.dot_general(a, b, (((1,), (1,)), ((), ())), preferred_element_type=F32)


def _silu(x):
    return x * jax.nn.sigmoid(x)


def _gelu_tanh(x):
    c = np.float32(np.sqrt(2.0 / np.pi))
    return 0.5 * x * (1.0 + jnp.tanh(c * (x + np.float32(0.044715) * (x * x * x))))


def _log_sigmoid(x):
    return jnp.minimum(x, 0.0) - jnp.log1p(jnp.exp(-jnp.abs(x)))


def _seg_mean(x, pm_ref):
    w = x.shape[-1]
    pm = pm_ref[0:w, 0:w]
    return _dot(x.astype(BF16), pm)


def _lane_iota(shape):
    return lax.broadcasted_iota(jnp.int32, shape, len(shape) - 1)


def _mod_kernel(cc_ref, w_ref, b_ref, o_ref):
    s = _silu(cc_ref[...])
    o_ref[0] = _dot(s.astype(BF16), w_ref[0].astype(BF16)) + b_ref[0]


def _modulation(cc, w_ada, b_ada):
    depth, d, d3 = w_ada.shape
    rows = cc.shape[0]
    tn = d
    return pl.pallas_call(
        _mod_kernel,
        grid=(depth, d3 // tn),
        in_specs=[
            pl.BlockSpec((rows, d), lambda l, n: (0, 0)),
            pl.BlockSpec((1, d, tn), lambda l, n: (l, 0, n)),
            pl.BlockSpec((1, 1, tn), lambda l, n: (l, 0, n)),
        ],
        out_specs=pl.BlockSpec((1, rows, tn), lambda l, n: (l, 0, n)),
        out_shape=jax.ShapeDtypeStruct((depth, rows, d3), F32),
        compiler_params=pltpu.CompilerParams(
            dimension_semantics=("arbitrary", "arbitrary"), vmem_limit_bytes=VMEM_LIMIT_BYTES),
        name="modulation",
    )(cc, w_ada, b_ada.reshape(depth, 1, d3))


def _proj_body(x, sc, sh, gn_ref, w_ref, wgr_ref, pm_ref, ws_ref, bs_ref,
               gq_ref, gk_ref, cos_ref, sin_ref,
               ya_ref, q_ref, k_ref, vt_ref, zb_ref, cqt_ref, ck_ref, cvt_ref, og_ref, gr_ref):
    tm = x.shape[0]
    ms = jnp.mean(x * x, axis=-1, keepdims=True)
    xn = (x * lax.rsqrt(ms + EPS)) * (gn_ref[...] * (1.0 + sc)) + sh
    xb = xn.astype(BF16)

    mixers = ((OFF_AU, OFF_BQ), (OFF_BQ, OFF_CQ), (OFF_CQ, OFF_CG))
    wide = {}

    def proj(off, width):
        for lo, hi in mixers:
            if lo <= off < hi:
                if lo not in wide:
                    wide[lo] = _dot(xb, w_ref[:, lo:hi])
                return wide[lo][:, off - lo:off - lo + width]
        raise ValueError(off)

    lane = _lane_iota((tm, LANES))
    lo_half = lane < HEAD_DIM

    gu = _gelu_tanh(proj(OFF_AU, A_WIDTH))
    gv = _gelu_tanh(proj(OFF_AV, A_WIDTH))
    mu = _seg_mean(gv, pm_ref)
    dv = gv - mu
    var = _seg_mean(dv * dv, pm_ref)
    vn = (dv * lax.rsqrt(var + EPS)).astype(BF16)
    sz = _silu(proj(OFF_AZ, A_WIDTH))
    lo_half_chunk = _lane_iota((CHUNK, LANES)) < HEAD_DIM
    for c in range(tm // CHUNK):
        r0, r1 = c * CHUNK, (c + 1) * CHUNK
        for j in range(A_WIDTH // LANES):
            c0, c1 = j * LANES, (j + 1) * LANES
            vs = vn[r0:r1, c0:c1]
            stacked = jnp.concatenate([jnp.where(lo_half_chunk, vs, 0.0), jnp.where(lo_half_chunk, 0.0, vs)],
                                      axis=0).astype(BF16)
            sv = _dot(ws_ref[j], stacked) + bs_ref[:, c0:c1]
            ya_ref[0, r0:r1, c0:c1] = (gu[r0:r1, c0:c1] * sv * sz[r0:r1, c0:c1]).astype(BF16)

    cos = cos_ref[...]
    sin = sin_ref[...]
    bit4 = (lane & ROPE_FREQS) != 0

    def rope(t):
        partner = jnp.where(bit4, pltpu.roll(t, ROPE_FREQS, 1), pltpu.roll(t, LANES - ROPE_FREQS, 1))
        return t * cos + partner * sin

    half_w = B_WIDTH // 2
    q_halves = []
    for off in (0, half_w):
        qh = proj(OFF_BQ + off, half_w)
        q_halves.append(qh * lax.rsqrt(_seg_mean(qh * qh, pm_ref) + EPS) * gq_ref[:, off:off + half_w])
    for j in range(B_WIDTH // LANES):
        jj = j % (half_w // LANES)
        qs = rope(q_halves[j // (half_w // LANES)][:, jj * LANES:(jj + 1) * LANES]) * QK_PRESCALE
        sw = pltpu.roll(qs, HEAD_DIM, 1)
        if (2 * j) // B_GROUP == 0:
            h_even = jnp.where(lo_half, qs, 0.0)
            h_odd = jnp.where(lo_half, sw, 0.0)
        else:
            h_even = jnp.where(lo_half, 0.0, sw)
            h_odd = jnp.where(lo_half, 0.0, qs)
        q_ref[0, :, (2 * j) * LANES:(2 * j + 1) * LANES] = h_even.astype(BF16)
        q_ref[0, :, (2 * j + 1) * LANES:(2 * j + 2) * LANES] = h_odd.astype(BF16)

    k = proj(OFF_BK, B_KV_WIDTH)
    kn = k * lax.rsqrt(_seg_mean(k * k, pm_ref) + EPS) * gk_ref[...]
    k_ref[0] = rope(kn).astype(BF16)
    vt_ref[0] = proj(OFF_BV, B_KV_WIDTH).T.astype(BF16)
    zb_ref[0] = _silu(proj(OFF_BZ, B_WIDTH)).astype(zb_ref.dtype)

    cq = proj(OFF_CQ, C_WIDTH)
    ck = proj(OFF_CK, C_WIDTH) * (HEAD_DIM ** -0.5)
    cv = proj(OFF_CV, C_WIDTH)
    for j in range(C_WIDTH // LANES):
        sl = slice(j * LANES, (j + 1) * LANES)
        for par in range(2):
            hd = 2 * j + par
            h_sl = slice(hd * LANES, (hd + 1) * LANES)

            def head_tile(src, pad):
                s = src[:, sl] if par == 0 else pltpu.roll(src[:, sl], HEAD_DIM, 1)
                return jnp.where(lo_half, s, pad)

            ck_ref[0, :, h_sl] = head_tile(ck, 0.0).astype(BF16)
            qt = head_tile(cq, 0.0)
            vt = head_tile(cv, 1.0)
            for c in range(tm // CHUNK):
                cqt_ref[0, c, h_sl, :] = qt[c * CHUNK:(c + 1) * CHUNK, :].T.astype(BF16)
                cvt_ref[0, c, h_sl, :] = vt[c * CHUNK:(c + 1) * CHUNK, :].T.astype(BF16)
    og = jax.nn.sigmoid(proj(OFF_CO, C_WIDTH)) * _silu(proj(OFF_CZ, C_WIDTH))
    og_ref[0] = og.astype(og_ref.dtype)
    gr = _dot_nt(wgr_ref[...], xb)
    for c in range(tm // CHUNK):
        for ty in range(4):
            gr_ref[0, ty, c * GATE_ROWS:(c + 1) * GATE_ROWS, :] = (
                gr[ty * GATE_ROWS:(ty + 1) * GATE_ROWS, c * CHUNK:(c + 1) * CHUNK])


N_MIX_IN = 6
N_PROJ_IN = 10
SUB_TILES = 3
ATTN_SUB_TILES = (4, 3, 2, 1)
MLSTM_STEP_UNROLL = 9
MLSTM_COMBINE_UNROLL = 6


def _layer_kernel(*refs, split_src, mix, proj, final, n_ctx_tiles, n_sub, tm, tile_off):
    refs = list(refs)
    if split_src:
        n_ctx_src = min(n_sub, n_ctx_tiles)
        ctx_src = [refs.pop(0) for _ in range(n_ctx_src)]
        lat_src = [refs.pop(0) for _ in range(n_sub)]
    else:
        h_ref = refs.pop(0)
    if mix:
        gt_lat, gt_ctx, ya_ref, yb_ref, yc_ref, w_ref = (refs.pop(0) for _ in range(N_MIX_IN))
    if final:
        gf_ref = refs.pop(0)
    if proj:
        sc_lat, sc_ctx, sh_lat, sh_ctx = (refs.pop(0) for _ in range(4))
        gn_ref, w_in_ref, wgr_ref, pm_ref, ws_ref, bs_ref, gq_ref, gk_ref, cos_ref, sin_ref = (
            refs.pop(0) for _ in range(N_PROJ_IN))
    if mix:
        h_out = refs.pop(0)
    n_chunks = tm // CHUNK

    for j in range(n_sub):
        rs = slice(j * tm, (j + 1) * tm)
        is_ctx = (pl.program_id(1) + tile_off) * n_sub + j < n_ctx_tiles
        if split_src:
            x = lat_src[j][0]
            if j < len(ctx_src):
                x = jnp.where(is_ctx, ctx_src[j][0], x)
        else:
            x = h_ref[0, rs, :]
        if mix:
            acc = _dot(ya_ref[0, rs, :], w_ref[0:A_WIDTH, :])
            acc += _dot(yb_ref[0, rs, :], w_ref[A_WIDTH:A_WIDTH + B_WIDTH, :])
            acc += _dot(yc_ref[0, rs, :], w_ref[A_WIDTH + B_WIDTH:, :])
            x = x + jnp.where(is_ctx, gt_ctx[0], gt_lat[0]) * acc
        if final:
            ms = jnp.mean(x * x, axis=-1, keepdims=True)
            x = (x * lax.rsqrt(ms + EPS)) * gf_ref[...]
        if mix:
            h_out[0, rs, :] = x
        if proj:
            ya, q, k, vt, zb, cqt, ck, cvt, og, gr = refs
            cs = slice(j * n_chunks, (j + 1) * n_chunks)
            _proj_body(
                x, jnp.where(is_ctx, sc_ctx[0], sc_lat[0]), jnp.where(is_ctx, sh_ctx[0], sh_lat[0]),
                gn_ref, w_in_ref, wgr_ref, pm_ref, ws_ref, bs_ref, gq_ref, gk_ref,
                cos_ref.at[rs, :], sin_ref.at[rs, :],
                ya.at[:, rs, :], q.at[:, rs, :], k.at[:, rs, :], vt.at[:, :, rs], zb.at[:, rs, :],
                cqt.at[:, cs], ck.at[:, rs, :], cvt.at[:, cs], og.at[:, rs, :],
                gr.at[:, :, j * n_chunks * GATE_ROWS:(j + 1) * n_chunks * GATE_ROWS, :])


def _layer_call(src, mix, lw, mod, consts, g_final, n_ctx_tiles, tm):
    split_src = len(src) == 2
    final = lw is None
    b, d = src[0].shape[0], src[0].shape[2]
    t = src[0].shape[1] + src[1].shape[1] if split_src else src[0].shape[1]
    off = n_ctx_tiles if final else 0
    n_sub = SUB_TILES if (not final and (t // tm) % SUB_TILES == 0) else 1
    tb = tm * n_sub
    nt = (t // tm - off) // n_sub

    def tile3(w):
        return pl.BlockSpec((1, tb, w), lambda bi, i: (bi, i + off, 0))

    def whole(a):
        nd = a.ndim
        return pl.BlockSpec(a.shape, lambda bi, i: (0,) * nd)

    def mod_specs(a):
        n_mod = a.shape[0]
        return [pl.BlockSpec((1, 1, d), lambda bi, i: (bi, 0, 0)),
                pl.BlockSpec((1, 1, d), lambda bi, i: (n_mod - 1, 0, 0))]

    if split_src:
        ctx, x = src
        n_ctx_src = min(n_sub, n_ctx_tiles)
        ins = [ctx] * n_ctx_src + [x] * n_sub
        in_specs = [pl.BlockSpec((1, tm, d), functools.partial(
            lambda bi, i, j: (bi, jnp.minimum(i * n_sub + j, n_ctx_tiles - 1), 0), j=j))
            for j in range(n_ctx_src)]
        in_specs += [pl.BlockSpec((1, tm, d), functools.partial(
            lambda bi, i, j: (bi, jnp.maximum(i * n_sub + j - n_ctx_tiles, 0), 0), j=j))
            for j in range(n_sub)]
    else:
        ins, in_specs = [src[0]], [tile3(d)]
    out_shape, out_specs, aliases = [], [], {}
    if mix is not None:
        gt, ya, yb, yc, w_out = mix
        yb_spec = tile3(B_WIDTH) if yb.shape[1] == t else pl.BlockSpec((1, tb, B_WIDTH), lambda bi, i: (bi, i, 0))
        ins += [gt, gt, ya, yb, yc, w_out]
        in_specs += mod_specs(gt) + [tile3(A_WIDTH), yb_spec, tile3(C_WIDTH), whole(w_out)]
        if final:
            ins.append(g_final)
            in_specs.append(whole(g_final))
            out_shape.append(jax.ShapeDtypeStruct((b, t - off * tm, d), F32))
            out_specs.append(pl.BlockSpec((1, tb, d), lambda bi, i: (bi, i, 0)))
        else:
            out_shape.append(jax.ShapeDtypeStruct((b, t, d), F32))
            out_specs.append(tile3(d))
            if not split_src:
                aliases = {0: 0}
    if not final:
        sc, sh = mod
        ins += [sc, sc, sh, sh, lw["g_norm"], lw["w_main"], lw["w_gr"], consts["pm"], lw["w_s"],
                lw["b_s"], lw["g_q"], lw["g_k"], consts["cos"], consts["sin"]]
        in_specs += mod_specs(sc) + mod_specs(sh) + [
            whole(lw["g_norm"]), whole(lw["w_main"]), whole(lw["w_gr"]),
            whole(consts["pm"]), whole(lw["w_s"]), whole(lw["b_s"]), whole(lw["g_q"]),
            whole(lw["g_k"]),
            pl.BlockSpec((tb, LANES), lambda bi, i: (i, 0)),
            pl.BlockSpec((tb, LANES), lambda bi, i: (i, 0))]
        pad_w = C_HEADS * LANES
        out_shape += [
            jax.ShapeDtypeStruct((b, t, A_WIDTH), BF16),
            jax.ShapeDtypeStruct((b, t, B_HEADS * LANES), BF16),
            jax.ShapeDtypeStruct((b, t, B_KV_WIDTH), BF16),
            jax.ShapeDtypeStruct((b, B_KV_WIDTH, t), BF16),
            jax.ShapeDtypeStruct((b, t, B_WIDTH), GATE_DTYPE),
            jax.ShapeDtypeStruct((b, t // CHUNK, pad_w, CHUNK), BF16),
            jax.ShapeDtypeStruct((b, t, pad_w), BF16),
            jax.ShapeDtypeStruct((b, t // CHUNK, pad_w, CHUNK), BF16),
            jax.ShapeDtypeStruct((b, t, C_WIDTH), GATE_DTYPE),
            jax.ShapeDtypeStruct((b, 4, t // CHUNK * GATE_ROWS, CHUNK), F32),
        ]
        out_specs += [tile3(A_WIDTH), tile3(B_HEADS * LANES), tile3(B_KV_WIDTH),
                      pl.BlockSpec((1, B_KV_WIDTH, tb), lambda bi, i: (bi, 0, i)),
                      tile3(B_WIDTH),
                      pl.BlockSpec((1, tb // CHUNK, pad_w, CHUNK), lambda bi, i: (bi, i, 0, 0)),
                      tile3(pad_w),
                      pl.BlockSpec((1, tb // CHUNK, pad_w, CHUNK), lambda bi, i: (bi, i, 0, 0)),
                      tile3(C_WIDTH),
                      pl.BlockSpec((1, 4, tb // CHUNK * GATE_ROWS, CHUNK), lambda bi, i: (bi, 0, i, 0))]
    kern = functools.partial(_layer_kernel, split_src=split_src, mix=mix is not None, proj=not final,
                             final=final, n_ctx_tiles=n_ctx_tiles, n_sub=n_sub, tm=tm, tile_off=off)
    return pl.pallas_call(
        kern,
        grid=(b, nt),
        in_specs=in_specs,
        out_specs=out_specs,
        out_shape=out_shape,
        input_output_aliases=aliases,
        compiler_params=pltpu.CompilerParams(
            dimension_semantics=("arbitrary", "arbitrary"), vmem_limit_bytes=VMEM_LIMIT_BYTES),
        name="layer_tile",
    )(*ins)


def _attn_kernel(*refs, n_sub, n_ctx_tiles, tile_off, ctx_len, stabilise):
    q_refs = refs[0:n_sub]
    k_ref, vt_ref = refs[n_sub:n_sub + 2]
    zb_refs = refs[n_sub + 2:2 * n_sub + 2]
    y_ref = refs[2 * n_sub + 2]
    t_all = k_ref.shape[1]
    tq = q_refs[0].shape[1]

    def attend(sub, n_keys):
        q_ref, zb_ref = q_refs[sub], zb_refs[sub]
        rs = slice(sub * tq, (sub + 1) * tq)
        kk = k_ref[0, 0:n_keys, :]
        vt = vt_ref[0, :, 0:n_keys]
        for g in range(B_KV_HEADS):
            q_grp = jnp.concatenate(
                [q_ref[0, :, h * LANES:(h + 1) * LANES] for h in range(g * B_GROUP, (g + 1) * B_GROUP)],
                axis=0)
            s = _dot_nt(kk, q_grp)
            if stabilise:
                s = s - jnp.max(s, axis=0, keepdims=True)
            p = jnp.exp2(s)
            l = jnp.sum(p, axis=0, keepdims=True)
            o = _dot(vt, p.astype(BF16))[g * HEAD_DIM:(g + 1) * HEAD_DIM, :] / l
            for jj in range(B_GROUP // 2):
                pair = jnp.concatenate([o[:, (2 * jj) * tq:(2 * jj + 1) * tq],
                                        o[:, (2 * jj + 1) * tq:(2 * jj + 2) * tq]], axis=0).T
                j = g * (B_GROUP // 2) + jj
                sl = slice(j * LANES, (j + 1) * LANES)
                y_ref[0, rs, sl] = (pair * zb_ref[0, :, sl].astype(F32)).astype(BF16)

    for sub in range(n_sub):
        tile = pl.program_id(1) * n_sub + sub + tile_off
        if tile_off == 0 and sub < n_ctx_tiles:
            pl.when(tile < n_ctx_tiles)(functools.partial(attend, sub, ctx_len))
            pl.when(tile >= n_ctx_tiles)(functools.partial(attend, sub, t_all))
        else:
            attend(sub, t_all)


def _attn_call(q, k, vt, zb, *, n_ctx_tiles, tq, with_ctx, stabilise):
    b, t, _ = q.shape
    tile_off = 0 if with_ctx else n_ctx_tiles
    n_tiles = t // tq - tile_off
    n_sub = next(n for n in ATTN_SUB_TILES if n_tiles % n == 0)
    kern = functools.partial(_attn_kernel, n_sub=n_sub, n_ctx_tiles=n_ctx_tiles, tile_off=tile_off,
                             ctx_len=n_ctx_tiles * tq, stabilise=stabilise)

    def tile_spec(w, sub):
        return pl.BlockSpec((1, tq, w), lambda bi, i: (bi, i * n_sub + sub + tile_off, 0))

    return pl.pallas_call(
        kern,
        grid=(b, n_tiles // n_sub),
        in_specs=([tile_spec(B_HEADS * LANES, sub) for sub in range(n_sub)]
                  + [pl.BlockSpec((1, t, B_KV_WIDTH), lambda bi, i: (bi, 0, 0)),
                     pl.BlockSpec((1, B_KV_WIDTH, t), lambda bi, i: (bi, 0, 0))]
                  + [tile_spec(B_WIDTH, sub) for sub in range(n_sub)]),
        out_specs=pl.BlockSpec((1, tq * n_sub, B_WIDTH), lambda bi, i: (bi, i, 0)),
        out_shape=jax.ShapeDtypeStruct((b, t - tile_off * tq, B_WIDTH), BF16),
        compiler_params=pltpu.CompilerParams(
            dimension_semantics=("arbitrary", "arbitrary"), vmem_limit_bytes=VMEM_LIMIT_BYTES),
        name="attention",
    )(*([q] * n_sub + [k, vt] + [zb] * n_sub))


def _mlstm_kernel(cqt_ref, ck_ref, cvt_ref, gr_ref, og_ref, br_ref, gh_ref,
                  y_ref, hf_ref, hb_ref, st_ref, m_ref, gp_ref, sel_ref, madd_ref, *, n_ctx_chunks):
    nc = cqt_ref.shape[1]
    L = M_CHUNK
    n_units = 2 * C_HEADS
    W = n_units * L
    n_rows = nc * GATE_ROWS
    n_spread = 2 * GATE_ROWS
    row = lax.broadcasted_iota(jnp.int32, (L, L), 0)
    col = lax.broadcasted_iota(jnp.int32, (L, L), 1)
    tri_row = tuple(jnp.where(mk, 1.0, 0.0).astype(BF16) for mk in (row <= col, row >= col))
    eye_bf = jnp.where(row == col, 1.0, 0.0).astype(BF16)
    @pl.when(pl.program_id(0) == 0)
    def _():
        srow = lax.broadcasted_iota(jnp.int32, (L, W), 0)
        scol = lax.broadcasted_iota(jnp.int32, (L, W), 1)
        sel_ref[...] = jnp.where((srow < 3 * n_spread)
                                 & (((srow // GATE_ROWS) % 2) == (scol // (C_HEADS * L)))
                                 & ((srow % GATE_ROWS) == ((scol // L) % C_HEADS)), 1.0, 0.0).astype(BF16)
        upward = scol < C_HEADS * L
        hidden = (upward & (srow > (scol % L))) | (jnp.logical_not(upward) & (srow < (scol % L)))
        madd_ref[...] = jnp.where(hidden, -jnp.inf, 0.0).astype(F32)

    st_ref[...] = jnp.zeros(st_ref.shape, F32)
    m_ref[...] = jnp.zeros(m_ref.shape, F32)

    lane_r = _lane_iota((n_rows, L))
    li, b, bend = [], [], []
    for d in range(2):
        li.append((gr_ref[0, 2 * d] + br_ref[2 * d]) * LOG2E)
        lf = _log_sigmoid(gr_ref[0, 2 * d + 1] + br_ref[2 * d + 1]) * LOG2E
        lf_hi = lf.astype(BF16)
        lf_r1 = lf - lf_hi.astype(F32)
        lf_mid = lf_r1.astype(BF16)
        lf_lo = (lf_r1 - lf_mid.astype(F32)).astype(BF16)
        b.append(_dot(lf_hi, tri_row[d]) + _dot(lf_mid, tri_row[d]) + _dot(lf_lo, tri_row[d]))
        last = L - 1 if d == 0 else 0
        bend.append(jnp.broadcast_to(b[d][:, last:last + 1], (n_rows, L)))
    g = [li[d] - b[d] for d in range(2)]
    cm = list(g)
    sh = 1
    while sh < L:
        cm[0] = jnp.maximum(cm[0], jnp.where(lane_r >= sh, pltpu.roll(cm[0], sh, 1), -jnp.inf))
        cm[1] = jnp.maximum(cm[1], jnp.where(lane_r < L - sh, pltpu.roll(cm[1], L - sh, 1), -jnp.inf))
        sh *= 2
    for d in range(2):
        w_end = bend[d] - b[d] + li[d]
        mx = jnp.max(w_end, axis=1, keepdims=True)
        gp_ref[d, 0] = g[d]
        gp_ref[d, 1] = cm[d]
        gp_ref[d, 2] = -(b[d] + cm[d])
        gp_ref[d, 3] = jnp.exp2(w_end - mx)
        gp_ref[d, 4] = bend[d]
        gp_ref[d, 5] = jnp.broadcast_to(mx, (n_rows, L))

    def unit_lanes(per_dir):
        return jnp.concatenate([per_dir[d][h:h + 1, :] for d in range(2) for h in range(C_HEADS)], axis=1)

    def chunks_of(i):
        return (i, jnp.where(i < n_ctx_chunks, n_ctx_chunks - 1 - i, nc + n_ctx_chunks - 1 - i))

    def gate_rows(chunks):
        return tuple(pl.ds(pl.multiple_of(c * GATE_ROWS, GATE_ROWS), GATE_ROWS) for c in chunks)

    def step(i, carry):
        chunks = chunks_of(i)
        rows = tuple(pl.ds(pl.multiple_of(c * L, L), L) for c in chunks)
        grow = gate_rows(chunks)
        g, cm, nbc, wk0, bend, mx = (tuple(gp_ref[d, k, grow[d], :] for d in range(2)) for k in range(6))
        m_old = (m_ref[0], m_ref[1])

        decay, wk = [], []
        for d in range(2):
            m_new = jnp.maximum(bend[d] + m_old[d], mx[d])
            decay.append(jnp.exp2(bend[d] + m_old[d] - m_new))
            wk.append(wk0[d] * jnp.exp2(mx[d] - m_new))
            m_ref[d] = m_new

        r16 = jnp.concatenate([g[0], g[1]], axis=0)
        hi = r16.astype(BF16)
        r1 = r16 - hi.astype(F32)
        mid = r1.astype(BF16)
        lo = (r1 - mid.astype(F32)).astype(BF16)
        pieces = jnp.concatenate([hi, mid, lo, jnp.zeros((L - 3 * n_spread, L), BF16)], axis=0)
        pieces_t = _dot_nt(eye_bf, pieces).astype(BF16)
        g_all = _dot(pieces_t, sel_ref[...])

        e0 = jnp.exp2((g_all - unit_lanes(cm)) + madd_ref[...])

        k_d = tuple(ck_ref[0, rows[d], :] for d in range(2))
        qt_d = tuple(cqt_ref[0, chunks[d]] for d in range(2))
        vt_d = tuple(cvt_ref[0, chunks[d]] for d in range(2))
        units = [(d, h) for d in range(2) for h in range(C_HEADS)]
        st_old, qk, bq = [], [], []
        for u, (d, h) in enumerate(units):
            sl = slice(h * LANES, (h + 1) * LANES)
            st = st_ref[u]
            st_old.append(st)
            r = _dot(jnp.concatenate([k_d[d][:, sl], st.astype(BF16)], axis=0), qt_d[d][sl, :])
            qk.append(r[0:L])
            bq.append(r[L:2 * L])
        s0 = (jnp.concatenate(qk, axis=1) * e0).astype(BF16)

        acc, upd = [], []
        for u, (d, h) in enumerate(units):
            sl = slice(h * LANES, (h + 1) * LANES)
            vt = vt_d[d][sl, :]
            vw = (vt.astype(F32) * wk[d][h:h + 1, :]).astype(BF16)
            acc.append(_dot(vt, s0[:, u * L:(u + 1) * L]))
            upd.append(_dot(vw, k_d[d][:, sl]))

        f, inter, thr = [], [], []
        for d in range(2):
            diff = cm[d] - m_old[d]
            below = jnp.minimum(diff, 0.0)
            f.append(jnp.exp2(below))
            inter.append(jnp.exp2(jnp.minimum(-diff, 0.0)))
            thr.append(jnp.exp2(nbc[d] + below))
        na = unit_lanes(f) * jnp.concatenate(acc, axis=1) + unit_lanes(inter) * jnp.concatenate(bq, axis=1)
        den = na[HEAD_DIM:HEAD_DIM + 1, :]
        h_all = na[0:HEAD_DIM, :] * (1.0 / jnp.maximum(jnp.abs(den), unit_lanes(thr)))
        for u, (d, h) in enumerate(units):
            dst = hf_ref if d == 0 else hb_ref
            dst[chunks[d], h * HEAD_DIM:(h + 1) * HEAD_DIM, :] = h_all[:, u * L:(u + 1) * L]

        for u, (d, h) in enumerate(units):
            st_ref[u] = decay[d][h:h + 1, :] * st_old[u] + upd[u]
        return carry

    lax.fori_loop(0, nc, step, 0, unroll=MLSTM_STEP_UNROLL)

    def combine(c, carry):
        rows = pl.ds(pl.multiple_of(c * L, L), L)
        hs = hf_ref[c] + hb_ref[c]
        normed = []
        for hd in range(C_HEADS):
            blk = hs[hd * HEAD_DIM:(hd + 1) * HEAD_DIM, :]
            ms = jnp.sum(blk * blk, axis=0, keepdims=True) * (1.0 / HEAD_DIM)
            normed.append(blk * lax.rsqrt(ms + EPS))
        y = (jnp.concatenate(normed, axis=0) * gh_ref[...]).T
        y_ref[0, rows, :] = (og_ref[0, rows, :].astype(F32) * y).astype(BF16)
        return carry

    lax.fori_loop(0, nc, combine, 0, unroll=MLSTM_COMBINE_UNROLL)


def _mlstm_call(cqt, ck, cvt, gr, og, lw, n_ctx_chunks):
    b, t, pad_w = ck.shape

    def per_b(a):
        nd = a.ndim
        return pl.BlockSpec((1,) + a.shape[1:], lambda bi: (bi,) + (0,) * (nd - 1))

    def whole(a):
        nd = a.ndim
        return pl.BlockSpec(a.shape, lambda bi: (0,) * nd)

    kern = functools.partial(_mlstm_kernel, n_ctx_chunks=n_ctx_chunks)
    return pl.pallas_call(
        kern,
        grid=(b,),
        in_specs=[per_b(cqt), per_b(ck), per_b(cvt), per_b(gr), per_b(og),
                  whole(lw["bg_r"]), whole(lw["g_head"])],
        out_specs=pl.BlockSpec((1, t, C_WIDTH), lambda bi: (bi, 0, 0)),
        out_shape=jax.ShapeDtypeStruct((b, t, C_WIDTH), BF16),
        scratch_shapes=[
            pltpu.VMEM((t // M_CHUNK, C_WIDTH, M_CHUNK), F32),
            pltpu.VMEM((t // M_CHUNK, C_WIDTH, M_CHUNK), F32),
            pltpu.VMEM((2 * C_HEADS, LANES, LANES), F32),
            pltpu.VMEM((2, GATE_ROWS, LANES), F32),
            pltpu.VMEM((2, 6, t // M_CHUNK * GATE_ROWS, M_CHUNK), F32),
            pltpu.VMEM((M_CHUNK, 2 * C_HEADS * M_CHUNK), BF16),
            pltpu.VMEM((M_CHUNK, 2 * C_HEADS * M_CHUNK), F32),
        ],
        compiler_params=pltpu.CompilerParams(
            dimension_semantics=("arbitrary",), vmem_limit_bytes=VMEM_LIMIT_BYTES),
        name="mlstm",
    )(cqt, ck, cvt, gr, og, lw["bg_r"], lw["g_head"])


def _rope_tables(ctx_len, seq):
    pos = np.arange(seq)
    freqs = ROPE_THETA ** (-np.arange(ROPE_FREQS, dtype=np.float32) / ROPE_FREQS)
    ang_r = (pos // GRID_W).astype(np.float32)[:, None] * freqs
    ang_c = (pos % GRID_W).astype(np.float32)[:, None] * freqs
    cos_h = np.concatenate([np.cos(ang_r), np.cos(ang_r), np.cos(ang_c), np.cos(ang_c)], axis=1)
    sin_h = np.concatenate([-np.sin(ang_r), np.sin(ang_r), -np.sin(ang_c), np.sin(ang_c)], axis=1)
    cos = np.concatenate([np.ones((ctx_len, HEAD_DIM), np.float32), cos_h.astype(np.float32)], axis=0)
    sin = np.concatenate([np.zeros((ctx_len, HEAD_DIM), np.float32), sin_h.astype(np.float32)], axis=0)
    return jnp.asarray(np.tile(cos, (1, 2))), jnp.asarray(np.tile(sin, (1, 2)))


def _layer_weights(l, n_chunks, g_norm, w_in, w_s, b_s, g_q, g_k, b_gates, g_head, w_out):
    d = w_in.shape[1]
    w_l = w_in[l]
    w_g = w_l[:, OFF_CG:]
    w_gr = jnp.zeros((4, GATE_ROWS, d), F32).at[:, :C_HEADS, :].set(w_g.T.reshape(4, C_HEADS, d))
    bg_r = jnp.zeros((4, GATE_ROWS), F32).at[:, :C_HEADS].set(b_gates[l].reshape(4, C_HEADS))
    return {
        "g_norm": g_norm[l].reshape(1, d),
        "w_main": w_l[:, :OFF_CG].astype(BF16),
        "w_gr": w_gr.reshape(4 * GATE_ROWS, d).astype(BF16),
        "w_s": jnp.concatenate([w_s[l, 0::2], w_s[l, 1::2]], axis=2).astype(BF16),
        "b_s": jnp.repeat(b_s[l].T, HEAD_DIM, axis=1),
        "g_q": jnp.tile(g_q[l], B_HEADS).reshape(1, B_WIDTH),
        "g_k": jnp.tile(g_k[l], B_KV_HEADS).reshape(1, B_KV_WIDTH),
        "bg_r": jnp.broadcast_to(jnp.tile(bg_r, (1, n_chunks))[:, :, None],
                                 (4, n_chunks * GATE_ROWS, M_CHUNK)),
        "g_head": jnp.broadcast_to(g_head[l][:, None], (C_WIDTH, M_CHUNK)),
        "w_out": w_out[l].astype(BF16),
    }


def kernel(x, c, ctx, c_ctx, w_ada, b_ada, g_norm, w_in, w_s, b_s, g_q, g_k, b_gates, g_head, w_out,
           g_final):
    b, seq, d = x.shape
    ctx_len = ctx.shape[1]
    depth = w_ada.shape[0]
    tm = 256 if (ctx_len % 256 == 0 and seq % 256 == 0) else CHUNK
    n_ctx_tiles = ctx_len // tm
    n_ctx_chunks = ctx_len // M_CHUNK

    n_mod = b + 1
    rows = -(-n_mod // 8) * 8
    cc = jnp.zeros((rows, d), F32).at[:b].set(c).at[b].set(c_ctx)
    mods = _modulation(cc, w_ada, b_ada)[:, :n_mod]
    mods = mods.reshape(depth, n_mod, 3, 1, d)

    cos, sin = _rope_tables(ctx_len, seq)
    seg = np.kron(np.eye(B_HEADS, dtype=np.float32), np.full((HEAD_DIM, HEAD_DIM), 1.0 / HEAD_DIM, np.float32))
    consts = {"cos": cos, "sin": sin, "pm": jnp.asarray(seg, BF16)}
    gf = g_final.reshape(1, d)

    src = (ctx, x)
    mix = None
    for l in range(depth + 1):
        lw = mod = None
        if l < depth:
            lw = _layer_weights(l, (ctx_len + seq) // M_CHUNK, g_norm, w_in, w_s, b_s, g_q, g_k, b_gates,
                                g_head, w_out)
            sh, sc, gt = mods[l, :, 0], mods[l, :, 1], mods[l, :, 2]
            mod = (sc, sh)
        outs = _layer_call(src, mix, lw, mod, consts, gf, n_ctx_tiles, tm)
        if mix is not None:
            src = (outs[0],)
            outs = outs[1:]
        if l == depth:
            return src[0]
        (ya, q, k, vt, zb, cqt, ck, cvt, og, gr) = outs
        bound = (HEAD_DIM * ATTN_SCALE) * jnp.max(jnp.abs(g_q[l])) * jnp.max(jnp.abs(g_k[l]))
        attn = functools.partial(_attn_call, n_ctx_tiles=n_ctx_tiles, tq=tm, with_ctx=l < depth - 1)
        yb = lax.cond(bound <= SAFE_LOGIT_BOUND, functools.partial(attn, stabilise=False),
                      functools.partial(attn, stabilise=True), q, k, vt, zb)
        yc = _mlstm_call(cqt, ck, cvt, gr, og, lw, n_ctx_chunks)
        mix = (gt, ya, yb, yc, lw["w_out"])
```

```python
import functools

import jax
import jax.numpy as jnp
import numpy as np
from jax import lax
from jax.experimental import pallas as pl
from jax.experimental.pallas import tpu as pltpu

HEAD_DIM = 64
LANES = 128
GRID_W = 64
EPS = 1e-6
A_GROUPS = 4
A_WIDTH = A_GROUPS * HEAD_DIM
CHUNK = 128
B_HEADS = 8
B_KV_HEADS = 2
B_GROUP = B_HEADS // B_KV_HEADS
B_WIDTH = B_HEADS * HEAD_DIM
B_KV_WIDTH = B_KV_HEADS * HEAD_DIM
ATTN_SCALE = HEAD_DIM ** -0.5
ROPE_THETA = 10000.0
ROPE_FREQS = HEAD_DIM // 4
C_HEADS = 4
C_WIDTH = C_HEADS * HEAD_DIM
M_CHUNK = 128
GATE_ROWS = 8

OFF_AU = 0
OFF_AV = OFF_AU + A_WIDTH
OFF_AZ = OFF_AV + A_WIDTH
OFF_BQ = OFF_AZ + A_WIDTH
OFF_BK = OFF_BQ + B_WIDTH
OFF_BV = OFF_BK + B_KV_WIDTH
OFF_BZ = OFF_BV + B_KV_WIDTH
OFF_CQ = OFF_BZ + B_WIDTH
OFF_CK = OFF_CQ + C_WIDTH
OFF_CV = OFF_CK + C_WIDTH
OFF_CO = OFF_CV + C_WIDTH
OFF_CZ = OFF_CO + C_WIDTH
OFF_CG = OFF_CZ + C_WIDTH

VMEM_LIMIT_BYTES = 56 * 1024 * 1024

F32 = jnp.float32
BF16 = jnp.bfloat16
GATE_DTYPE = BF16

LOG2E = float(np.log2(np.e))
QK_PRESCALE = ATTN_SCALE * LOG2E
SAFE_LOGIT_BOUND = 40.0


def _dot(a, b):
    return jnp.dot(a, b, preferred_element_type=F32)


def _dot_nt(a, b):
    return lax.dot_general(a, b, (((1,), (1,)), ((), ())), preferred_element_type=F32)


def _silu(x):
    return x * jax.nn.sigmoid(x)


def _gelu_tanh(x):
    c = np.float32(np.sqrt(2.0 / np.pi))
    return 0.5 * x * (1.0 + jnp.tanh(c * (x + np.float32(0.044715) * (x * x * x))))


def _log_sigmoid(x):
    return jnp.minimum(x, 0.0) - jnp.log1p(jnp.exp(-jnp.abs(x)))


def _seg_mean(x, pm_ref):
    w = x.shape[-1]
    pm = pm_ref[0:w, 0:w]
    return _dot(x.astype(BF16), pm)


def _lane_iota(shape):
    return lax.broadcasted_iota(jnp.int32, shape, len(shape) - 1)


def _mod_kernel(cc_ref, w_ref, b_ref, o_ref):
    s = _silu(cc_ref[...])
    o_ref[0] = _dot(s.astype(BF16), w_ref[0].astype(BF16)) + b_ref[0]


def _modulation(cc, w_ada, b_ada):
    depth, d, d3 = w_ada.shape
    rows = cc.shape[0]
    tn = d
    return pl.pallas_call(
        _mod_kernel,
        grid=(depth, d3 // tn),
        in_specs=[
            pl.BlockSpec((rows, d), lambda l, n: (0, 0)),
            pl.BlockSpec((1, d, tn), lambda l, n: (l, 0, n)),
            pl.BlockSpec((1, 1, tn), lambda l, n: (l, 0, n)),
        ],
        out_specs=pl.BlockSpec((1, rows, tn), lambda l, n: (l, 0, n)),
        out_shape=jax.ShapeDtypeStruct((depth, rows, d3), F32),
        compiler_params=pltpu.CompilerParams(
            dimension_semantics=("arbitrary", "arbitrary"), vmem_limit_bytes=VMEM_LIMIT_BYTES),
        name="modulation",
    )(cc, w_ada, b_ada.reshape(depth, 1, d3))


def _proj_body(x, sc, sh, gn_ref, w_ref, wgr_ref, pm_ref, ws_ref, bs_ref,
               gq_ref, gk_ref, cos_ref, sin_ref,
               ya_ref, q_ref, k_ref, vt_ref, zb_ref, cqt_ref, ck_ref, cvt_ref, og_ref, gr_ref):
    tm = x.shape[0]
    ms = jnp.mean(x * x, axis=-1, keepdims=True)
    xn = (x * lax.rsqrt(ms + EPS)) * (gn_ref[...] * (1.0 + sc)) + sh
    xb = xn.astype(BF16)

    mixers = ((OFF_AU, OFF_BQ), (OFF_BQ, OFF_CQ), (OFF_CQ, OFF_CG))
    wide = {}

    def proj(off, width):
        for lo, hi in mixers:
            if lo <= off < hi:
                if lo not in wide:
                    wide[lo] = _dot(xb, w_ref[:, lo:hi])
                return wide[lo][:, off - lo:off - lo + width]
        raise ValueError(off)

    lane = _lane_iota((tm, LANES))
    lo_half = lane < HEAD_DIM

    gu = _gelu_tanh(proj(OFF_AU, A_WIDTH))
    gv = _gelu_tanh(proj(OFF_AV, A_WIDTH))
    mu = _seg_mean(gv, pm_ref)
    dv = gv - mu
    var = _seg_mean(dv * dv, pm_ref)
    vn = (dv * lax.rsqrt(var + EPS)).astype(BF16)
    sz = _silu(proj(OFF_AZ, A_WIDTH))
    lo_half_chunk = _lane_iota((CHUNK, LANES)) < HEAD_DIM
    for c in range(tm // CHUNK):
        r0, r1 = c * CHUNK, (c + 1) * CHUNK
        for j in range(A_WIDTH // LANES):
            c0, c1 = j * LANES, (j + 1) * LANES
            vs = vn[r0:r1, c0:c1]
            stacked = jnp.concatenate([jnp.where(lo_half_chunk, vs, 0.0), jnp.where(lo_half_chunk, 0.0, vs)],
                                      axis=0).astype(BF16)
            sv = _dot(ws_ref[j], stacked) + bs_ref[:, c0:c1]
            ya_ref[0, r0:r1, c0:c1] = (gu[r0:r1, c0:c1] * sv * sz[r0:r1, c0:c1]).astype(BF16)

    cos = cos_ref[...]
    sin = sin_ref[...]
    bit4 = (lane & ROPE_FREQS) != 0

    def rope(t):
        partner = jnp.where(bit4, pltpu.roll(t, ROPE_FREQS, 1), pltpu.roll(t, LANES - ROPE_FREQS, 1))
        return t * cos + partner * sin

    half_w = B_WIDTH // 2
    q_halves = []
    for off in (0, half_w):
        qh = proj(OFF_BQ + off, half_w)
        q_halves.append(qh * lax.rsqrt(_seg_mean(qh * qh, pm_ref) + EPS) * gq_ref[:, off:off + half_w])
    for j in range(B_WIDTH // LANES):
        jj = j % (half_w // LANES)
        qs = rope(q_halves[j // (half_w // LANES)][:, jj * LANES:(jj + 1) * LANES]) * QK_PRESCALE
        sw = pltpu.roll(qs, HEAD_DIM, 1)
        if (2 * j) // B_GROUP == 0:
            h_even = jnp.where(lo_half, qs, 0.0)
            h_odd = jnp.where(lo_half, sw, 0.0)
        else:
            h_even = jnp.where(lo_half, 0.0, sw)
            h_odd = jnp.where(lo_half, 0.0, qs)
        q_ref[0, :, (2 * j) * LANES:(2 * j + 1) * LANES] = h_even.astype(BF16)
        q_ref[0, :, (2 * j + 1) * LANES:(2 * j + 2) * LANES] = h_odd.astype(BF16)

    k = proj(OFF_BK, B_KV_WIDTH)
    kn = k * lax.rsqrt(_seg_mean(k * k, pm_ref) + EPS) * gk_ref[...]
    k_ref[0] = rope(kn).astype(BF16)
    vt_ref[0] = proj(OFF_BV, B_KV_WIDTH).T.astype(BF16)
    zb_ref[0] = _silu(proj(OFF_BZ, B_WIDTH)).astype(zb_ref.dtype)

    cq = proj(OFF_CQ, C_WIDTH)
    ck = proj(OFF_CK, C_WIDTH) * (HEAD_DIM ** -0.5)
    cv = proj(OFF_CV, C_WIDTH)
    for j in range(C_WIDTH // LANES):
        sl = slice(j * LANES, (j + 1) * LANES)
        for par in range(2):
            hd = 2 * j + par
            h_sl = slice(hd * LANES, (hd + 1) * LANES)

            def head_tile(src, pad):
                s = src[:, sl] if par == 0 else pltpu.roll(src[:, sl], HEAD_DIM, 1)
                return jnp.where(lo_half, s, pad)

            ck_ref[0, :, h_sl] = head_tile(ck, 0.0).astype(BF16)
            qt = head_tile(cq, 0.0)
            vt = head_tile(cv, 1.0)
            for c in range(tm // CHUNK):
                cqt_ref[0, c, h_sl, :] = qt[c * CHUNK:(c + 1) * CHUNK, :].T.astype(BF16)
                cvt_ref[0, c, h_sl, :] = vt[c * CHUNK:(c + 1) * CHUNK, :].T.astype(BF16)
    og = jax.nn.sigmoid(proj(OFF_CO, C_WIDTH)) * _silu(proj(OFF_CZ, C_WIDTH))
    og_ref[0] = og.astype(og_ref.dtype)
    gr = _dot_nt(wgr_ref[...], xb)
    for c in range(tm // CHUNK):
        for ty in range(4):
            gr_ref[0, ty, c * GATE_ROWS:(c + 1) * GATE_ROWS, :] = (
                gr[ty * GATE_ROWS:(ty + 1) * GATE_ROWS, c * CHUNK:(c + 1) * CHUNK])


N_PROJ_IN = 10
SUB_TILES = 3
FINAL_SUB_TILES = (4, 2, 1)
ATTN_SUB_TILES = (4, 3, 2, 1)
MLSTM_STEP_UNROLL = 9
MLSTM_COMBINE_UNROLL = 6


def _layer_kernel(*refs, split_src, tile_inputs, mix, proj, final, n_ctx_tiles, n_sub, tm, tile_off):
    refs = list(refs)
    n_in = n_sub if tile_inputs else 1

    def take(n):
        return [refs.pop(0) for _ in range(n)]

    def tile(blocks, j):
        return blocks[j][0] if tile_inputs else blocks[0][0, j * tm:(j + 1) * tm, :]

    if split_src:
        ctx_src = take(min(n_sub, n_ctx_tiles))
        lat_src = take(n_sub)
    else:
        h_in = take(n_in)
    if mix:
        gt_lat, gt_ctx = take(2)
        ya_in, yb_in, yc_in = take(n_in), take(n_in), take(n_in)
        w_ref = refs.pop(0)
    if final:
        gf_ref = refs.pop(0)
    if proj:
        sc_lat, sc_ctx, sh_lat, sh_ctx = (refs.pop(0) for _ in range(4))
        gn_ref, w_in_ref, wgr_ref, pm_ref, ws_ref, bs_ref, gq_ref, gk_ref, cos_ref, sin_ref = (
            refs.pop(0) for _ in range(N_PROJ_IN))
    if mix:
        h_out = refs.pop(0)
    n_chunks = tm // CHUNK

    for j in range(n_sub):
        rs = slice(j * tm, (j + 1) * tm)
        is_ctx = (pl.program_id(1) + tile_off) * n_sub + j < n_ctx_tiles
        if split_src:
            x = lat_src[j][0]
            if j < len(ctx_src):
                x = jnp.where(is_ctx, ctx_src[j][0], x)
        else:
            x = tile(h_in, j)
        if mix:
            acc = _dot(tile(ya_in, j), w_ref[0:A_WIDTH, :])
            acc += _dot(tile(yb_in, j), w_ref[A_WIDTH:A_WIDTH + B_WIDTH, :])
            acc += _dot(tile(yc_in, j), w_ref[A_WIDTH + B_WIDTH:, :])
            x = x + jnp.where(is_ctx, gt_ctx[0], gt_lat[0]) * acc
        if final:
            ms = jnp.mean(x * x, axis=-1, keepdims=True)
            x = (x * lax.rsqrt(ms + EPS)) * gf_ref[...]
        if mix:
            h_out[0, rs, :] = x
        if proj:
            ya, q, k, vt, zb, cqt, ck, cvt, og, gr = refs
            cs = slice(j * n_chunks, (j + 1) * n_chunks)
            _proj_body(
                x, jnp.where(is_ctx, sc_ctx[0], sc_lat[0]), jnp.where(is_ctx, sh_ctx[0], sh_lat[0]),
                gn_ref, w_in_ref, wgr_ref, pm_ref, ws_ref, bs_ref, gq_ref, gk_ref,
                cos_ref.at[rs, :], sin_ref.at[rs, :],
                ya.at[:, rs, :], q.at[:, rs, :], k.at[:, rs, :], vt.at[:, :, rs], zb.at[:, rs, :],
                cqt.at[:, cs], ck.at[:, rs, :], cvt.at[:, cs], og.at[:, rs, :],
                gr.at[:, :, j * n_chunks * GATE_ROWS:(j + 1) * n_chunks * GATE_ROWS, :])


def _layer_call(src, mix, lw, mod, consts, g_final, n_ctx_tiles, tm):
    split_src = len(src) == 2
    final = lw is None
    b, d = src[0].shape[0], src[0].shape[2]
    t = src[0].shape[1] + src[1].shape[1] if split_src else src[0].shape[1]
    off = n_ctx_tiles if final else 0
    n_tiles = t // tm - off
    if final:
        n_sub = next(n for n in FINAL_SUB_TILES if n_tiles % n == 0)
    else:
        n_sub = SUB_TILES if n_tiles % SUB_TILES == 0 else 1
    tile_inputs = final and n_sub > 1
    tb = tm * n_sub
    nt = n_tiles // n_sub

    def tile3(w, row_off=off):
        if tile_inputs:
            return [pl.BlockSpec((1, tm, w), functools.partial(
                lambda bi, i, j: (bi, i * n_sub + j + row_off, 0), j=j)) for j in range(n_sub)]
        return [pl.BlockSpec((1, tb, w), lambda bi, i: (bi, i + row_off, 0))]

    n_in = n_sub if tile_inputs else 1

    def whole(a):
        nd = a.ndim
        return pl.BlockSpec(a.shape, lambda bi, i: (0,) * nd)

    def mod_specs(a):
        n_mod = a.shape[0]
        return [pl.BlockSpec((1, 1, d), lambda bi, i: (bi, 0, 0)),
                pl.BlockSpec((1, 1, d), lambda bi, i: (n_mod - 1, 0, 0))]

    if split_src:
        ctx, x = src
        n_ctx_src = min(n_sub, n_ctx_tiles)
        ins = [ctx] * n_ctx_src + [x] * n_sub
        in_specs = [pl.BlockSpec((1, tm, d), functools.partial(
            lambda bi, i, j: (bi, jnp.minimum(i * n_sub + j, n_ctx_tiles - 1), 0), j=j))
            for j in range(n_ctx_src)]
        in_specs += [pl.BlockSpec((1, tm, d), functools.partial(
            lambda bi, i, j: (bi, jnp.maximum(i * n_sub + j - n_ctx_tiles, 0), 0), j=j))
            for j in range(n_sub)]
    else:
        ins, in_specs = [src[0]] * n_in, tile3(d)
    out_shape, out_specs, aliases = [], [], {}
    if mix is not None:
        gt, ya, yb, yc, w_out = mix
        yb_specs = tile3(B_WIDTH) if yb.shape[1] == t else tile3(B_WIDTH, row_off=0)
        ins += [gt, gt] + [ya] * n_in + [yb] * n_in + [yc] * n_in + [w_out]
        in_specs += mod_specs(gt) + tile3(A_WIDTH) + yb_specs + tile3(C_WIDTH) + [whole(w_out)]
        if final:
            ins.append(g_final)
            in_specs.append(whole(g_final))
            out_shape.append(jax.ShapeDtypeStruct((b, t - off * tm, d), F32))
            out_specs.append(pl.BlockSpec((1, tb, d), lambda bi, i: (bi, i, 0)))
        else:
            out_shape.append(jax.ShapeDtypeStruct((b, t, d), F32))
            out_specs += tile3(d)
            if not split_src:
                aliases = {0: 0}
    if not final:
        sc, sh = mod
        ins += [sc, sc, sh, sh, lw["g_norm"], lw["w_main"], lw["w_gr"], consts["pm"], lw["w_s"],
                lw["b_s"], lw["g_q"], lw["g_k"], consts["cos"], consts["sin"]]
        in_specs += mod_specs(sc) + mod_specs(sh) + [
            whole(lw["g_norm"]), whole(lw["w_main"]), whole(lw["w_gr"]),
            whole(consts["pm"]), whole(lw["w_s"]), whole(lw["b_s"]), whole(lw["g_q"]),
            whole(lw["g_k"]),
            pl.BlockSpec((tb, LANES), lambda bi, i: (i, 0)),
            pl.BlockSpec((tb, LANES), lambda bi, i: (i, 0))]
        pad_w = C_HEADS * LANES
        out_shape += [
            jax.ShapeDtypeStruct((b, t, A_WIDTH), BF16),
            jax.ShapeDtypeStruct((b, t, B_HEADS * LANES), BF16),
            jax.ShapeDtypeStruct((b, t, B_KV_WIDTH), BF16),
            jax.ShapeDtypeStruct((b, B_KV_WIDTH, t), BF16),
            jax.ShapeDtypeStruct((b, t, B_WIDTH), GATE_DTYPE),
            jax.ShapeDtypeStruct((b, t // CHUNK, pad_w, CHUNK), BF16),
            jax.ShapeDtypeStruct((b, t, pad_w), BF16),
            jax.ShapeDtypeStruct((b, t // CHUNK, pad_w, CHUNK), BF16),
            jax.ShapeDtypeStruct((b, t, C_WIDTH), GATE_DTYPE),
            jax.ShapeDtypeStruct((b, 4, t // CHUNK * GATE_ROWS, CHUNK), F32),
        ]
        out_specs += (tile3(A_WIDTH) + tile3(B_HEADS * LANES) + tile3(B_KV_WIDTH)
                      + [pl.BlockSpec((1, B_KV_WIDTH, tb), lambda bi, i: (bi, 0, i))]
                      + tile3(B_WIDTH)
                      + [pl.BlockSpec((1, tb // CHUNK, pad_w, CHUNK), lambda bi, i: (bi, i, 0, 0))]
                      + tile3(pad_w)
                      + [pl.BlockSpec((1, tb // CHUNK, pad_w, CHUNK), lambda bi, i: (bi, i, 0, 0))]
                      + tile3(C_WIDTH)
                      + [pl.BlockSpec((1, 4, tb // CHUNK * GATE_ROWS, CHUNK), lambda bi, i: (bi, 0, i, 0))])
    kern = functools.partial(_layer_kernel, split_src=split_src, tile_inputs=tile_inputs,
                             mix=mix is not None, proj=not final, final=final, n_ctx_tiles=n_ctx_tiles,
                             n_sub=n_sub, tm=tm, tile_off=off)
    return pl.pallas_call(
        kern,
        grid=(b, nt),
        in_specs=in_specs,
        out_specs=out_specs,
        out_shape=out_shape,
        input_output_aliases=aliases,
        compiler_params=pltpu.CompilerParams(
            dimension_semantics=("arbitrary", "arbitrary"), vmem_limit_bytes=VMEM_LIMIT_BYTES),
        name="layer_tile",
    )(*ins)


def _attn_kernel(*refs, n_sub, n_ctx_tiles, tile_off, ctx_len, stabilise):
    q_refs = refs[0:n_sub]
    k_ref, vt_ref = refs[n_sub:n_sub + 2]
    zb_refs = refs[n_sub + 2:2 * n_sub + 2]
    y_ref = refs[2 * n_sub + 2]
    t_all = k_ref.shape[1]
    tq = q_refs[0].shape[1]

    def attend(sub, n_keys):
        q_ref, zb_ref = q_refs[sub], zb_refs[sub]
        rs = slice(sub * tq, (sub + 1) * tq)
        kk = k_ref[0, 0:n_keys, :]
        vt = vt_ref[0, :, 0:n_keys]
        for g in range(B_KV_HEADS):
            q_grp = jnp.concatenate(
                [q_ref[0, :, h * LANES:(h + 1) * LANES] for h in range(g * B_GROUP, (g + 1) * B_GROUP)],
                axis=0)
            s = _dot_nt(kk, q_grp)
            if stabilise:
                s = s - jnp.max(s, axis=0, keepdims=True)
            p = jnp.exp2(s)
            l = jnp.sum(p, axis=0, keepdims=True)
            o = _dot(vt, p.astype(BF16))[g * HEAD_DIM:(g + 1) * HEAD_DIM, :] / l
            for jj in range(B_GROUP // 2):
                pair = jnp.concatenate([o[:, (2 * jj) * tq:(2 * jj + 1) * tq],
                                        o[:, (2 * jj + 1) * tq:(2 * jj + 2) * tq]], axis=0).T
                j = g * (B_GROUP // 2) + jj
                sl = slice(j * LANES, (j + 1) * LANES)
                y_ref[0, rs, sl] = (pair * zb_ref[0, :, sl].astype(F32)).astype(BF16)

    for sub in range(n_sub):
        tile = pl.program_id(1) * n_sub + sub + tile_off
        if tile_off == 0 and sub < n_ctx_tiles:
            pl.when(tile < n_ctx_tiles)(functools.partial(attend, sub, ctx_len))
            pl.when(tile >= n_ctx_tiles)(functools.partial(attend, sub, t_all))
        else:
            attend(sub, t_all)


def _attn_call(q, k, vt, zb, *, n_ctx_tiles, tq, with_ctx, stabilise):
    b, t, _ = q.shape
    tile_off = 0 if with_ctx else n_ctx_tiles
    n_tiles = t // tq - tile_off
    n_sub = next(n for n in ATTN_SUB_TILES if n_tiles % n == 0)
    kern = functools.partial(_attn_kernel, n_sub=n_sub, n_ctx_tiles=n_ctx_tiles, tile_off=tile_off,
                             ctx_len=n_ctx_tiles * tq, stabilise=stabilise)

    def tile_spec(w, sub):
        return pl.BlockSpec((1, tq, w), lambda bi, i: (bi, i * n_sub + sub + tile_off, 0))

    return pl.pallas_call(
        kern,
        grid=(b, n_tiles // n_sub),
        in_specs=([tile_spec(B_HEADS * LANES, sub) for sub in range(n_sub)]
                  + [pl.BlockSpec((1, t, B_KV_WIDTH), lambda bi, i: (bi, 0, 0)),
                     pl.BlockSpec((1, B_KV_WIDTH, t), lambda bi, i: (bi, 0, 0))]
                  + [tile_spec(B_WIDTH, sub) for sub in range(n_sub)]),
        out_specs=pl.BlockSpec((1, tq * n_sub, B_WIDTH), lambda bi, i: (bi, i, 0)),
        out_shape=jax.ShapeDtypeStruct((b, t - tile_off * tq, B_WIDTH), BF16),
        compiler_params=pltpu.CompilerParams(
            dimension_semantics=("arbitrary", "arbitrary"), vmem_limit_bytes=VMEM_LIMIT_BYTES),
        name="attention",
    )(*([q] * n_sub + [k, vt] + [zb] * n_sub))


def _mlstm_kernel(cqt_ref, ck_ref, cvt_ref, gr_ref, og_ref, br_ref, gh_ref,
                  y_ref, hf_ref, hb_ref, st_ref, m_ref, gp_ref, sel_ref, madd_ref, *, n_ctx_chunks):
    nc = cqt_ref.shape[1]
    L = M_CHUNK
    n_units = 2 * C_HEADS
    W = n_units * L
    n_rows = nc * GATE_ROWS
    n_spread = 2 * GATE_ROWS
    row = lax.broadcasted_iota(jnp.int32, (L, L), 0)
    col = lax.broadcasted_iota(jnp.int32, (L, L), 1)
    tri_row = tuple(jnp.where(mk, 1.0, 0.0).astype(BF16) for mk in (row <= col, row >= col))
    eye_bf = jnp.where(row == col, 1.0, 0.0).astype(BF16)
    @pl.when(pl.program_id(0) == 0)
    def _():
        srow = lax.broadcasted_iota(jnp.int32, (L, W), 0)
        scol = lax.broadcasted_iota(jnp.int32, (L, W), 1)
        sel_ref[...] = jnp.where((srow < 3 * n_spread)
                                 & (((srow // GATE_ROWS) % 2) == (scol // (C_HEADS * L)))
                                 & ((srow % GATE_ROWS) == ((scol // L) % C_HEADS)), 1.0, 0.0).astype(BF16)
        upward = scol < C_HEADS * L
        hidden = (upward & (srow > (scol % L))) | (jnp.logical_not(upward) & (srow < (scol % L)))
        madd_ref[...] = jnp.where(hidden, -jnp.inf, 0.0).astype(F32)

    st_ref[...] = jnp.zeros(st_ref.shape, F32)
    m_ref[...] = jnp.zeros(m_ref.shape, F32)

    lane_r = _lane_iota((n_rows, L))
    li, b, bend = [], [], []
    for d in range(2):
        li.append((gr_ref[0, 2 * d] + br_ref[2 * d]) * LOG2E)
        lf = _log_sigmoid(gr_ref[0, 2 * d + 1] + br_ref[2 * d + 1]) * LOG2E
        lf_hi = lf.astype(BF16)
        lf_r1 = lf - lf_hi.astype(F32)
        lf_mid = lf_r1.astype(BF16)
        lf_lo = (lf_r1 - lf_mid.astype(F32)).astype(BF16)
        b.append(_dot(lf_hi, tri_row[d]) + _dot(lf_mid, tri_row[d]) + _dot(lf_lo, tri_row[d]))
        last = L - 1 if d == 0 else 0
        bend.append(jnp.broadcast_to(b[d][:, last:last + 1], (n_rows, L)))
    g = [li[d] - b[d] for d in range(2)]
    cm = list(g)
    sh = 1
    while sh < L:
        cm[0] = jnp.maximum(cm[0], jnp.where(lane_r >= sh, pltpu.roll(cm[0], sh, 1), -jnp.inf))
        cm[1] = jnp.maximum(cm[1], jnp.where(lane_r < L - sh, pltpu.roll(cm[1], L - sh, 1), -jnp.inf))
        sh *= 2
    for d in range(2):
        w_end = bend[d] - b[d] + li[d]
        mx = jnp.max(w_end, axis=1, keepdims=True)
        gp_ref[d, 0] = g[d]
        gp_ref[d, 1] = cm[d]
        gp_ref[d, 2] = -(b[d] + cm[d])
        gp_ref[d, 3] = jnp.exp2(w_end - mx)
        gp_ref[d, 4] = bend[d]
        gp_ref[d, 5] = jnp.broadcast_to(mx, (n_rows, L))

    def unit_lanes(per_dir):
        return jnp.concatenate([per_dir[d][h:h + 1, :] for d in range(2) for h in range(C_HEADS)], axis=1)

    def chunks_of(i):
        return (i, jnp.where(i < n_ctx_chunks, n_ctx_chunks - 1 - i, nc + n_ctx_chunks - 1 - i))

    def gate_rows(chunks):
        return tuple(pl.ds(pl.multiple_of(c * GATE_ROWS, GATE_ROWS), GATE_ROWS) for c in chunks)

    def step(i, carry):
        chunks = chunks_of(i)
        rows = tuple(pl.ds(pl.multiple_of(c * L, L), L) for c in chunks)
        grow = gate_rows(chunks)
        g, cm, nbc, wk0, bend, mx = (tuple(gp_ref[d, k, grow[d], :] for d in range(2)) for k in range(6))
        m_old = (m_ref[0], m_ref[1])

        decay, wk = [], []
        for d in range(2):
            m_new = jnp.maximum(bend[d] + m_old[d], mx[d])
            decay.append(jnp.exp2(bend[d] + m_old[d] - m_new))
            wk.append(wk0[d] * jnp.exp2(mx[d] - m_new))
            m_ref[d] = m_new

        r16 = jnp.concatenate([g[0], g[1]], axis=0)
        hi = r16.astype(BF16)
        r1 = r16 - hi.astype(F32)
        mid = r1.astype(BF16)
        lo = (r1 - mid.astype(F32)).astype(BF16)
        pieces = jnp.concatenate([hi, mid, lo, jnp.zeros((L - 3 * n_spread, L), BF16)], axis=0)
        pieces_t = _dot_nt(eye_bf, pieces).astype(BF16)
        g_all = _dot(pieces_t, sel_ref[...])

        e0 = jnp.exp2((g_all - unit_lanes(cm)) + madd_ref[...])

        k_d = tuple(ck_ref[0, rows[d], :] for d in range(2))
        qt_d = tuple(cqt_ref[0, chunks[d]] for d in range(2))
        vt_d = tuple(cvt_ref[0, chunks[d]] for d in range(2))
        units = [(d, h) for d in range(2) for h in range(C_HEADS)]
        st_old, qk, bq = [], [], []
        for u, (d, h) in enumerate(units):
            sl = slice(h * LANES, (h + 1) * LANES)
            st = st_ref[u]
            st_old.append(st)
            r = _dot(jnp.concatenate([k_d[d][:, sl], st.astype(BF16)], axis=0), qt_d[d][sl, :])
            qk.append(r[0:L])
            bq.append(r[L:2 * L])
        s0 = (jnp.concatenate(qk, axis=1) * e0).astype(BF16)

        acc, upd = [], []
        for u, (d, h) in enumerate(units):
            sl = slice(h * LANES, (h + 1) * LANES)
            vt = vt_d[d][sl, :]
            vw = (vt.astype(F32) * wk[d][h:h + 1, :]).astype(BF16)
            acc.append(_dot(vt, s0[:, u * L:(u + 1) * L]))
            upd.append(_dot(vw, k_d[d][:, sl]))

        f, inter, thr = [], [], []
        for d in range(2):
            diff = cm[d] - m_old[d]
            below = jnp.minimum(diff, 0.0)
            f.append(jnp.exp2(below))
            inter.append(jnp.exp2(jnp.minimum(-diff, 0.0)))
            thr.append(jnp.exp2(nbc[d] + below))
        na = unit_lanes(f) * jnp.concatenate(acc, axis=1) + unit_lanes(inter) * jnp.concatenate(bq, axis=1)
        den = na[HEAD_DIM:HEAD_DIM + 1, :]
        h_all = na[0:HEAD_DIM, :] * (1.0 / jnp.maximum(jnp.abs(den), unit_lanes(thr)))
        for u, (d, h) in enumerate(units):
            dst = hf_ref if d == 0 else hb_ref
            dst[chunks[d], h * HEAD_DIM:(h + 1) * HEAD_DIM, :] = h_all[:, u * L:(u + 1) * L]

        for u, (d, h) in enumerate(units):
            st_ref[u] = decay[d][h:h + 1, :] * st_old[u] + upd[u]
        return carry

    lax.fori_loop(0, nc, step, 0, unroll=MLSTM_STEP_UNROLL)

    def combine(c, carry):
        rows = pl.ds(pl.multiple_of(c * L, L), L)
        hs = hf_ref[c] + hb_ref[c]
        normed = []
        for hd in range(C_HEADS):
            blk = hs[hd * HEAD_DIM:(hd + 1) * HEAD_DIM, :]
            ms = jnp.sum(blk * blk, axis=0, keepdims=True) * (1.0 / HEAD_DIM)
            normed.append(blk * lax.rsqrt(ms + EPS))
        y = (jnp.concatenate(normed, axis=0) * gh_ref[...]).T
        y_ref[0, rows, :] = (og_ref[0, rows, :].astype(F32) * y).astype(BF16)
        return carry

    lax.fori_loop(0, nc, combine, 0, unroll=MLSTM_COMBINE_UNROLL)


def _mlstm_call(cqt, ck, cvt, gr, og, lw, n_ctx_chunks):
    b, t, pad_w = ck.shape

    def per_b(a):
        nd = a.ndim
        return pl.BlockSpec((1,) + a.shape[1:], lambda bi: (bi,) + (0,) * (nd - 1))

    def whole(a):
        nd = a.ndim
        return pl.BlockSpec(a.shape, lambda bi: (0,) * nd)

    kern = functools.partial(_mlstm_kernel, n_ctx_chunks=n_ctx_chunks)
    return pl.pallas_call(
        kern,
        grid=(b,),
        in_specs=[per_b(cqt), per_b(ck), per_b(cvt), per_b(gr), per_b(og),
                  whole(lw["bg_r"]), whole(lw["g_head"])],
        out_specs=pl.BlockSpec((1, t, C_WIDTH), lambda bi: (bi, 0, 0)),
        out_shape=jax.ShapeDtypeStruct((b, t, C_WIDTH), BF16),
        scratch_shapes=[
            pltpu.VMEM((t // M_CHUNK, C_WIDTH, M_CHUNK), F32),
            pltpu.VMEM((t // M_CHUNK, C_WIDTH, M_CHUNK), F32),
            pltpu.VMEM((2 * C_HEADS, LANES, LANES), F32),
            pltpu.VMEM((2, GATE_ROWS, LANES), F32),
            pltpu.VMEM((2, 6, t // M_CHUNK * GATE_ROWS, M_CHUNK), F32),
            pltpu.VMEM((M_CHUNK, 2 * C_HEADS * M_CHUNK), BF16),
            pltpu.VMEM((M_CHUNK, 2 * C_HEADS * M_CHUNK), F32),
        ],
        compiler_params=pltpu.CompilerParams(
            dimension_semantics=("arbitrary",), vmem_limit_bytes=VMEM_LIMIT_BYTES),
        name="mlstm",
    )(cqt, ck, cvt, gr, og, lw["bg_r"], lw["g_head"])


def _rope_tables(ctx_len, seq):
    pos = np.arange(seq)
    freqs = ROPE_THETA ** (-np.arange(ROPE_FREQS, dtype=np.float32) / ROPE_FREQS)
    ang_r = (pos // GRID_W).astype(np.float32)[:, None] * freqs
    ang_c = (pos % GRID_W).astype(np.float32)[:, None] * freqs
    cos_h = np.concatenate([np.cos(ang_r), np.cos(ang_r), np.cos(ang_c), np.cos(ang_c)], axis=1)
    sin_h = np.concatenate([-np.sin(ang_r), np.sin(ang_r), -np.sin(ang_c), np.sin(ang_c)], axis=1)
    cos = np.concatenate([np.ones((ctx_len, HEAD_DIM), np.float32), cos_h.astype(np.float32)], axis=0)
    sin = np.concatenate([np.zeros((ctx_len, HEAD_DIM), np.float32), sin_h.astype(np.float32)], axis=0)
    return jnp.asarray(np.tile(cos, (1, 2))), jnp.asarray(np.tile(sin, (1, 2)))


def _layer_weights(l, n_chunks, g_norm, w_in, w_s, b_s, g_q, g_k, b_gates, g_head, w_out):
    d = w_in.shape[1]
    w_l = w_in[l]
    w_g = w_l[:, OFF_CG:]
    w_gr = jnp.zeros((4, GATE_ROWS, d), F32).at[:, :C_HEADS, :].set(w_g.T.reshape(4, C_HEADS, d))
    bg_r = jnp.zeros((4, GATE_ROWS), F32).at[:, :C_HEADS].set(b_gates[l].reshape(4, C_HEADS))
    return {
        "g_norm": g_norm[l].reshape(1, d),
        "w_main": w_l[:, :OFF_CG].astype(BF16),
        "w_gr": w_gr.reshape(4 * GATE_ROWS, d).astype(BF16),
        "w_s": jnp.concatenate([w_s[l, 0::2], w_s[l, 1::2]], axis=2).astype(BF16),
        "b_s": jnp.repeat(b_s[l].T, HEAD_DIM, axis=1),
        "g_q": jnp.tile(g_q[l], B_HEADS).reshape(1, B_WIDTH),
        "g_k": jnp.tile(g_k[l], B_KV_HEADS).reshape(1, B_KV_WIDTH),
        "bg_r": jnp.broadcast_to(jnp.tile(bg_r, (1, n_chunks))[:, :, None],
                                 (4, n_chunks * GATE_ROWS, M_CHUNK)),
        "g_head": jnp.broadcast_to(g_head[l][:, None], (C_WIDTH, M_CHUNK)),
        "w_out": w_out[l].astype(BF16),
    }


def kernel(x, c, ctx, c_ctx, w_ada, b_ada, g_norm, w_in, w_s, b_s, g_q, g_k, b_gates, g_head, w_out,
           g_final):
    b, seq, d = x.shape
    ctx_len = ctx.shape[1]
    depth = w_ada.shape[0]
    tm = 256 if (ctx_len % 256 == 0 and seq % 256 == 0) else CHUNK
    n_ctx_tiles = ctx_len // tm
    n_ctx_chunks = ctx_len // M_CHUNK

    n_mod = b + 1
    rows = -(-n_mod // 8) * 8
    cc = jnp.zeros((rows, d), F32).at[:b].set(c).at[b].set(c_ctx)
    mods = _modulation(cc, w_ada, b_ada)[:, :n_mod]
    mods = mods.reshape(depth, n_mod, 3, 1, d)

    cos, sin = _rope_tables(ctx_len, seq)
    seg = np.kron(np.eye(B_HEADS, dtype=np.float32), np.full((HEAD_DIM, HEAD_DIM), 1.0 / HEAD_DIM, np.float32))
    consts = {"cos": cos, "sin": sin, "pm": jnp.asarray(seg, BF16)}
    gf = g_final.reshape(1, d)

    src = (ctx, x)
    mix = None
    for l in range(depth + 1):
        lw = mod = None
        if l < depth:
            lw = _layer_weights(l, (ctx_len + seq) // M_CHUNK, g_norm, w_in, w_s, b_s, g_q, g_k, b_gates,
                                g_head, w_out)
            sh, sc, gt = mods[l, :, 0], mods[l, :, 1], mods[l, :, 2]
            mod = (sc, sh)
        outs = _layer_call(src, mix, lw, mod, consts, gf, n_ctx_tiles, tm)
        if mix is not None:
            src = (outs[0],)
            outs = outs[1:]
        if l == depth:
            return src[0]
        (ya, q, k, vt, zb, cqt, ck, cvt, og, gr) = outs
        bound = (HEAD_DIM * ATTN_SCALE) * jnp.max(jnp.abs(g_q[l])) * jnp.max(jnp.abs(g_k[l]))
        attn = functools.partial(_attn_call, n_ctx_tiles=n_ctx_tiles, tq=tm, with_ctx=l < depth - 1)
        yb = lax.cond(bound <= SAFE_LOGIT_BOUND, functools.partial(attn, stabilise=False),
                      functools.partial(attn, stabilise=True), q, k, vt, zb)
        yc = _mlstm_call(cqt, ck, cvt, gr, og, lw, n_ctx_chunks)
        mix = (gt, ya, yb, yc, lw["w_out"])
```

```python
import functools

import jax
import jax.numpy as jnp
import numpy as np
from jax import lax
from jax.experimental import pallas as pl
from jax.experimental.pallas import tpu as pltpu

HEAD_DIM = 64
LANES = 128
GRID_W = 64
EPS = 1e-6
A_GROUPS = 4
A_WIDTH = A_GROUPS * HEAD_DIM
CHUNK = 128
B_HEADS = 8
B_KV_HEADS = 2
B_GROUP = B_HEADS // B_KV_HEADS
B_WIDTH = B_HEADS * HEAD_DIM
B_KV_WIDTH = B_KV_HEADS * HEAD_DIM
ATTN_SCALE = HEAD_DIM ** -0.5
ROPE_THETA = 10000.0
ROPE_FREQS = HEAD_DIM // 4
C_HEADS = 4
C_WIDTH = C_HEADS * HEAD_DIM
M_CHUNK = 128
GATE_ROWS = 8

OFF_AU = 0
OFF_AV = OFF_AU + A_WIDTH
OFF_AZ = OFF_AV + A_WIDTH
OFF_BQ = OFF_AZ + A_WIDTH
OFF_BK = OFF_BQ + B_WIDTH
OFF_BV = OFF_BK + B_KV_WIDTH
OFF_BZ = OFF_BV + B_KV_WIDTH
OFF_CQ = OFF_BZ + B_WIDTH
OFF_CK = OFF_CQ + C_WIDTH
OFF_CV = OFF_CK + C_WIDTH
OFF_CO = OFF_CV + C_WIDTH
OFF_CZ = OFF_CO + C_WIDTH
OFF_CG = OFF_CZ + C_WIDTH

VMEM_LIMIT_BYTES = 56 * 1024 * 1024

F32 = jnp.float32
BF16 = jnp.bfloat16
GATE_DTYPE = BF16

LOG2E = float(np.log2(np.e))
QK_PRESCALE = ATTN_SCALE * LOG2E
SAFE_LOGIT_BOUND = 40.0


def _dot(a, b):
    return jnp.dot(a, b, preferred_element_type=F32)


def _dot_nt(a, b):
    return lax.dot_general(a, b, (((1,), (1,)), ((), ())), preferred_element_type=F32)


def _silu(x):
    return x * jax.nn.sigmoid(x)


def _gelu_tanh(x):
    c = np.float32(np.sqrt(2.0 / np.pi))
    return 0.5 * x * (1.0 + jnp.tanh(c * (x + np.float32(0.044715) * (x * x * x))))


def _log_sigmoid(x):
    return jnp.minimum(x, 0.0) - jnp.log1p(jnp.exp(-jnp.abs(x)))


def _seg_mean(x, pm_ref):
    w = x.shape[-1]
    pm = pm_ref[0:w, 0:w]
    return _dot(x.astype(BF16), pm)


def _lane_iota(shape):
    return lax.broadcasted_iota(jnp.int32, shape, len(shape) - 1)


def _mod_kernel(cc_ref, w_ref, b_ref, o_ref):
    s = _silu(cc_ref[...])
    o_ref[0] = _dot(s.astype(BF16), w_ref[0].astype(BF16)) + b_ref[0]


def _modulation(cc, w_ada, b_ada):
    depth, d, d3 = w_ada.shape
    rows = cc.shape[0]
    tn = d
    return pl.pallas_call(
        _mod_kernel,
        grid=(depth, d3 // tn),
        in_specs=[
            pl.BlockSpec((rows, d), lambda l, n: (0, 0)),
            pl.BlockSpec((1, d, tn), lambda l, n: (l, 0, n)),
            pl.BlockSpec((1, 1, tn), lambda l, n: (l, 0, n)),
        ],
        out_specs=pl.BlockSpec((1, rows, tn), lambda l, n: (l, 0, n)),
        out_shape=jax.ShapeDtypeStruct((depth, rows, d3), F32),
        compiler_params=pltpu.CompilerParams(
            dimension_semantics=("arbitrary", "arbitrary"), vmem_limit_bytes=VMEM_LIMIT_BYTES),
        name="modulation",
    )(cc, w_ada, b_ada.reshape(depth, 1, d3))


def _proj_body(x, sc, sh, gn_ref, w_ref, wgr_ref, pm_ref, ws_ref, bs_ref,
               gq_ref, gk_ref, cos_ref, sin_ref,
               ya_ref, q_ref, k_ref, vt_ref, zb_ref, cqt_ref, ck_ref, cvt_ref, og_ref, gr_ref):
    tm = x.shape[0]
    ms = jnp.mean(x * x, axis=-1, keepdims=True)
    xn = (x * lax.rsqrt(ms + EPS)) * (gn_ref[...] * (1.0 + sc)) + sh
    xb = xn.astype(BF16)

    mixers = ((OFF_AU, OFF_BQ), (OFF_BQ, OFF_CQ), (OFF_CQ, OFF_CG))
    wide = {}

    def proj(off, width):
        for lo, hi in mixers:
            if lo <= off < hi:
                if lo not in wide:
                    wide[lo] = _dot(xb, w_ref[:, lo:hi])
                return wide[lo][:, off - lo:off - lo + width]
        raise ValueError(off)

    lane = _lane_iota((tm, LANES))
    lo_half = lane < HEAD_DIM

    gu = _gelu_tanh(proj(OFF_AU, A_WIDTH))
    gv = _gelu_tanh(proj(OFF_AV, A_WIDTH))
    mu = _seg_mean(gv, pm_ref)
    dv = gv - mu
    var = _seg_mean(dv * dv, pm_ref)
    vn = (dv * lax.rsqrt(var + EPS)).astype(BF16)
    sz = _silu(proj(OFF_AZ, A_WIDTH))
    lo_half_chunk = _lane_iota((CHUNK, LANES)) < HEAD_DIM
    for c in range(tm // CHUNK):
        r0, r1 = c * CHUNK, (c + 1) * CHUNK
        for j in range(A_WIDTH // LANES):
            c0, c1 = j * LANES, (j + 1) * LANES
            vs = vn[r0:r1, c0:c1]
            stacked = jnp.concatenate([jnp.where(lo_half_chunk, vs, 0.0), jnp.where(lo_half_chunk, 0.0, vs)],
                                      axis=0).astype(BF16)
            sv = _dot(ws_ref[j], stacked) + bs_ref[:, c0:c1]
            ya_ref[0, r0:r1, c0:c1] = (gu[r0:r1, c0:c1] * sv * sz[r0:r1, c0:c1]).astype(BF16)

    cos = cos_ref[...]
    sin = sin_ref[...]
    bit4 = (lane & ROPE_FREQS) != 0

    def rope(t):
        partner = jnp.where(bit4, pltpu.roll(t, ROPE_FREQS, 1), pltpu.roll(t, LANES - ROPE_FREQS, 1))
        return t * cos + partner * sin

    half_w = B_WIDTH // 2
    q_halves = []
    for off in (0, half_w):
        qh = proj(OFF_BQ + off, half_w)
        q_halves.append(qh * lax.rsqrt(_seg_mean(qh * qh, pm_ref) + EPS) * gq_ref[:, off:off + half_w])
    for j in range(B_WIDTH // LANES):
        jj = j % (half_w // LANES)
        qs = rope(q_halves[j // (half_w // LANES)][:, jj * LANES:(jj + 1) * LANES]) * QK_PRESCALE
        sw = pltpu.roll(qs, HEAD_DIM, 1)
        if (2 * j) // B_GROUP == 0:
            h_even = jnp.where(lo_half, qs, 0.0)
            h_odd = jnp.where(lo_half, sw, 0.0)
        else:
            h_even = jnp.where(lo_half, 0.0, sw)
            h_odd = jnp.where(lo_half, 0.0, qs)
        q_ref[0, :, (2 * j) * LANES:(2 * j + 1) * LANES] = h_even.astype(BF16)
        q_ref[0, :, (2 * j + 1) * LANES:(2 * j + 2) * LANES] = h_odd.astype(BF16)

    k = proj(OFF_BK, B_KV_WIDTH)
    kn = k * lax.rsqrt(_seg_mean(k * k, pm_ref) + EPS) * gk_ref[...]
    k_ref[0] = rope(kn).astype(BF16)
    vt_ref[0] = proj(OFF_BV, B_KV_WIDTH).T.astype(BF16)
    zb_ref[0] = _silu(proj(OFF_BZ, B_WIDTH)).astype(zb_ref.dtype)

    cq = proj(OFF_CQ, C_WIDTH)
    ck = proj(OFF_CK, C_WIDTH) * (HEAD_DIM ** -0.5)
    cv = proj(OFF_CV, C_WIDTH)
    for j in range(C_WIDTH // LANES):
        sl = slice(j * LANES, (j + 1) * LANES)
        for par in range(2):
            hd = 2 * j + par
            h_sl = slice(hd * LANES, (hd + 1) * LANES)

            def head_tile(src, pad):
                s = src[:, sl] if par == 0 else pltpu.roll(src[:, sl], HEAD_DIM, 1)
                return jnp.where(lo_half, s, pad)

            ck_ref[0, :, h_sl] = head_tile(ck, 0.0).astype(BF16)
            qt = head_tile(cq, 0.0)
            vt = head_tile(cv, 1.0)
            for c in range(tm // CHUNK):
                cqt_ref[0, c, h_sl, :] = qt[c * CHUNK:(c + 1) * CHUNK, :].T.astype(BF16)
                cvt_ref[0, c, h_sl, :] = vt[c * CHUNK:(c + 1) * CHUNK, :].T.astype(BF16)
    og = jax.nn.sigmoid(proj(OFF_CO, C_WIDTH)) * _silu(proj(OFF_CZ, C_WIDTH))
    og_ref[0] = og.astype(og_ref.dtype)
    gr = _dot_nt(wgr_ref[...], xb)
    for c in range(tm // CHUNK):
        for ty in range(4):
            gr_ref[0, ty, c * GATE_ROWS:(c + 1) * GATE_ROWS, :] = (
                gr[ty * GATE_ROWS:(ty + 1) * GATE_ROWS, c * CHUNK:(c + 1) * CHUNK])


N_PROJ_IN = 10
SUB_TILES = 3
FINAL_SUB_TILES = (4, 2, 1)
ATTN_SUB_TILES = (4, 3, 2, 1)
MLSTM_STEP_UNROLL = 9
MLSTM_COMBINE_UNROLL = 6


def _layer_kernel(*refs, split_src, tile_inputs, mix, proj, final, n_ctx_tiles, n_sub, tm, tile_off):
    refs = list(refs)
    n_in = n_sub if tile_inputs else 1

    def take(n):
        return [refs.pop(0) for _ in range(n)]

    def tile(blocks, j):
        return blocks[j][0] if tile_inputs else blocks[0][0, j * tm:(j + 1) * tm, :]

    if split_src:
        ctx_src = take(min(n_sub, n_ctx_tiles))
        lat_src = take(n_sub)
    else:
        h_in = take(n_in)
    if mix:
        gt_lat, gt_ctx = take(2)
        ya_in, yb_in, yc_in = take(n_in), take(n_in), take(n_in)
        w_ref = refs.pop(0)
    if final:
        gf_ref = refs.pop(0)
    if proj:
        sc_lat, sc_ctx, sh_lat, sh_ctx = (refs.pop(0) for _ in range(4))
        gn_ref, w_in_ref, wgr_ref, pm_ref, ws_ref, bs_ref, gq_ref, gk_ref, cos_ref, sin_ref = (
            refs.pop(0) for _ in range(N_PROJ_IN))
    if mix:
        h_out = refs.pop(0)
    n_chunks = tm // CHUNK

    for j in range(n_sub):
        rs = slice(j * tm, (j + 1) * tm)
        is_ctx = (pl.program_id(1) + tile_off) * n_sub + j < n_ctx_tiles
        if split_src:
            x = lat_src[j][0]
            if j < len(ctx_src):
                x = jnp.where(is_ctx, ctx_src[j][0], x)
        else:
            x = tile(h_in, j)
        if mix:
            acc = _dot(tile(ya_in, j), w_ref[0:A_WIDTH, :])
            acc += _dot(tile(yb_in, j), w_ref[A_WIDTH:A_WIDTH + B_WIDTH, :])
            acc += _dot(tile(yc_in, j), w_ref[A_WIDTH + B_WIDTH:, :])
            x = x + jnp.where(is_ctx, gt_ctx[0], gt_lat[0]) * acc
        if final:
            ms = jnp.mean(x * x, axis=-1, keepdims=True)
            x = (x * lax.rsqrt(ms + EPS)) * gf_ref[...]
        if mix:
            h_out[0, rs, :] = x
        if proj:
            ya, q, k, vt, zb, cqt, ck, cvt, og, gr = refs
            cs = slice(j * n_chunks, (j + 1) * n_chunks)
            _proj_body(
                x, jnp.where(is_ctx, sc_ctx[0], sc_lat[0]), jnp.where(is_ctx, sh_ctx[0], sh_lat[0]),
                gn_ref, w_in_ref, wgr_ref, pm_ref, ws_ref, bs_ref, gq_ref, gk_ref,
                cos_ref.at[rs, :], sin_ref.at[rs, :],
                ya.at[:, rs, :], q.at[:, rs, :], k.at[:, rs, :], vt.at[:, :, rs], zb.at[:, rs, :],
                cqt.at[:, cs], ck.at[:, rs, :], cvt.at[:, cs], og.at[:, rs, :],
                gr.at[:, :, j * n_chunks * GATE_ROWS:(j + 1) * n_chunks * GATE_ROWS, :])


def _layer_call(src, mix, lw, mod, consts, g_final, n_ctx_tiles, tm):
    split_src = len(src) == 2
    final = lw is None
    b, d = src[0].shape[0], src[0].shape[2]
    t = src[0].shape[1] + src[1].shape[1] if split_src else src[0].shape[1]
    off = n_ctx_tiles if final else 0
    n_tiles = t // tm - off
    if final:
        n_sub = next(n for n in FINAL_SUB_TILES if n_tiles % n == 0)
    else:
        n_sub = SUB_TILES if n_tiles % SUB_TILES == 0 else 1
    tile_inputs = final and n_sub > 1
    tb = tm * n_sub
    nt = n_tiles // n_sub

    def tile3(w, row_off=off):
        if tile_inputs:
            return [pl.BlockSpec((1, tm, w), functools.partial(
                lambda bi, i, j: (bi, i * n_sub + j + row_off, 0), j=j)) for j in range(n_sub)]
        return [pl.BlockSpec((1, tb, w), lambda bi, i: (bi, i + row_off, 0))]

    n_in = n_sub if tile_inputs else 1

    def whole(a):
        nd = a.ndim
        return pl.BlockSpec(a.shape, lambda bi, i: (0,) * nd, pipeline_mode=pl.Buffered(1))

    def mod_specs(a):
        n_mod = a.shape[0]
        return [pl.BlockSpec((1, 1, d), lambda bi, i: (bi, 0, 0)),
                pl.BlockSpec((1, 1, d), lambda bi, i: (n_mod - 1, 0, 0))]

    if split_src:
        ctx, x = src
        n_ctx_src = min(n_sub, n_ctx_tiles)
        ins = [ctx] * n_ctx_src + [x] * n_sub
        in_specs = [pl.BlockSpec((1, tm, d), functools.partial(
            lambda bi, i, j: (bi, jnp.minimum(i * n_sub + j, n_ctx_tiles - 1), 0), j=j))
            for j in range(n_ctx_src)]
        in_specs += [pl.BlockSpec((1, tm, d), functools.partial(
            lambda bi, i, j: (bi, jnp.maximum(i * n_sub + j - n_ctx_tiles, 0), 0), j=j))
            for j in range(n_sub)]
    else:
        ins, in_specs = [src[0]] * n_in, tile3(d)
    out_shape, out_specs, aliases = [], [], {}
    if mix is not None:
        gt, ya, yb, yc, w_out = mix
        yb_specs = tile3(B_WIDTH) if yb.shape[1] == t else tile3(B_WIDTH, row_off=0)
        ins += [gt, gt] + [ya] * n_in + [yb] * n_in + [yc] * n_in + [w_out]
        in_specs += mod_specs(gt) + tile3(A_WIDTH) + yb_specs + tile3(C_WIDTH) + [whole(w_out)]
        if final:
            ins.append(g_final)
            in_specs.append(whole(g_final))
            out_shape.append(jax.ShapeDtypeStruct((b, t - off * tm, d), F32))
            out_specs.append(pl.BlockSpec((1, tb, d), lambda bi, i: (bi, i, 0)))
        else:
            out_shape.append(jax.ShapeDtypeStruct((b, t, d), F32))
            out_specs += tile3(d)
            if not split_src:
                aliases = {0: 0}
    if not final:
        sc, sh = mod
        ins += [sc, sc, sh, sh, lw["g_norm"], lw["w_main"], lw["w_gr"], consts["pm"], lw["w_s"],
                lw["b_s"], lw["g_q"], lw["g_k"], consts["cos"], consts["sin"]]
        in_specs += mod_specs(sc) + mod_specs(sh) + [
            whole(lw["g_norm"]), whole(lw["w_main"]), whole(lw["w_gr"]),
            whole(consts["pm"]), whole(lw["w_s"]), whole(lw["b_s"]), whole(lw["g_q"]),
            whole(lw["g_k"]),
            pl.BlockSpec((tb, LANES), lambda bi, i: (i, 0)),
            pl.BlockSpec((tb, LANES), lambda bi, i: (i, 0))]
        pad_w = C_HEADS * LANES
        out_shape += [
            jax.ShapeDtypeStruct((b, t, A_WIDTH), BF16),
            jax.ShapeDtypeStruct((b, t, B_HEADS * LANES), BF16),
            jax.ShapeDtypeStruct((b, t, B_KV_WIDTH), BF16),
            jax.ShapeDtypeStruct((b, B_KV_WIDTH, t), BF16),
            jax.ShapeDtypeStruct((b, t, B_WIDTH), GATE_DTYPE),
            jax.ShapeDtypeStruct((b, t // CHUNK, pad_w, CHUNK), BF16),
            jax.ShapeDtypeStruct((b, t, pad_w), BF16),
            jax.ShapeDtypeStruct((b, t // CHUNK, pad_w, CHUNK), BF16),
            jax.ShapeDtypeStruct((b, t, C_WIDTH), GATE_DTYPE),
            jax.ShapeDtypeStruct((b, 4, t // CHUNK * GATE_ROWS, CHUNK), F32),
        ]
        out_specs += (tile3(A_WIDTH) + tile3(B_HEADS * LANES) + tile3(B_KV_WIDTH)
                      + [pl.BlockSpec((1, B_KV_WIDTH, tb), lambda bi, i: (bi, 0, i))]
                      + tile3(B_WIDTH)
                      + [pl.BlockSpec((1, tb // CHUNK, pad_w, CHUNK), lambda bi, i: (bi, i, 0, 0))]
                      + tile3(pad_w)
                      + [pl.BlockSpec((1, tb // CHUNK, pad_w, CHUNK), lambda bi, i: (bi, i, 0, 0))]
                      + tile3(C_WIDTH)
                      + [pl.BlockSpec((1, 4, tb // CHUNK * GATE_ROWS, CHUNK), lambda bi, i: (bi, 0, i, 0))])
    kern = functools.partial(_layer_kernel, split_src=split_src, tile_inputs=tile_inputs,
                             mix=mix is not None, proj=not final, final=final, n_ctx_tiles=n_ctx_tiles,
                             n_sub=n_sub, tm=tm, tile_off=off)
    return pl.pallas_call(
        kern,
        grid=(b, nt),
        in_specs=in_specs,
        out_specs=out_specs,
        out_shape=out_shape,
        input_output_aliases=aliases,
        compiler_params=pltpu.CompilerParams(
            dimension_semantics=("arbitrary", "arbitrary"), vmem_limit_bytes=VMEM_LIMIT_BYTES),
        name="layer_tile",
    )(*ins)


def _attn_kernel(*refs, n_sub, n_ctx_tiles, tile_off, ctx_len, stabilise):
    q_refs = refs[0:n_sub]
    k_ref, vt_ref = refs[n_sub:n_sub + 2]
    zb_refs = refs[n_sub + 2:2 * n_sub + 2]
    y_ref = refs[2 * n_sub + 2]
    t_all = k_ref.shape[1]
    tq = q_refs[0].shape[1]

    def attend(sub, n_keys):
        q_ref, zb_ref = q_refs[sub], zb_refs[sub]
        rs = slice(sub * tq, (sub + 1) * tq)
        kk = k_ref[0, 0:n_keys, :]
        vt = vt_ref[0, :, 0:n_keys]
        for g in range(B_KV_HEADS):
            q_grp = jnp.concatenate(
                [q_ref[0, :, h * LANES:(h + 1) * LANES] for h in range(g * B_GROUP, (g + 1) * B_GROUP)],
                axis=0)
            s = _dot_nt(kk, q_grp)
            if stabilise:
                s = s - jnp.max(s, axis=0, keepdims=True)
            p = jnp.exp2(s)
            l = jnp.sum(p, axis=0, keepdims=True)
            o = _dot(vt, p.astype(BF16))[g * HEAD_DIM:(g + 1) * HEAD_DIM, :] / l
            for jj in range(B_GROUP // 2):
                pair = jnp.concatenate([o[:, (2 * jj) * tq:(2 * jj + 1) * tq],
                                        o[:, (2 * jj + 1) * tq:(2 * jj + 2) * tq]], axis=0).T
                j = g * (B_GROUP // 2) + jj
                sl = slice(j * LANES, (j + 1) * LANES)
                y_ref[0, rs, sl] = (pair * zb_ref[0, :, sl].astype(F32)).astype(BF16)

    for sub in range(n_sub):
        tile = pl.program_id(1) * n_sub + sub + tile_off
        if tile_off == 0 and sub < n_ctx_tiles:
            pl.when(tile < n_ctx_tiles)(functools.partial(attend, sub, ctx_len))
            pl.when(tile >= n_ctx_tiles)(functools.partial(attend, sub, t_all))
        else:
            attend(sub, t_all)


def _attn_call(q, k, vt, zb, *, n_ctx_tiles, tq, with_ctx, stabilise):
    b, t, _ = q.shape
    tile_off = 0 if with_ctx else n_ctx_tiles
    n_tiles = t // tq - tile_off
    n_sub = next(n for n in ATTN_SUB_TILES if n_tiles % n == 0)
    kern = functools.partial(_attn_kernel, n_sub=n_sub, n_ctx_tiles=n_ctx_tiles, tile_off=tile_off,
                             ctx_len=n_ctx_tiles * tq, stabilise=stabilise)

    def tile_spec(w, sub):
        return pl.BlockSpec((1, tq, w), lambda bi, i: (bi, i * n_sub + sub + tile_off, 0))

    return pl.pallas_call(
        kern,
        grid=(b, n_tiles // n_sub),
        in_specs=([tile_spec(B_HEADS * LANES, sub) for sub in range(n_sub)]
                  + [pl.BlockSpec((1, t, B_KV_WIDTH), lambda bi, i: (bi, 0, 0)),
                     pl.BlockSpec((1, B_KV_WIDTH, t), lambda bi, i: (bi, 0, 0))]
                  + [tile_spec(B_WIDTH, sub) for sub in range(n_sub)]),
        out_specs=pl.BlockSpec((1, tq * n_sub, B_WIDTH), lambda bi, i: (bi, i, 0)),
        out_shape=jax.ShapeDtypeStruct((b, t - tile_off * tq, B_WIDTH), BF16),
        compiler_params=pltpu.CompilerParams(
            dimension_semantics=("arbitrary", "arbitrary"), vmem_limit_bytes=VMEM_LIMIT_BYTES),
        name="attention",
    )(*([q] * n_sub + [k, vt] + [zb] * n_sub))


def _mlstm_kernel(cqt_ref, ck_ref, cvt_ref, gr_ref, og_ref, br_ref, gh_ref,
                  y_ref, hf_ref, hb_ref, st_ref, m_ref, gp_ref, sel_ref, madd_ref, *, n_ctx_chunks):
    nc = cqt_ref.shape[1]
    L = M_CHUNK
    n_units = 2 * C_HEADS
    W = n_units * L
    n_rows = nc * GATE_ROWS
    n_spread = 2 * GATE_ROWS
    row = lax.broadcasted_iota(jnp.int32, (L, L), 0)
    col = lax.broadcasted_iota(jnp.int32, (L, L), 1)
    tri_row = tuple(jnp.where(mk, 1.0, 0.0).astype(BF16) for mk in (row <= col, row >= col))
    eye_bf = jnp.where(row == col, 1.0, 0.0).astype(BF16)
    @pl.when(pl.program_id(0) == 0)
    def _():
        srow = lax.broadcasted_iota(jnp.int32, (L, W), 0)
        scol = lax.broadcasted_iota(jnp.int32, (L, W), 1)
        sel_ref[...] = jnp.where((srow < 3 * n_spread)
                                 & (((srow // GATE_ROWS) % 2) == (scol // (C_HEADS * L)))
                                 & ((srow % GATE_ROWS) == ((scol // L) % C_HEADS)), 1.0, 0.0).astype(BF16)
        upward = scol < C_HEADS * L
        hidden = (upward & (srow > (scol % L))) | (jnp.logical_not(upward) & (srow < (scol % L)))
        madd_ref[...] = jnp.where(hidden, -jnp.inf, 0.0).astype(F32)

    st_ref[...] = jnp.zeros(st_ref.shape, F32)
    m_ref[...] = jnp.zeros(m_ref.shape, F32)

    lane_r = _lane_iota((n_rows, L))
    li, b, bend = [], [], []
    for d in range(2):
        li.append((gr_ref[0, 2 * d] + br_ref[2 * d]) * LOG2E)
        lf = _log_sigmoid(gr_ref[0, 2 * d + 1] + br_ref[2 * d + 1]) * LOG2E
        lf_hi = lf.astype(BF16)
        lf_r1 = lf - lf_hi.astype(F32)
        lf_mid = lf_r1.astype(BF16)
        lf_lo = (lf_r1 - lf_mid.astype(F32)).astype(BF16)
        b.append(_dot(lf_hi, tri_row[d]) + _dot(lf_mid, tri_row[d]) + _dot(lf_lo, tri_row[d]))
        last = L - 1 if d == 0 else 0
        bend.append(jnp.broadcast_to(b[d][:, last:last + 1], (n_rows, L)))
    g = [li[d] - b[d] for d in range(2)]
    cm = list(g)
    sh = 1
    while sh < L:
        cm[0] = jnp.maximum(cm[0], jnp.where(lane_r >= sh, pltpu.roll(cm[0], sh, 1), -jnp.inf))
        cm[1] = jnp.maximum(cm[1], jnp.where(lane_r < L - sh, pltpu.roll(cm[1], L - sh, 1), -jnp.inf))
        sh *= 2
    for d in range(2):
        w_end = bend[d] - b[d] + li[d]
        mx = jnp.max(w_end, axis=1, keepdims=True)
        gp_ref[d, 0] = g[d]
        gp_ref[d, 1] = cm[d]
        gp_ref[d, 2] = -(b[d] + cm[d])
        gp_ref[d, 3] = jnp.exp2(w_end - mx)
        gp_ref[d, 4] = bend[d]
        gp_ref[d, 5] = jnp.broadcast_to(mx, (n_rows, L))

    def unit_lanes(per_dir):
        return jnp.concatenate([per_dir[d][h:h + 1, :] for d in range(2) for h in range(C_HEADS)], axis=1)

    def chunks_of(i):
        return (i, jnp.where(i < n_ctx_chunks, n_ctx_chunks - 1 - i, nc + n_ctx_chunks - 1 - i))

    def gate_rows(chunks):
        return tuple(pl.ds(pl.multiple_of(c * GATE_ROWS, GATE_ROWS), GATE_ROWS) for c in chunks)

    def step(i, carry):
        chunks = chunks_of(i)
        rows = tuple(pl.ds(pl.multiple_of(c * L, L), L) for c in chunks)
        grow = gate_rows(chunks)
        g, cm, nbc, wk0, bend, mx = (tuple(gp_ref[d, k, grow[d], :] for d in range(2)) for k in range(6))
        m_old = (m_ref[0], m_ref[1])

        decay, wk = [], []
        for d in range(2):
            m_new = jnp.maximum(bend[d] + m_old[d], mx[d])
            decay.append(jnp.exp2(bend[d] + m_old[d] - m_new))
            wk.append(wk0[d] * jnp.exp2(mx[d] - m_new))
            m_ref[d] = m_new

        r16 = jnp.concatenate([g[0], g[1]], axis=0)
        hi = r16.astype(BF16)
        r1 = r16 - hi.astype(F32)
        mid = r1.astype(BF16)
        lo = (r1 - mid.astype(F32)).astype(BF16)
        pieces = jnp.concatenate([hi, mid, lo, jnp.zeros((L - 3 * n_spread, L), BF16)], axis=0)
        pieces_t = _dot_nt(eye_bf, pieces).astype(BF16)
        g_all = _dot(pieces_t, sel_ref[...])

        e0 = jnp.exp2((g_all - unit_lanes(cm)) + madd_ref[...])

        k_d = tuple(ck_ref[0, rows[d], :] for d in range(2))
        qt_d = tuple(cqt_ref[0, chunks[d]] for d in range(2))
        vt_d = tuple(cvt_ref[0, chunks[d]] for d in range(2))
        units = [(d, h) for d in range(2) for h in range(C_HEADS)]
        st_old, qk, bq = [], [], []
        for u, (d, h) in enumerate(units):
            sl = slice(h * LANES, (h + 1) * LANES)
            st = st_ref[u]
            st_old.append(st)
            r = _dot(jnp.concatenate([k_d[d][:, sl], st.astype(BF16)], axis=0), qt_d[d][sl, :])
            qk.append(r[0:L])
            bq.append(r[L:2 * L])
        s0 = (jnp.concatenate(qk, axis=1) * e0).astype(BF16)

        acc, upd = [], []
        for u, (d, h) in enumerate(units):
            sl = slice(h * LANES, (h + 1) * LANES)
            vt = vt_d[d][sl, :]
            vw = (vt.astype(F32) * wk[d][h:h + 1, :]).astype(BF16)
            acc.append(_dot(vt, s0[:, u * L:(u + 1) * L]))
            upd.append(_dot(vw, k_d[d][:, sl]))

        f, inter, thr = [], [], []
        for d in range(2):
            diff = cm[d] - m_old[d]
            below = jnp.minimum(diff, 0.0)
            f.append(jnp.exp2(below))
            inter.append(jnp.exp2(jnp.minimum(-diff, 0.0)))
            thr.append(jnp.exp2(nbc[d] + below))
        na = unit_lanes(f) * jnp.concatenate(acc, axis=1) + unit_lanes(inter) * jnp.concatenate(bq, axis=1)
        den = na[HEAD_DIM:HEAD_DIM + 1, :]
        h_all = na[0:HEAD_DIM, :] * (1.0 / jnp.maximum(jnp.abs(den), unit_lanes(thr)))
        for u, (d, h) in enumerate(units):
            dst = hf_ref if d == 0 else hb_ref
            dst[chunks[d], h * HEAD_DIM:(h + 1) * HEAD_DIM, :] = h_all[:, u * L:(u + 1) * L]

        for u, (d, h) in enumerate(units):
            st_ref[u] = decay[d][h:h + 1, :] * st_old[u] + upd[u]
        return carry

    lax.fori_loop(0, nc, step, 0, unroll=MLSTM_STEP_UNROLL)

    def combine(c, carry):
        rows = pl.ds(pl.multiple_of(c * L, L), L)
        hs = hf_ref[c] + hb_ref[c]
        normed = []
        for hd in range(C_HEADS):
            blk = hs[hd * HEAD_DIM:(hd + 1) * HEAD_DIM, :]
            ms = jnp.sum(blk * blk, axis=0, keepdims=True) * (1.0 / HEAD_DIM)
            normed.append(blk * lax.rsqrt(ms + EPS))
        y = (jnp.concatenate(normed, axis=0) * gh_ref[...]).T
        y_ref[0, rows, :] = (og_ref[0, rows, :].astype(F32) * y).astype(BF16)
        return carry

    lax.fori_loop(0, nc, combine, 0, unroll=MLSTM_COMBINE_UNROLL)


def _mlstm_call(cqt, ck, cvt, gr, og, lw, n_ctx_chunks):
    b, t, pad_w = ck.shape

    def per_b(a):
        nd = a.ndim
        return pl.BlockSpec((1,) + a.shape[1:], lambda bi: (bi,) + (0,) * (nd - 1))

    def whole(a):
        nd = a.ndim
        return pl.BlockSpec(a.shape, lambda bi: (0,) * nd)

    kern = functools.partial(_mlstm_kernel, n_ctx_chunks=n_ctx_chunks)
    return pl.pallas_call(
        kern,
        grid=(b,),
        in_specs=[per_b(cqt), per_b(ck), per_b(cvt), per_b(gr), per_b(og),
                  whole(lw["bg_r"]), whole(lw["g_head"])],
        out_specs=pl.BlockSpec((1, t, C_WIDTH), lambda bi: (bi, 0, 0)),
        out_shape=jax.ShapeDtypeStruct((b, t, C_WIDTH), BF16),
        scratch_shapes=[
            pltpu.VMEM((t // M_CHUNK, C_WIDTH, M_CHUNK), F32),
            pltpu.VMEM((t // M_CHUNK, C_WIDTH, M_CHUNK), F32),
            pltpu.VMEM((2 * C_HEADS, LANES, LANES), F32),
            pltpu.VMEM((2, GATE_ROWS, LANES), F32),
            pltpu.VMEM((2, 6, t // M_CHUNK * GATE_ROWS, M_CHUNK), F32),
            pltpu.VMEM((M_CHUNK, 2 * C_HEADS * M_CHUNK), BF16),
            pltpu.VMEM((M_CHUNK, 2 * C_HEADS * M_CHUNK), F32),
        ],
        compiler_params=pltpu.CompilerParams(
            dimension_semantics=("arbitrary",), vmem_limit_bytes=VMEM_LIMIT_BYTES),
        name="mlstm",
    )(cqt, ck, cvt, gr, og, lw["bg_r"], lw["g_head"])


def _rope_tables(ctx_len, seq):
    pos = np.arange(seq)
    freqs = ROPE_THETA ** (-np.arange(ROPE_FREQS, dtype=np.float32) / ROPE_FREQS)
    ang_r = (pos // GRID_W).astype(np.float32)[:, None] * freqs
    ang_c = (pos % GRID_W).astype(np.float32)[:, None] * freqs
    cos_h = np.concatenate([np.cos(ang_r), np.cos(ang_r), np.cos(ang_c), np.cos(ang_c)], axis=1)
    sin_h = np.concatenate([-np.sin(ang_r), np.sin(ang_r), -np.sin(ang_c), np.sin(ang_c)], axis=1)
    cos = np.concatenate([np.ones((ctx_len, HEAD_DIM), np.float32), cos_h.astype(np.float32)], axis=0)
    sin = np.concatenate([np.zeros((ctx_len, HEAD_DIM), np.float32), sin_h.astype(np.float32)], axis=0)
    return jnp.asarray(np.tile(cos, (1, 2))), jnp.asarray(np.tile(sin, (1, 2)))


def _layer_weights(l, n_chunks, g_norm, w_in, w_s, b_s, g_q, g_k, b_gates, g_head, w_out):
    d = w_in.shape[1]
    w_l = w_in[l]
    w_g = w_l[:, OFF_CG:]
    w_gr = jnp.zeros((4, GATE_ROWS, d), F32).at[:, :C_HEADS, :].set(w_g.T.reshape(4, C_HEADS, d))
    bg_r = jnp.zeros((4, GATE_ROWS), F32).at[:, :C_HEADS].set(b_gates[l].reshape(4, C_HEADS))
    return {
        "g_norm": g_norm[l].reshape(1, d),
        "w_main": w_l[:, :OFF_CG].astype(BF16),
        "w_gr": w_gr.reshape(4 * GATE_ROWS, d).astype(BF16),
        "w_s": jnp.concatenate([w_s[l, 0::2], w_s[l, 1::2]], axis=2).astype(BF16),
        "b_s": jnp.repeat(b_s[l].T, HEAD_DIM, axis=1),
        "g_q": jnp.tile(g_q[l], B_HEADS).reshape(1, B_WIDTH),
        "g_k": jnp.tile(g_k[l], B_KV_HEADS).reshape(1, B_KV_WIDTH),
        "bg_r": jnp.broadcast_to(jnp.tile(bg_r, (1, n_chunks))[:, :, None],
                                 (4, n_chunks * GATE_ROWS, M_CHUNK)),
        "g_head": jnp.broadcast_to(g_head[l][:, None], (C_WIDTH, M_CHUNK)),
        "w_out": w_out[l].astype(BF16),
    }


def kernel(x, c, ctx, c_ctx, w_ada, b_ada, g_norm, w_in, w_s, b_s, g_q, g_k, b_gates, g_head, w_out,
           g_final):
    b, seq, d = x.shape
    ctx_len = ctx.shape[1]
    depth = w_ada.shape[0]
    tm = 256 if (ctx_len % 256 == 0 and seq % 256 == 0) else CHUNK
    n_ctx_tiles = ctx_len // tm
    n_ctx_chunks = ctx_len // M_CHUNK

    n_mod = b + 1
    rows = -(-n_mod // 8) * 8
    cc = jnp.zeros((rows, d), F32).at[:b].set(c).at[b].set(c_ctx)
    mods = _modulation(cc, w_ada, b_ada)[:, :n_mod]
    mods = mods.reshape(depth, n_mod, 3, 1, d)

    cos, sin = _rope_tables(ctx_len, seq)
    seg = np.kron(np.eye(B_HEADS, dtype=np.float32), np.full((HEAD_DIM, HEAD_DIM), 1.0 / HEAD_DIM, np.float32))
    consts = {"cos": cos, "sin": sin, "pm": jnp.asarray(seg, BF16)}
    gf = g_final.reshape(1, d)

    src = (ctx, x)
    mix = None
    for l in range(depth + 1):
        lw = mod = None
        if l < depth:
            lw = _layer_weights(l, (ctx_len + seq) // M_CHUNK, g_norm, w_in, w_s, b_s, g_q, g_k, b_gates,
                                g_head, w_out)
            sh, sc, gt = mods[l, :, 0], mods[l, :, 1], mods[l, :, 2]
            mod = (sc, sh)
        outs = _layer_call(src, mix, lw, mod, consts, gf, n_ctx_tiles, tm)
        if mix is not None:
            src = (outs[0],)
            outs = outs[1:]
        if l == depth:
            return src[0]
        (ya, q, k, vt, zb, cqt, ck, cvt, og, gr) = outs
        bound = (HEAD_DIM * ATTN_SCALE) * jnp.max(jnp.abs(g_q[l])) * jnp.max(jnp.abs(g_k[l]))
        attn = functools.partial(_attn_call, n_ctx_tiles=n_ctx_tiles, tq=tm, with_ctx=l < depth - 1)
        yb = lax.cond(bound <= SAFE_LOGIT_BOUND, functools.partial(attn, stabilise=False),
                      functools.partial(attn, stabilise=True), q, k, vt, zb)
        yc = _mlstm_call(cqt, ck, cvt, gr, og, lw, n_ctx_chunks)
        mix = (gt, ya, yb, yc, lw["w_out"])
```
